```python
import jax, jax.numpy as jnp
from jax import lax
import numpy as np

D_MODEL = 1024
BATCH = 8
SEQ = 2048
DEPTH = 1
DEC_BATCH = 16
DEC_SEQ = 16
PAST_LEN = 4096

CHUNK = 64
EPS = 1e-6
DN_HEADS = 4
DN_HEAD_DIM = 128
DN_WIDTH = DN_HEADS * DN_HEAD_DIM
DN_CONV = 4
SC_WIDTH = 256
SC_CONV = 3
MEM_LEN = 256
MEM_HEADS = 4
MEM_HEAD_DIM = 64
MEM_WIDTH = MEM_HEADS * MEM_HEAD_DIM
MIX_WIDTH = DN_WIDTH + SC_WIDTH + MEM_WIDTH
N_BRANCH = 3
IN_WIDTH = 4 * DN_WIDTH + 2 * DN_HEADS + 3 * SC_WIDTH + MEM_WIDTH + N_BRANCH * D_MODEL
N_EXPERTS = 32
TOP_K = 4
D_FF = D_MODEL
SWIGLU_ALPHA = 1.702
SWIGLU_LIMIT = 7.0

kernel_name = 'hybrid_deltanet_shortconv_mem_moe_stream_step'


def rmsnorm(x, w):
    xf = x.astype(jnp.float32)
    y = xf * lax.rsqrt(jnp.mean(xf * xf, axis=-1, keepdims=True) + EPS)
    return (y * w.astype(jnp.float32)).astype(x.dtype)


def l2norm(x):
    xf = x.astype(jnp.float32)
    return xf * lax.rsqrt(jnp.sum(xf * xf, axis=-1, keepdims=True) + EPS)


def causal_conv(x, buf, w):
    width, t = w.shape[0], x.shape[1]
    xp = jnp.concatenate([buf.astype(x.dtype), x], axis=1)
    y = w[0] * xp[:, 0:t]
    for j in range(1, width):
        y = y + w[j] * xp[:, j:j + t]
    return y.astype(x.dtype), xp[:, t:]


def gated_delta_chunked(q, k, v, g, beta, s0):
    b, t, h, dk = q.shape
    dv = v.shape[-1]
    L = CHUNK if t % CHUNK == 0 else t
    n = t // L

    def blocks(a):
        a = a.astype(jnp.float32).reshape((b, n, L, h) + a.shape[3:])
        return jnp.moveaxis(a, (1, 3), (0, 2))

    qc, kc, vc = blocks(q), blocks(k), blocks(v)
    gc, bc = blocks(g), blocks(beta)
    cum = jnp.cumsum(gc, axis=-1)
    causal = jnp.tril(jnp.ones((L, L), dtype=bool))
    strict = jnp.tril(jnp.ones((L, L), dtype=bool), -1)
    diff = cum[..., :, None] - cum[..., None, :]
    decay = jnp.where(causal, jnp.exp(jnp.where(causal, diff, 0.0)), 0.0)
    kk = jnp.einsum('nbhik,nbhjk->nbhij', kc, kc)
    a_mat = jnp.where(strict, decay * kk, 0.0) * bc[..., :, None]
    t_mat = a_mat + jnp.eye(L, dtype=jnp.float32)
    rhs = jnp.concatenate([bc[..., None] * vc, (bc * jnp.exp(cum))[..., None] * kc], axis=-1)
    sol = lax.linalg.triangular_solve(t_mat, rhs, left_side=True, lower=True, unit_diagonal=True)
    w_v, w_k = sol[..., :dv], sol[..., dv:]
    qk = jnp.einsum('nbhik,nbhjk->nbhij', qc, kc) * decay
    q_dec = jnp.exp(cum)[..., None] * qc
    k_dec = jnp.exp(cum[..., -1:] - cum)[..., None] * kc
    c_dec = jnp.exp(cum[..., -1])

    def step(s, xs):
        w_v_c, w_k_c, qk_c, q_c, k_c, d_c = xs
        u = w_v_c - jnp.einsum('bhlk,bhkv->bhlv', w_k_c, s)
        o = jnp.einsum('bhlk,bhkv->bhlv', q_c, s) + jnp.einsum('bhij,bhjv->bhiv', qk_c, u)
        s = d_c[..., None, None] * s + jnp.einsum('bhlk,bhlv->bhkv', k_c, u)
        return s, o

    s_final, o = lax.scan(step, s0.astype(jnp.float32), (w_v, w_k, qk, q_dec, k_dec, c_dec))
    o = jnp.moveaxis(o, (0, 2), (1, 3)).reshape(b, t, h, dv)
    return o, s_final


def split_in(proj):
    sizes = [DN_WIDTH, DN_WIDTH, DN_WIDTH, DN_HEADS, DN_HEADS, DN_WIDTH,
             SC_WIDTH, SC_WIDTH, SC_WIDTH, MEM_WIDTH, D_MODEL, D_MODEL, D_MODEL]
    offs = [int(o) for o in np.cumsum(sizes)[:-1]]
    return jnp.split(proj, offs, axis=-1)


def memory_kv(mem, mem_norm_w, w_mem_kv):
    b, m, _ = mem.shape
    kv = rmsnorm(mem, mem_norm_w) @ w_mem_kv
    mk, mv = jnp.split(kv, 2, axis=-1)
    return (mk.reshape(b, m, MEM_HEADS, MEM_HEAD_DIM), mv.reshape(b, m, MEM_HEADS, MEM_HEAD_DIM))


def mem_attend(q, mk, mv):
    s = jnp.einsum('bthd,bmhd->bhtm', q.astype(jnp.float32), mk.astype(jnp.float32)) * (MEM_HEAD_DIM ** -0.5)
    p = jax.nn.softmax(s, axis=-1)
    return jnp.einsum('bhtm,bmhd->bthd', p, mv.astype(jnp.float32)).astype(q.dtype)


def moe(x, w_router, b_router, w_gate, b_gate, w_up, b_up, w_down, b_down):
    bsz, t, d = x.shape
    n_tok = bsz * t
    xf = x.reshape(n_tok, d)
    logits = xf.astype(jnp.float32) @ w_router.astype(jnp.float32) + b_router.astype(jnp.float32)
    top_val, top_idx = lax.top_k(logits, TOP_K)
    gate = jax.nn.softmax(top_val, axis=-1)
    nk = n_tok * TOP_K
    flat_e = top_idx.reshape(nk)
    flat_tok = jnp.repeat(jnp.arange(n_tok, dtype=jnp.int32), TOP_K)
    order = jnp.argsort(flat_e)
    se, stok, sgate = flat_e[order], flat_tok[order], gate.reshape(nk)[order]
    blk = max(8, min(128, nk // N_EXPERTS))
    n_blocks = -(-nk // blk) + N_EXPERTS
    counts = jnp.bincount(flat_e, length=N_EXPERTS)
    pcounts = (counts + blk - 1) // blk * blk
    starts = jnp.cumsum(counts) - counts
    pends = jnp.cumsum(pcounts)
    pstarts = pends - pcounts
    dest = pstarts[se] + jnp.arange(nk, dtype=jnp.int32) - starts[se]
    rows = jnp.full((n_blocks * blk,), n_tok, dtype=jnp.int32).at[dest].set(stok)
    xpad = jnp.concatenate([xf, jnp.zeros((1, d), xf.dtype)], axis=0)
    xb = xpad[rows].reshape(n_blocks, blk, d)
    block_e = jnp.minimum(jnp.searchsorted(pends, jnp.arange(n_blocks, dtype=jnp.int32) * blk, side='right'), N_EXPERTS - 1)

    def expert_block(args):
        xblk, e = args
        gl = jnp.minimum(xblk @ w_gate[e] + b_gate[e], SWIGLU_LIMIT)
        ul = jnp.clip(xblk @ w_up[e] + b_up[e], -SWIGLU_LIMIT, SWIGLU_LIMIT)
        return ((ul + 1.0) * (gl * jax.nn.sigmoid(SWIGLU_ALPHA * gl))) @ w_down[e] + b_down[e]

    yb = lax.map(expert_block, (xb, block_e)).reshape(n_blocks * blk, d)
    pair = yb[dest] * sgate[:, None].astype(yb.dtype)
    out = jax.ops.segment_sum(pair, stok, num_segments=n_tok)
    return out.reshape(bsz, t, d).astype(x.dtype)


def trunk_layer(x, dn_conv_buf, dn_state, sc_conv_buf, mem_k, mem_v, p):
    b, t, _ = x.shape
    xn = rmsnorm(x, p['norm1_w'])
    proj = xn @ p['w_in']
    (q, k, v, a_logit, b_logit, dn_gate, sc_b, sc_c, sc_h, mq, g_dn, g_sc, g_mem) = split_in(proj)
    qkv, new_dn_conv = causal_conv(jnp.concatenate([q, k, v], axis=-1), dn_conv_buf, p['dn_conv_w'])
    q, k, v = jnp.split(jax.nn.silu(qkv), 3, axis=-1)
    q = l2norm(q.reshape(b, t, DN_HEADS, DN_HEAD_DIM)) * (DN_HEAD_DIM ** -0.5)
    k = l2norm(k.reshape(b, t, DN_HEADS, DN_HEAD_DIM))
    v = v.reshape(b, t, DN_HEADS, DN_HEAD_DIM)
    beta = jax.nn.sigmoid(b_logit.astype(jnp.float32))
    g = -jnp.exp(p['dn_A_log'].astype(jnp.float32)) * jax.nn.softplus(a_logit.astype(jnp.float32) + p['dn_dt_bias'].astype(jnp.float32))
    o, new_dn_state = gated_delta_chunked(q, k, v, g, beta, dn_state)
    o = o * lax.rsqrt(jnp.mean(o * o, axis=-1, keepdims=True) + EPS) * p['dn_norm_w'].astype(jnp.float32)
    o = o * jax.nn.silu(dn_gate.reshape(b, t, DN_HEADS, DN_HEAD_DIM).astype(jnp.float32))
    y_dn = o.reshape(b, t, DN_WIDTH).astype(x.dtype)
    conv_out, new_sc_conv = causal_conv(sc_c * sc_h, sc_conv_buf, p['sc_conv_w'])
    y_sc = sc_b * conv_out
    y_mem = mem_attend(mq.reshape(b, t, MEM_HEADS, MEM_HEAD_DIM), mem_k, mem_v).reshape(b, t, MEM_WIDTH)
    w_br = p['w_br']
    merged = (jax.nn.sigmoid(g_dn) * (y_dn @ w_br[:DN_WIDTH])
              + jax.nn.sigmoid(g_sc) * (y_sc @ w_br[DN_WIDTH:DN_WIDTH + SC_WIDTH])
              + jax.nn.sigmoid(g_mem) * (y_mem @ w_br[DN_WIDTH + SC_WIDTH:]))
    h = x + (merged @ p['w_o']).astype(x.dtype)
    h = h + moe(rmsnorm(h, p['norm2_w']), p['w_router'], p['b_router'], p['w_gate'], p['b_gate'],
                p['w_up'], p['b_up'], p['w_down'], p['b_down'])
    return h, new_dn_conv, new_dn_state.astype(x.dtype), new_sc_conv


def setup_inputs(seed: int = 0) -> dict:
    key = jax.random.key(seed)
    ks = jax.random.split(key, 32)
    f32 = jnp.float32

    def nrm(kk, shape, scale):
        return jax.random.normal(kk, shape, f32) * scale

    def gain(kk, shape):
        return 1.0 + 0.02 * jax.random.normal(kk, shape, f32)

    return {
        'x_prompt': nrm(ks[0], (BATCH, SEQ, D_MODEL), 1.0),
        'x_sample': nrm(ks[1], (DEC_BATCH, DEC_SEQ, D_MODEL), 1.0),
        'mem_prompt': nrm(ks[2], (BATCH, MEM_LEN, D_MODEL), 1.0),
        'state_dn': nrm(ks[3], (DEPTH, DEC_BATCH, DN_HEADS, DN_HEAD_DIM, DN_HEAD_DIM), 0.1),
        'state_dn_conv': nrm(ks[4], (DEPTH, DEC_BATCH, DN_CONV - 1, 3 * DN_WIDTH), 1.0),
        'state_sc_conv': nrm(ks[5], (DEPTH, DEC_BATCH, SC_CONV - 1, SC_WIDTH), 1.0),
        'cache_mem_k': nrm(ks[6], (DEPTH, DEC_BATCH, MEM_LEN, MEM_HEADS, MEM_HEAD_DIM), 1.0),
        'cache_mem_v': nrm(ks[7], (DEPTH, DEC_BATCH, MEM_LEN, MEM_HEADS, MEM_HEAD_DIM), 1.0),
        'w_in': nrm(ks[8], (DEPTH, D_MODEL, IN_WIDTH), D_MODEL ** -0.5),
        'dn_conv_w': nrm(ks[9], (DEPTH, DN_CONV, 3 * DN_WIDTH), DN_CONV ** -0.5),
        'dn_A_log': jnp.log(jax.random.uniform(ks[10], (DEPTH, DN_HEADS), f32, 1.0, 16.0)),
        'dn_dt_bias': jax.random.uniform(ks[11], (DEPTH, DN_HEADS), f32, -6.0, -3.0),
        'dn_norm_w': gain(ks[12], (DEPTH, DN_HEAD_DIM)),
        'sc_conv_w': nrm(ks[13], (DEPTH, SC_CONV, SC_WIDTH), SC_CONV ** -0.5),
        'mem_norm_w': gain(ks[14], (DEPTH, D_MODEL)),
        'w_mem_kv': nrm(ks[15], (DEPTH, D_MODEL, 2 * MEM_WIDTH), D_MODEL ** -0.5),
        'w_br': nrm(ks[16], (DEPTH, MIX_WIDTH, D_MODEL), MIX_WIDTH ** -0.5),
        'w_o': nrm(ks[17], (DEPTH, D_MODEL, D_MODEL), D_MODEL ** -0.5),
        'norm1_w': gain(ks[18], (DEPTH, D_MODEL)),
        'norm2_w': gain(ks[19], (DEPTH, D_MODEL)),
        'w_router': nrm(ks[20], (DEPTH, D_MODEL, N_EXPERTS), D_MODEL ** -0.5),
        'b_router': nrm(ks[21], (DEPTH, N_EXPERTS), 0.01),
        'w_gate': nrm(ks[22], (DEPTH, N_EXPERTS, D_MODEL, D_FF), D_MODEL ** -0.5),
        'b_gate': nrm(ks[23], (DEPTH, N_EXPERTS, D_FF), 0.01),
        'w_up': nrm(ks[24], (DEPTH, N_EXPERTS, D_MODEL, D_FF), D_MODEL ** -0.5),
        'b_up': nrm(ks[25], (DEPTH, N_EXPERTS, D_FF), 0.01),
        'w_down': nrm(ks[26], (DEPTH, N_EXPERTS, D_FF, D_MODEL), D_FF ** -0.5),
        'b_down': nrm(ks[27], (DEPTH, N_EXPERTS, D_MODEL), 0.01),
        'final_norm_w': gain(ks[28], (D_MODEL,)),
    }


def reference(x_prompt, x_sample, mem_prompt, state_dn, state_dn_conv, state_sc_conv, cache_mem_k, cache_mem_v,
              w_in, dn_conv_w, dn_A_log, dn_dt_bias, dn_norm_w, sc_conv_w, mem_norm_w, w_mem_kv, w_br, w_o,
              norm1_w, norm2_w, w_router, b_router, w_gate, b_gate, w_up, b_up, w_down, b_down, final_norm_w):
    hp, hs = x_prompt, x_sample
    bp = x_prompt.shape[0]
    p_dn, p_dnc, p_scc, p_mk, p_mv = [], [], [], [], []
    s_dn, s_dnc, s_scc = [], [], []
    for l in range(DEPTH):
        p = {'w_in': w_in[l], 'dn_conv_w': dn_conv_w[l], 'dn_A_log': dn_A_log[l], 'dn_dt_bias': dn_dt_bias[l],
             'dn_norm_w': dn_norm_w[l], 'sc_conv_w': sc_conv_w[l], 'w_br': w_br[l], 'w_o': w_o[l],
             'norm1_w': norm1_w[l], 'norm2_w': norm2_w[l], 'w_router': w_router[l], 'b_router': b_router[l],
             'w_gate': w_gate[l], 'b_gate': b_gate[l], 'w_up': w_up[l], 'b_up': b_up[l],
             'w_down': w_down[l], 'b_down': b_down[l]}
        mk, mv = memory_kv(mem_prompt, mem_norm_w[l], w_mem_kv[l])
        hp, dnc, dns, scc = trunk_layer(
            hp, jnp.zeros((bp, DN_CONV - 1, 3 * DN_WIDTH), hp.dtype),
            jnp.zeros((bp, DN_HEADS, DN_HEAD_DIM, DN_HEAD_DIM), jnp.float32),
            jnp.zeros((bp, SC_CONV - 1, SC_WIDTH), hp.dtype), mk, mv, p)
        p_dn.append(dns); p_dnc.append(dnc); p_scc.append(scc); p_mk.append(mk); p_mv.append(mv)
        hs, dnc2, dns2, scc2 = trunk_layer(hs, state_dn_conv[l], state_dn[l], state_sc_conv[l],
                                           cache_mem_k[l], cache_mem_v[l], p)
        s_dn.append(dns2); s_dnc.append(dnc2); s_scc.append(scc2)
    y_prompt = rmsnorm(hp, final_norm_w)
    y_sample = rmsnorm(hs, final_norm_w)
    return (y_prompt, y_sample, jnp.stack(p_dn), jnp.stack(p_dnc), jnp.stack(p_scc), jnp.stack(p_mk), jnp.stack(p_mv),
            jnp.stack(s_dn), jnp.stack(s_dnc), jnp.stack(s_scc))
```

```python
import functools

import jax
import jax.numpy as jnp
from jax import lax
from jax.experimental import pallas as pl
from jax.experimental.pallas import tpu as pltpu

F32 = jnp.float32
BF16 = jnp.bfloat16

D_MODEL = 1024
CHUNK = 64
EPS = 1e-6
DN_HEADS = 4
DN_HEAD_DIM = 128
DN_WIDTH = DN_HEADS * DN_HEAD_DIM
QKV_WIDTH = 3 * DN_WIDTH
DN_CONV = 4
SC_WIDTH = 256
SC_CONV = 3
MEM_LEN = 256
MEM_HEADS = 4
MEM_HEAD_DIM = 64
MEM_WIDTH = MEM_HEADS * MEM_HEAD_DIM
N_EXPERTS = 32
TOP_K = 4
SWIGLU_ALPHA = 1.702
SWIGLU_LIMIT = 7.0

LANES = 128
CONV_PAD = 8

OFF_QKV = 0
OFF_DNG = OFF_QKV + QKV_WIDTH
OFF_SC = OFF_DNG + DN_WIDTH
OFF_MQ = OFF_SC + 3 * SC_WIDTH
OFF_GATE = OFF_MQ + MEM_WIDTH
OFF_AB = OFF_GATE + 3 * D_MODEL
IN_PERM_WIDTH = OFF_AB + LANES

MIX_TILE = 512
TOK_TILE = 256
ROW_BLOCK = 256
VMEM_LIMIT = 56 * 1024 * 1024
NEG_BIG = -1e30


def _dot(a, b):
    return jnp.dot(a, b, preferred_element_type=F32)


def _dot_nt(a, b):
    return lax.dot_general(a, b, (((1,), (1,)), ((), ())), preferred_element_type=F32)


def _dot_tn(a, b):
    return lax.dot_general(a, b, (((0,), (0,)), ((), ())), preferred_element_type=F32)


def _sigmoid(x):
    return 1.0 / (1.0 + jnp.exp(-x))


def _softplus(x):
    return jnp.maximum(x, 0.0) + jnp.log1p(jnp.exp(-jnp.abs(x)))


def _for_each(n, body):
    if n == 1:
        body(0)
    else:
        def step(i, carry):
            body(i)
            return carry
        lax.fori_loop(0, n, step, 0)


def _memkv_kernel(mem_ref, nw_ref, w_ref, k_ref, v_ref):
    x = mem_ref[...]
    xn = x * lax.rsqrt(jnp.mean(x * x, axis=-1, keepdims=True) + EPS) * nw_ref[...]
    kv = _dot(xn.astype(BF16), w_ref[...])
    k_ref[...] = kv[:, :MEM_WIDTH]
    v_ref[...] = kv[:, MEM_WIDTH:]


def _memkv(mem2d, norm_w, w_kv_bf16):
    rows = mem2d.shape[0]
    grid = rows // MEM_LEN
    return pl.pallas_call(
        _memkv_kernel,
        grid=(grid,),
        in_specs=[
            pl.BlockSpec((MEM_LEN, D_MODEL), lambda i: (i, 0)),
            pl.BlockSpec((1, D_MODEL), lambda i: (0, 0)),
            pl.BlockSpec((D_MODEL, 2 * MEM_WIDTH), lambda i: (0, 0)),
        ],
        out_specs=[
            pl.BlockSpec((MEM_LEN, MEM_WIDTH), lambda i: (i, 0)),
            pl.BlockSpec((MEM_LEN, MEM_WIDTH), lambda i: (i, 0)),
        ],
        out_shape=[jax.ShapeDtypeStruct((rows, MEM_WIDTH), F32)] * 2,
        name="memkv",
    )(mem2d, norm_w, w_kv_bf16)


def _unit_lower_inverse(a, eye, size):
    inv = eye - a
    power = a
    span = 2
    while span < size:
        power = _dot(power, power)
        inv = _dot(inv, eye + power)
        span *= 2
    return inv


def _mixer_kernel(x_ref, dnc_in, dns_in, scc_in, mk_ref, mv_ref, w_in, w_br, w_o, n1_ref, n2_ref, dcw_ref, scw_ref,
                  alog_ref, dtb_ref, dnw_ref, wr_ref, br_ref,
                  h_ref, hn_ref, lg_ref, dnc_out, dns_out, scc_out,
                  xp, scp, q_s, k_s, v_s, gb_s, o_s, mq_s, ysc_s, ymem_s, *, nb, tt, chunk):
    t_idx = pl.program_id(1)
    rows = nb * tt
    n_chunk = tt // chunk

    @pl.when(t_idx == 0)
    def _():
        xp[:, CONV_PAD - (DN_CONV - 1):CONV_PAD, :] = dnc_in[...]
        scp[:, CONV_PAD - (SC_CONV - 1):CONV_PAD, :] = scc_in[...]
        dns_out[...] = dns_in[...]

    x = x_ref[...]
    xn = (x * lax.rsqrt(jnp.mean(x * x, axis=-1, keepdims=True) + EPS) * n1_ref[...]).astype(BF16)

    def proj(off, width):
        return _dot(xn, w_in[:, off:off + width])

    qkv_pre = proj(OFF_QKV, QKV_WIDTH)
    for s in range(nb):
        xp[s, CONV_PAD:CONV_PAD + tt, :] = qkv_pre[s * tt:(s + 1) * tt, :]
    sc = proj(OFF_SC, 3 * SC_WIDTH)
    sc_b = sc[:, :SC_WIDTH]
    sc_ch = sc[:, SC_WIDTH:2 * SC_WIDTH] * sc[:, 2 * SC_WIDTH:]
    for s in range(nb):
        scp[s, CONV_PAD:CONV_PAD + tt, :] = sc_ch[s * tt:(s + 1) * tt, :]
    mq_s[...] = proj(OFF_MQ, MEM_WIDTH)

    ab = proj(OFF_AB, LANES)
    lane = lax.broadcasted_iota(jnp.int32, (rows, LANES), 1)
    g_log = -jnp.exp(alog_ref[...]) * _softplus(ab + dtb_ref[...])
    gb_s[...] = jnp.where(lane < DN_HEADS, g_log, _sigmoid(ab))

    def conv_seq(s):
        base = CONV_PAD - (DN_CONV - 1)
        acc = dcw_ref[0:1, :] * xp[s, pl.ds(base, tt), :]
        for j in range(1, DN_CONV):
            acc = acc + dcw_ref[j:j + 1, :] * xp[s, pl.ds(base + j, tt), :]
        act = acc * _sigmoid(acc)
        r0 = pl.multiple_of(s * tt, tt)
        for hd in range(DN_HEADS):
            lo = hd * DN_HEAD_DIM
            qh = act[:, lo:lo + DN_HEAD_DIM]
            kh = act[:, DN_WIDTH + lo:DN_WIDTH + lo + DN_HEAD_DIM]
            q_s[pl.ds(r0, tt), lo:lo + DN_HEAD_DIM] = (
                qh * lax.rsqrt(jnp.sum(qh * qh, axis=-1, keepdims=True) + EPS) * (DN_HEAD_DIM ** -0.5))
            k_s[pl.ds(r0, tt), lo:lo + DN_HEAD_DIM] = (
                kh * lax.rsqrt(jnp.sum(kh * kh, axis=-1, keepdims=True) + EPS))
        v_s[pl.ds(r0, tt), :] = act[:, 2 * DN_WIDTH:]
        tail = xp[s, pl.ds(tt + base, DN_CONV - 1), :]
        dnc_out[s] = tail
        xp[s, pl.ds(base, DN_CONV - 1), :] = tail

        base2 = CONV_PAD - (SC_CONV - 1)
        acc2 = scw_ref[0:1, :] * scp[s, pl.ds(base2, tt), :]
        for j in range(1, SC_CONV):
            acc2 = acc2 + scw_ref[j:j + 1, :] * scp[s, pl.ds(base2 + j, tt), :]
        ysc_s[pl.ds(r0, tt), :] = acc2
        tail2 = scp[s, pl.ds(tt + base2, SC_CONV - 1), :]
        scc_out[s] = tail2
        scp[s, pl.ds(base2, SC_CONV - 1), :] = tail2

    _for_each(nb, conv_seq)

    ri = lax.broadcasted_iota(jnp.int32, (chunk, chunk), 0)
    ci = lax.broadcasted_iota(jnp.int32, (chunk, chunk), 1)
    causal = ri >= ci
    strict = ri > ci
    tril = causal.astype(F32)
    triu = (ri <= ci).astype(F32)
    eye = (ri == ci).astype(F32)

    def chunk_step(idx):
        s = idx // n_chunk
        r0 = pl.multiple_of(idx * chunk, chunk)
        gb = gb_s[pl.ds(r0, chunk), :]
        cum = _dot(tril, gb)
        cum_t = _dot_tn(gb, triu)
        for hd in range(DN_HEADS):
            lo = hd * DN_HEAD_DIM
            cc = cum[:, hd:hd + 1]
            cr = cum_t[hd:hd + 1, :]
            beta = gb[:, DN_HEADS + hd:DN_HEADS + hd + 1]
            qh = q_s[pl.ds(r0, chunk), lo:lo + DN_HEAD_DIM]
            kh = k_s[pl.ds(r0, chunk), lo:lo + DN_HEAD_DIM]
            vh = v_s[pl.ds(r0, chunk), lo:lo + DN_HEAD_DIM]
            decay = jnp.where(causal, jnp.exp(jnp.where(causal, cc - cr, 0.0)), 0.0)
            kk = _dot_nt(kh, kh)
            a_mat = jnp.where(strict, decay * kk, 0.0) * beta
            t_inv = _unit_lower_inverse(a_mat, eye, chunk)
            e_cum = jnp.exp(cc)
            w_v = _dot(t_inv, beta * vh)
            w_k = _dot(t_inv, (beta * e_cum) * kh)
            qk = _dot_nt(qh, kh) * decay
            state = dns_out[s, hd]
            u = w_v - _dot(w_k, state)
            o_s[pl.ds(r0, chunk), lo:lo + DN_HEAD_DIM] = _dot(e_cum * qh, state) + _dot(qk, u)
            c_last = cum[chunk - 1:chunk, hd:hd + 1]
            k_dec = jnp.exp(c_last - cc) * kh
            dns_out[s, hd] = jnp.exp(c_last) * state + _dot_tn(k_dec, u)

    _for_each(nb * n_chunk, chunk_step)

    def attn_seq(s):
        r0 = pl.multiple_of(s * tt, tt)
        mq = mq_s[pl.ds(r0, tt), :]
        for hd in range(MEM_HEADS):
            lo = hd * MEM_HEAD_DIM
            qh = mq[:, lo:lo + MEM_HEAD_DIM].astype(BF16)
            kh = mk_ref[s, :, lo:lo + MEM_HEAD_DIM].astype(BF16)
            vh = mv_ref[s, :, lo:lo + MEM_HEAD_DIM].astype(BF16)
            sc_h = _dot_nt(qh, kh) * (MEM_HEAD_DIM ** -0.5)
            p = jnp.exp(sc_h - jnp.max(sc_h, axis=-1, keepdims=True))
            denom = jnp.sum(p, axis=-1, keepdims=True)
            ymem_s[pl.ds(r0, tt), lo:lo + MEM_HEAD_DIM] = _dot(p.astype(BF16), vh) / denom

    _for_each(nb, attn_seq)

    dn_gate = proj(OFF_DNG, DN_WIDTH)
    o_all = o_s[...]
    y_heads = []
    for hd in range(DN_HEADS):
        lo = hd * DN_HEAD_DIM
        oh = o_all[:, lo:lo + DN_HEAD_DIM]
        oh = oh * lax.rsqrt(jnp.mean(oh * oh, axis=-1, keepdims=True) + EPS) * dnw_ref[...]
        gh = dn_gate[:, lo:lo + DN_HEAD_DIM]
        y_heads.append(oh * (gh * _sigmoid(gh)))
    y_dn = jnp.concatenate(y_heads, axis=-1).astype(BF16)
    y_sc = (sc_b * ysc_s[...]).astype(BF16)
    y_mem = ymem_s[...].astype(BF16)

    merged = _sigmoid(proj(OFF_GATE, D_MODEL)) * _dot(y_dn, w_br[0:DN_WIDTH, :])
    merged = merged + _sigmoid(proj(OFF_GATE + D_MODEL, D_MODEL)) * _dot(y_sc, w_br[DN_WIDTH:DN_WIDTH + SC_WIDTH, :])
    merged = merged + _sigmoid(proj(OFF_GATE + 2 * D_MODEL, D_MODEL)) * _dot(y_mem, w_br[DN_WIDTH + SC_WIDTH:, :])
    h = x + _dot(merged.astype(BF16), w_o[...])
    h_ref[...] = h
    hn = h * lax.rsqrt(jnp.mean(h * h, axis=-1, keepdims=True) + EPS) * n2_ref[...]
    hn_ref[...] = hn
    lg_ref[...] = _dot(hn, wr_ref[...]) + br_ref[...]


def _mixer(x2d, dnc_in, dns_in, scc_in, mk, mv, weights, *, n_seq, seq_len, nb, tt):
    chunk = CHUNK if seq_len % CHUNK == 0 else seq_len
    rows = nb * tt
    n_t = seq_len // tt
    total_rows = n_seq * seq_len
    grid = (n_seq // nb, n_t)
    const = lambda b, t: (0, 0)
    seq3 = lambda b, t: (b, 0, 0)

    def tok(b, t):
        return (b * n_t + t, 0)

    tok_out = tok

    (w_in, w_br, w_o, n1, n2, dcw, scw, alog, dtb, dnw, wr, br) = weights
    in_specs = [
        pl.BlockSpec((rows, D_MODEL), tok),
        pl.BlockSpec((nb, DN_CONV - 1, QKV_WIDTH), seq3),
        pl.BlockSpec((nb, DN_HEADS, DN_HEAD_DIM, DN_HEAD_DIM), lambda b, t: (b, 0, 0, 0)),
        pl.BlockSpec((nb, SC_CONV - 1, SC_WIDTH), seq3),
        pl.BlockSpec((nb, MEM_LEN, MEM_WIDTH), seq3),
        pl.BlockSpec((nb, MEM_LEN, MEM_WIDTH), seq3),
        pl.BlockSpec(w_in.shape, const),
        pl.BlockSpec(w_br.shape, const),
        pl.BlockSpec(w_o.shape, const),
        pl.BlockSpec(n1.shape, const),
        pl.BlockSpec(n2.shape, const),
        pl.BlockSpec(dcw.shape, const),
        pl.BlockSpec(scw.shape, const),
        pl.BlockSpec(alog.shape, const),
        pl.BlockSpec(dtb.shape, const),
        pl.BlockSpec(dnw.shape, const),
        pl.BlockSpec(wr.shape, const),
        pl.BlockSpec(br.shape, const),
    ]
    out_shape = [
        jax.ShapeDtypeStruct((total_rows, D_MODEL), F32),
        jax.ShapeDtypeStruct((total_rows, D_MODEL), F32),
        jax.ShapeDtypeStruct((total_rows, LANES), F32),
        jax.ShapeDtypeStruct((n_seq, DN_CONV - 1, QKV_WIDTH), F32),
        jax.ShapeDtypeStruct((n_seq, DN_HEADS, DN_HEAD_DIM, DN_HEAD_DIM), F32),
        jax.ShapeDtypeStruct((n_seq, SC_CONV - 1, SC_WIDTH), F32),
    ]
    out_specs = [
        pl.BlockSpec((rows, D_MODEL), tok_out),
        pl.BlockSpec((rows, D_MODEL), tok_out),
        pl.BlockSpec((rows, LANES), tok_out),
        pl.BlockSpec((nb, DN_CONV - 1, QKV_WIDTH), seq3),
        pl.BlockSpec((nb, DN_HEADS, DN_HEAD_DIM, DN_HEAD_DIM), lambda b, t: (b, 0, 0, 0)),
        pl.BlockSpec((nb, SC_CONV - 1, SC_WIDTH), seq3),
    ]
    args = [x2d, dnc_in, dns_in, scc_in, mk, mv, w_in, w_br, w_o, n1, n2, dcw, scw, alog, dtb, dnw, wr, br]
    scratch = [
        pltpu.VMEM((nb, CONV_PAD + tt, QKV_WIDTH), F32),
        pltpu.VMEM((nb, CONV_PAD + tt, SC_WIDTH), F32),
        pltpu.VMEM((rows, DN_WIDTH), F32),
        pltpu.VMEM((rows, DN_WIDTH), F32),
        pltpu.VMEM((rows, DN_WIDTH), F32),
        pltpu.VMEM((rows, LANES), F32),
        pltpu.VMEM((rows, DN_WIDTH), F32),
        pltpu.VMEM((rows, MEM_WIDTH), F32),
        pltpu.VMEM((rows, SC_WIDTH), F32),
        pltpu.VMEM((rows, MEM_WIDTH), F32),
    ]
    return pl.pallas_call(
        functools.partial(_mixer_kernel, nb=nb, tt=tt, chunk=chunk),
        grid=grid,
        in_specs=in_specs,
        out_specs=out_specs,
        out_shape=out_shape,
        scratch_shapes=scratch,
        compiler_params=pltpu.CompilerParams(
            dimension_semantics=("arbitrary", "arbitrary"), vmem_limit_bytes=VMEM_LIMIT),
        name="mixer",
    )(*args)


def _router_kernel(lg_ref, idx_ref, gate_ref, rank_ref, cnt_ref, carry):
    i = pl.program_id(0)

    @pl.when(i == 0)
    def _():
        carry[...] = jnp.zeros_like(carry)

    work = lg_ref[...]
    tm = work.shape[0]
    lane = lax.broadcasted_iota(jnp.int32, (tm, LANES), 1).astype(F32)
    idxs, vals = [], []
    for _ in range(TOP_K):
        m = jnp.max(work, axis=-1, keepdims=True)
        ik = jnp.min(jnp.where(work == m, lane, float(LANES)), axis=-1, keepdims=True)
        idxs.append(ik)
        vals.append(m)
        work = jnp.where(lane == ik, -jnp.inf, work)
    exps = [jnp.exp(v - vals[0]) for v in vals]
    denom = exps[0] + exps[1] + exps[2] + exps[3]
    hot = jnp.zeros((tm, LANES), F32)
    for ik in idxs:
        hot = hot + (lane == ik).astype(F32)
    ri = lax.broadcasted_iota(jnp.int32, (tm, tm), 0)
    ci = lax.broadcasted_iota(jnp.int32, (tm, tm), 1)
    before = (ri > ci).astype(BF16)
    prefix = _dot(before, hot.astype(BF16)) + carry[...]
    idx_out = jnp.zeros((tm, LANES), F32)
    gate_out = jnp.zeros((tm, LANES), F32)
    rank_out = jnp.zeros((tm, LANES), F32)
    for k in range(TOP_K):
        rk = jnp.sum(jnp.where(lane == idxs[k], prefix, 0.0), axis=-1, keepdims=True)
        idx_out = jnp.where(lane == k, idxs[k], idx_out)
        gate_out = jnp.where(lane == k, exps[k] / denom, gate_out)
        rank_out = jnp.where(lane == k, rk, rank_out)
    idx_ref[...] = idx_out.astype(jnp.int32)
    gate_ref[...] = gate_out
    rank_ref[...] = rank_out.astype(jnp.int32)
    carry[...] = carry[...] + jnp.sum(hot, axis=0, keepdims=True)
    cnt_ref[...] = carry[...].astype(jnp.int32)


def _router(logits):
    n_tok = logits.shape[0]
    tile = lambda i: (i, 0)
    return pl.pallas_call(
        _router_kernel,
        grid=(n_tok // TOK_TILE,),
        in_specs=[pl.BlockSpec((TOK_TILE, LANES), tile)],
        out_specs=[pl.BlockSpec((TOK_TILE, LANES), tile)] * 3 + [pl.BlockSpec((1, LANES), lambda i: (0, 0))],
        out_shape=[
            jax.ShapeDtypeStruct((n_tok, LANES), jnp.int32),
            jax.ShapeDtypeStruct((n_tok, LANES), F32),
            jax.ShapeDtypeStruct((n_tok, LANES), jnp.int32),
            jax.ShapeDtypeStruct((1, LANES), jnp.int32),
        ],
        scratch_shapes=[pltpu.VMEM((1, LANES), F32)],
        compiler_params=pltpu.CompilerParams(dimension_semantics=("arbitrary",)),
        name="router",
    )(logits)


def _dispatch_kernel(pad_start, pad_n, n_valid, dest_ref, hp_ref, hs_ref, xs_ref, zeros, sem, zsem, *,
                     n_tiles_p, n_blocks):
    i = pl.program_id(0)
    n_pairs = TOK_TILE * TOP_K

    def scatter_rows(src_ref):
        def row_copy(p):
            return pltpu.make_async_copy(
                src_ref.at[pl.ds(p // TOP_K, 1)], xs_ref.at[pl.ds(dest_ref[p], 1)], sem)

        def issue(p, c):
            row_copy(p).start()
            return c

        def drain(p, c):
            row_copy(p).wait()
            return c

        lax.fori_loop(0, n_pairs, issue, 0)
        lax.fori_loop(0, n_pairs, drain, 0)

    @pl.when(i < n_tiles_p)
    def _():
        scatter_rows(hp_ref)

    @pl.when(i >= n_tiles_p)
    def _():
        scatter_rows(hs_ref)

    @pl.when(i == pl.num_programs(0) - 1)
    def _():
        zeros[...] = jnp.zeros_like(zeros)

        def pad_expert(e, c):
            def zero_copy(r):
                return pltpu.make_async_copy(
                    zeros.at[pl.ds(0, 1)], xs_ref.at[pl.ds(pad_start[e] + r, 1)], zsem)

            def issue_pad(r, cc):
                zero_copy(r).start()
                return cc

            def drain_pad(r, cc):
                zero_copy(r).wait()
                return cc

            lax.fori_loop(0, pad_n[e], issue_pad, 0)
            lax.fori_loop(0, pad_n[e], drain_pad, 0)
            return c

        lax.fori_loop(0, N_EXPERTS, pad_expert, 0)

        def zero_block(b, c):
            cp = pltpu.make_async_copy(
                zeros, xs_ref.at[pl.ds(pl.multiple_of(b * ROW_BLOCK, ROW_BLOCK), ROW_BLOCK)], zsem)
            cp.start()
            cp.wait()
            return c

        lax.fori_loop(n_valid[0], n_blocks, zero_block, 0)


def _dispatch(pad_start, pad_n, n_valid, dest_flat, hn_p, hn_s, n_blocks):
    n_tiles_p = hn_p.shape[0] // TOK_TILE
    n_tiles_s = hn_s.shape[0] // TOK_TILE
    grid_spec = pltpu.PrefetchScalarGridSpec(
        num_scalar_prefetch=3,
        grid=(n_tiles_p + n_tiles_s,),
        in_specs=[
            pl.BlockSpec((TOK_TILE * TOP_K,), lambda i, *_: (i,), memory_space=pltpu.SMEM),
            pl.BlockSpec((TOK_TILE, D_MODEL), lambda i, *_: (jnp.minimum(i, n_tiles_p - 1), 0)),
            pl.BlockSpec((TOK_TILE, D_MODEL), lambda i, *_: (jnp.maximum(i - n_tiles_p, 0), 0)),
        ],
        out_specs=pl.BlockSpec(memory_space=pl.ANY),
        scratch_shapes=[
            pltpu.VMEM((ROW_BLOCK, D_MODEL), F32),
            pltpu.SemaphoreType.DMA(()),
            pltpu.SemaphoreType.DMA(()),
        ],
    )
    return pl.pallas_call(
        functools.partial(_dispatch_kernel, n_tiles_p=n_tiles_p, n_blocks=n_blocks),
        grid_spec=grid_spec,
        out_shape=jax.ShapeDtypeStruct((n_blocks * ROW_BLOCK, D_MODEL), F32),
        compiler_params=pltpu.CompilerParams(dimension_semantics=("arbitrary",)),
        name="dispatch",
    )(pad_start, pad_n, n_valid, dest_flat, hn_p, hn_s)


def _experts_kernel(blk_e, n_valid, xs_ref, wg_ref, bg_ref, wu_ref, bu_ref, wd_ref, bd_ref, ys_ref,
                    wg_bf, wu_bf, wd_bf):
    i = pl.program_id(0)
    e = blk_e[i]
    prev = blk_e[jnp.maximum(i - 1, 0)]

    @pl.when((i == 0) | (e != prev))
    def _():
        wg_bf[...] = wg_ref[0].astype(BF16)
        wu_bf[...] = wu_ref[0].astype(BF16)
        wd_bf[...] = wd_ref[0].astype(BF16)

    @pl.when(i < n_valid[0])
    def _():
        x = xs_ref[...].astype(BF16)
        gl = jnp.minimum(_dot(x, wg_bf[...]) + bg_ref[0], SWIGLU_LIMIT)
        ul = jnp.clip(_dot(x, wu_bf[...]) + bu_ref[0], -SWIGLU_LIMIT, SWIGLU_LIMIT)
        act = (ul + 1.0) * (gl * _sigmoid(SWIGLU_ALPHA * gl))
        ys_ref[...] = _dot(act.astype(BF16), wd_bf[...]) + bd_ref[0]

    @pl.when(i >= n_valid[0])
    def _():
        ys_ref[...] = jnp.zeros_like(ys_ref)


def _experts(blk_e, n_valid, xs, w_gate, b_gate, w_up, b_up, w_down, b_down):
    n_rows = xs.shape[0]
    n_blocks = blk_e.shape[0]
    rows = lambda i, be, nv: (i, 0)
    wsel = lambda i, be, nv: (be[i], 0, 0)
    d_ff = w_gate.shape[-1]
    grid_spec = pltpu.PrefetchScalarGridSpec(
        num_scalar_prefetch=2,
        grid=(n_blocks,),
        in_specs=[
            pl.BlockSpec((ROW_BLOCK, D_MODEL), rows),
            pl.BlockSpec((1, D_MODEL, d_ff), wsel),
            pl.BlockSpec((1, 1, d_ff), wsel),
            pl.BlockSpec((1, D_MODEL, d_ff), wsel),
            pl.BlockSpec((1, 1, d_ff), wsel),
            pl.BlockSpec((1, d_ff, D_MODEL), wsel),
            pl.BlockSpec((1, 1, D_MODEL), wsel),
        ],
        out_specs=pl.BlockSpec((ROW_BLOCK, D_MODEL), rows),
        scratch_shapes=[
            pltpu.VMEM((D_MODEL, d_ff), BF16),
            pltpu.VMEM((D_MODEL, d_ff), BF16),
            pltpu.VMEM((d_ff, D_MODEL), BF16),
        ],
    )
    return pl.pallas_call(
        _experts_kernel,
        grid_spec=grid_spec,
        out_shape=jax.ShapeDtypeStruct((n_rows, D_MODEL), F32),
        compiler_params=pltpu.CompilerParams(
            dimension_semantics=("arbitrary",), vmem_limit_bytes=VMEM_LIMIT),
        name="experts",
    )(blk_e, n_valid, xs, w_gate, b_gate, w_up, b_up, w_down, b_down)


def _combine_kernel(dest_ref, gate_ref, h_ref, fw_ref, ys_ref, y_ref, buf, sem):
    n_pairs = TOK_TILE * TOP_K

    def row_copy(p):
        return pltpu.make_async_copy(
            ys_ref.at[pl.ds(dest_ref[p], 1)], buf.at[p % TOP_K, pl.ds(p // TOP_K, 1)], sem)

    def issue(p, c):
        row_copy(p).start()
        return c

    lax.fori_loop(0, n_pairs, issue, 0)

    def drain(p, c):
        row_copy(p).wait()
        return c

    lax.fori_loop(0, n_pairs, drain, 0)

    gate = gate_ref[...]
    acc = h_ref[...]
    for k in range(TOP_K):
        acc = acc + gate[:, k:k + 1] * buf[k]
    y_ref[...] = acc * lax.rsqrt(jnp.mean(acc * acc, axis=-1, keepdims=True) + EPS) * fw_ref[...]


def _combine(dest_flat, gates, h_group, final_w, ys, *, tile0):
    n_tiles = h_group.shape[0] // TOK_TILE
    return pl.pallas_call(
        _combine_kernel,
        grid=(n_tiles,),
        in_specs=[
            pl.BlockSpec((TOK_TILE * TOP_K,), lambda i: (tile0 + i,), memory_space=pltpu.SMEM),
            pl.BlockSpec((TOK_TILE, LANES), lambda i: (tile0 + i, 0)),
            pl.BlockSpec((TOK_TILE, D_MODEL), lambda i: (i, 0)),
            pl.BlockSpec((1, D_MODEL), lambda i: (0, 0)),
            pl.BlockSpec(memory_space=pl.ANY),
        ],
        out_specs=pl.BlockSpec((TOK_TILE, D_MODEL), lambda i: (i, 0)),
        out_shape=jax.ShapeDtypeStruct((n_tiles * TOK_TILE, D_MODEL), F32),
        scratch_shapes=[
            pltpu.VMEM((TOP_K, TOK_TILE, D_MODEL), F32),
            pltpu.SemaphoreType.DMA(()),
        ],
        compiler_params=pltpu.CompilerParams(dimension_semantics=("arbitrary",)),
        name="combine",
    )(dest_flat, gates, h_group, final_w, ys)


def _permute_w_in(w):
    o_a = QKV_WIDTH
    o_g = o_a + 2 * DN_HEADS
    o_rest = o_g + DN_WIDTH
    pad = jnp.zeros((w.shape[0], LANES - 2 * DN_HEADS), w.dtype)
    return jnp.concatenate([w[:, :o_a], w[:, o_g:o_rest], w[:, o_rest:], w[:, o_a:o_g], pad], axis=1)


def _lane_row(v, fill=0.0):
    return jnp.concatenate([v.astype(F32), jnp.full((LANES - v.shape[0],), fill, F32)]).reshape(1, LANES)


def kernel(x_prompt, x_sample, mem_prompt, state_dn, state_dn_conv, state_sc_conv, cache_mem_k, cache_mem_v, w_in, dn_conv_w, dn_A_log, dn_dt_bias, dn_norm_w, sc_conv_w, mem_norm_w, w_mem_kv, w_br, w_o, norm1_w, norm2_w, w_router, b_router, w_gate, b_gate, w_up, b_up, w_down, b_down, final_norm_w):
    assert w_in.shape[0] == 1, "one layer"
    bp, tp, _ = x_prompt.shape
    bs, ts, _ = x_sample.shape
    n_p, n_s = bp * tp, bs * ts
    n_tok = n_p + n_s
    assert n_p % TOK_TILE == 0 and n_s % TOK_TILE == 0 and tp % MIX_TILE == 0

    weights = (
        _permute_w_in(w_in[0]).astype(BF16),
        w_br[0].astype(BF16),
        w_o[0].astype(BF16),
        norm1_w[0].reshape(1, D_MODEL),
        norm2_w[0].reshape(1, D_MODEL),
        dn_conv_w[0],
        sc_conv_w[0],
        _lane_row(dn_A_log[0]),
        _lane_row(dn_dt_bias[0]),
        dn_norm_w[0].reshape(1, DN_HEAD_DIM),
        jnp.concatenate([w_router[0], jnp.zeros((D_MODEL, LANES - N_EXPERTS), F32)], axis=1),
        _lane_row(b_router[0], NEG_BIG),
    )

    mk2d, mv2d = _memkv(mem_prompt.reshape(bp * MEM_LEN, D_MODEL), mem_norm_w[0].reshape(1, D_MODEL),
                        w_mem_kv[0].astype(BF16))

    h_p, hn_p, lg_p, p_dnc, p_dns, p_scc = _mixer(
        x_prompt.reshape(n_p, D_MODEL),
        jnp.zeros((bp, DN_CONV - 1, QKV_WIDTH), F32),
        jnp.zeros((bp, DN_HEADS, DN_HEAD_DIM, DN_HEAD_DIM), F32),
        jnp.zeros((bp, SC_CONV - 1, SC_WIDTH), F32),
        mk2d.reshape(bp, MEM_LEN, MEM_WIDTH), mv2d.reshape(bp, MEM_LEN, MEM_WIDTH),
        weights, n_seq=bp, seq_len=tp, nb=1, tt=MIX_TILE)
    h_s, hn_s, lg_s, s_dnc, s_dns, s_scc = _mixer(
        x_sample.reshape(n_s, D_MODEL), state_dn_conv[0], state_dn[0], state_sc_conv[0],
        cache_mem_k[0].reshape(bs, MEM_LEN, MEM_WIDTH), cache_mem_v[0].reshape(bs, MEM_LEN, MEM_WIDTH),
        weights, n_seq=bs, seq_len=ts, nb=bs, tt=ts)

    idx, gates, rank, counts = _router(jnp.concatenate([lg_p, lg_s], axis=0))
    counts = counts[0, :N_EXPERTS]
    n_blk_e = (counts + ROW_BLOCK - 1) // ROW_BLOCK
    blk_end = jnp.cumsum(n_blk_e)
    blk_start = blk_end - n_blk_e
    row_start = blk_start * ROW_BLOCK
    dest = (row_start[idx[:, :TOP_K]] + rank[:, :TOP_K]).reshape(n_tok * TOP_K)

    n_blocks = (n_tok * TOP_K) // ROW_BLOCK + N_EXPERTS
    bi = jnp.arange(n_blocks, dtype=jnp.int32)
    n_valid = blk_end[-1].astype(jnp.int32)
    bclip = jnp.minimum(bi, n_valid - 1)
    blk_e = jnp.minimum(jnp.searchsorted(blk_end, bclip, side='right'), N_EXPERTS - 1).astype(jnp.int32)
    pad_start = (row_start + counts).astype(jnp.int32)
    pad_n = (n_blk_e * ROW_BLOCK - counts).astype(jnp.int32)

    dest = dest.astype(jnp.int32)
    n_valid = n_valid.reshape(1)

    xs = _dispatch(pad_start, pad_n, n_valid, dest, hn_p, hn_s, n_blocks)
    ys = _experts(blk_e, n_valid, xs,
                  w_gate[0], b_gate[0].reshape(N_EXPERTS, 1, -1), w_up[0], b_up[0].reshape(N_EXPERTS, 1, -1),
                  w_down[0], b_down[0].reshape(N_EXPERTS, 1, -1))
    fw = final_norm_w.reshape(1, D_MODEL)
    y_p = _combine(dest, gates, h_p, fw, ys, tile0=0)
    y_s = _combine(dest, gates, h_s, fw, ys, tile0=n_p // TOK_TILE)

    return (y_p.reshape(bp, tp, D_MODEL), y_s.reshape(bs, ts, D_MODEL),
            p_dns[None], p_dnc[None], p_scc[None],
            mk2d.reshape(1, bp, MEM_LEN, MEM_HEADS, MEM_HEAD_DIM), mv2d.reshape(1, bp, MEM_LEN, MEM_HEADS, MEM_HEAD_DIM),
            s_dns[None], s_dnc[None], s_scc[None])
```

```python
import functools

import jax
import jax.numpy as jnp
from jax import lax
from jax.experimental import pallas as pl
from jax.experimental.pallas import tpu as pltpu

F32 = jnp.float32
BF16 = jnp.bfloat16

D_MODEL = 1024
CHUNK = 64
EPS = 1e-6
DN_HEADS = 4
DN_HEAD_DIM = 128
DN_WIDTH = DN_HEADS * DN_HEAD_DIM
QKV_WIDTH = 3 * DN_WIDTH
DN_CONV = 4
SC_WIDTH = 256
SC_CONV = 3
MEM_LEN = 256
MEM_HEADS = 4
MEM_HEAD_DIM = 64
MEM_WIDTH = MEM_HEADS * MEM_HEAD_DIM
N_EXPERTS = 32
TOP_K = 4
SWIGLU_ALPHA = 1.702
SWIGLU_LIMIT = 7.0

LANES = 128
CONV_PAD = 8

OFF_QKV = 0
OFF_DNG = OFF_QKV + QKV_WIDTH
OFF_SC = OFF_DNG + DN_WIDTH
OFF_MQ = OFF_SC + 3 * SC_WIDTH
OFF_GATE = OFF_MQ + MEM_WIDTH
OFF_AB = OFF_GATE + 3 * D_MODEL
IN_PERM_WIDTH = OFF_AB + LANES

MIX_TILE = 512
TOK_TILE = 256
ROW_BLOCK = 256
ISSUE_UNROLL = 4
VMEM_LIMIT = 56 * 1024 * 1024
NEG_BIG = -1e30


def _dot(a, b):
    return jnp.dot(a, b, preferred_element_type=F32)


def _dot_nt(a, b):
    return lax.dot_general(a, b, (((1,), (1,)), ((), ())), preferred_element_type=F32)


def _dot_tn(a, b):
    return lax.dot_general(a, b, (((0,), (0,)), ((), ())), preferred_element_type=F32)


def _sigmoid(x):
    return 1.0 / (1.0 + jnp.exp(-x))


def _softplus(x):
    return jnp.maximum(x, 0.0) + jnp.log1p(jnp.exp(-jnp.abs(x)))


def _for_each(n, body):
    if n == 1:
        body(0)
    else:
        def step(i, carry):
            body(i)
            return carry
        lax.fori_loop(0, n, step, 0)


def _memkv_kernel(mem_ref, nw_ref, w_ref, k_ref, v_ref):
    x = mem_ref[...]
    xn = x * lax.rsqrt(jnp.mean(x * x, axis=-1, keepdims=True) + EPS) * nw_ref[...]
    kv = _dot(xn.astype(BF16), w_ref[...])
    k_ref[...] = kv[:, :MEM_WIDTH]
    v_ref[...] = kv[:, MEM_WIDTH:]


def _memkv(mem2d, norm_w, w_kv_bf16):
    rows = mem2d.shape[0]
    grid = rows // MEM_LEN
    return pl.pallas_call(
        _memkv_kernel,
        grid=(grid,),
        in_specs=[
            pl.BlockSpec((MEM_LEN, D_MODEL), lambda i: (i, 0)),
            pl.BlockSpec((1, D_MODEL), lambda i: (0, 0)),
            pl.BlockSpec((D_MODEL, 2 * MEM_WIDTH), lambda i: (0, 0)),
        ],
        out_specs=[
            pl.BlockSpec((MEM_LEN, MEM_WIDTH), lambda i: (i, 0)),
            pl.BlockSpec((MEM_LEN, MEM_WIDTH), lambda i: (i, 0)),
        ],
        out_shape=[jax.ShapeDtypeStruct((rows, MEM_WIDTH), F32)] * 2,
        name="memkv",
    )(mem2d, norm_w, w_kv_bf16)


def _unit_lower_inverse(a, eye, size):
    inv = eye - a
    power = a
    span = 2
    while span < size:
        power = _dot(power, power)
        inv = _dot(inv, eye + power)
        span *= 2
    return inv


def _mixer_kernel(x_ref, dnc_in, dns_in, scc_in, mk_ref, mv_ref, w_in, w_br, w_o, n1_ref, n2_ref, dcw_ref, scw_ref,
                  alog_ref, dtb_ref, dnw_ref, wr_ref, br_ref,
                  h_ref, hn_ref, lg_ref, dnc_out, dns_out, scc_out,
                  xp, scp, q_s, k_s, v_s, gb_s, o_s, mq_s, ysc_s, ymem_s, wv_s, wk_s, qk_s, qd_s, kd_s, cd_s,
                  *, nb, tt, chunk):
    t_idx = pl.program_id(1)
    rows = nb * tt
    n_chunk = tt // chunk

    @pl.when(t_idx == 0)
    def _():
        xp[:, CONV_PAD - (DN_CONV - 1):CONV_PAD, :] = dnc_in[...]
        scp[:, CONV_PAD - (SC_CONV - 1):CONV_PAD, :] = scc_in[...]
        dns_out[...] = dns_in[...]

    x = x_ref[...]
    xn = (x * lax.rsqrt(jnp.mean(x * x, axis=-1, keepdims=True) + EPS) * n1_ref[...]).astype(BF16)

    def proj(off, width):
        return _dot(xn, w_in[:, off:off + width])

    qkv_pre = proj(OFF_QKV, QKV_WIDTH)
    for s in range(nb):
        xp[s, CONV_PAD:CONV_PAD + tt, :] = qkv_pre[s * tt:(s + 1) * tt, :]
    sc = proj(OFF_SC, 3 * SC_WIDTH)
    sc_b = sc[:, :SC_WIDTH]
    sc_ch = sc[:, SC_WIDTH:2 * SC_WIDTH] * sc[:, 2 * SC_WIDTH:]
    for s in range(nb):
        scp[s, CONV_PAD:CONV_PAD + tt, :] = sc_ch[s * tt:(s + 1) * tt, :]
    mq_s[...] = proj(OFF_MQ, MEM_WIDTH)

    ab = proj(OFF_AB, LANES)
    lane = lax.broadcasted_iota(jnp.int32, (rows, LANES), 1)
    g_log = -jnp.exp(alog_ref[...]) * _softplus(ab + dtb_ref[...])
    gb_s[...] = jnp.where(lane < DN_HEADS, g_log, _sigmoid(ab))

    def conv_seq(s):
        base = CONV_PAD - (DN_CONV - 1)
        acc = dcw_ref[0:1, :] * xp[s, pl.ds(base, tt), :]
        for j in range(1, DN_CONV):
            acc = acc + dcw_ref[j:j + 1, :] * xp[s, pl.ds(base + j, tt), :]
        act = acc * _sigmoid(acc)
        r0 = pl.multiple_of(s * tt, tt)
        for hd in range(DN_HEADS):
            lo = hd * DN_HEAD_DIM
            qh = act[:, lo:lo + DN_HEAD_DIM]
            kh = act[:, DN_WIDTH + lo:DN_WIDTH + lo + DN_HEAD_DIM]
            q_s[pl.ds(r0, tt), lo:lo + DN_HEAD_DIM] = (
                qh * lax.rsqrt(jnp.sum(qh * qh, axis=-1, keepdims=True) + EPS) * (DN_HEAD_DIM ** -0.5))
            k_s[pl.ds(r0, tt), lo:lo + DN_HEAD_DIM] = (
                kh * lax.rsqrt(jnp.sum(kh * kh, axis=-1, keepdims=True) + EPS))
        v_s[pl.ds(r0, tt), :] = act[:, 2 * DN_WIDTH:]
        tail = xp[s, pl.ds(tt + base, DN_CONV - 1), :]
        dnc_out[s] = tail
        xp[s, pl.ds(base, DN_CONV - 1), :] = tail

        base2 = CONV_PAD - (SC_CONV - 1)
        acc2 = scw_ref[0:1, :] * scp[s, pl.ds(base2, tt), :]
        for j in range(1, SC_CONV):
            acc2 = acc2 + scw_ref[j:j + 1, :] * scp[s, pl.ds(base2 + j, tt), :]
        ysc_s[pl.ds(r0, tt), :] = acc2
        tail2 = scp[s, pl.ds(tt + base2, SC_CONV - 1), :]
        scc_out[s] = tail2
        scp[s, pl.ds(base2, SC_CONV - 1), :] = tail2

    _for_each(nb, conv_seq)

    hl = DN_HEADS * chunk
    shift = chunk.bit_length() - 1
    ri = lax.broadcasted_iota(jnp.int32, (hl, hl), 0)
    ci = lax.broadcasted_iota(jnp.int32, (hl, hl), 1)
    same_head = lax.shift_right_logical(ri, shift) == lax.shift_right_logical(ci, shift)
    causal = same_head & (ri >= ci)
    strict = same_head & (ri > ci)
    eye = (ri == ci).astype(F32)
    ri1 = lax.broadcasted_iota(jnp.int32, (chunk, chunk), 0)
    ci1 = lax.broadcasted_iota(jnp.int32, (chunk, chunk), 1)
    tril = (ri1 >= ci1).astype(F32)
    triu = (ri1 <= ci1).astype(F32)

    def stack_heads(ref, r0):
        return jnp.concatenate(
            [ref[r0:r0 + chunk, hd * DN_HEAD_DIM:(hd + 1) * DN_HEAD_DIM] for hd in range(DN_HEADS)], axis=0)

    for c in range(nb * n_chunk):
        r0 = c * chunk
        gb = gb_s[r0:r0 + chunk, :]
        cum = _dot(tril, gb)
        cum_t = _dot_tn(gb, triu)
        cc = jnp.concatenate([cum[:, hd:hd + 1] for hd in range(DN_HEADS)], axis=0)
        cr = jnp.concatenate([cum_t[hd:hd + 1, :] for hd in range(DN_HEADS)], axis=1)
        beta = jnp.concatenate([gb[:, DN_HEADS + hd:DN_HEADS + hd + 1] for hd in range(DN_HEADS)], axis=0)
        c_last = jnp.concatenate(
            [jnp.broadcast_to(cum[chunk - 1:chunk, hd:hd + 1], (chunk, 1)) for hd in range(DN_HEADS)], axis=0)
        qst, kst, vst = stack_heads(q_s, r0), stack_heads(k_s, r0), stack_heads(v_s, r0)
        decay = jnp.where(causal, jnp.exp(jnp.where(causal, cc - cr, 0.0)), 0.0)
        a_mat = jnp.where(strict, decay * _dot_nt(kst, kst), 0.0) * beta
        t_inv = _unit_lower_inverse(a_mat, eye, chunk)
        e_cum = jnp.exp(cc)
        w = _dot(t_inv, jnp.concatenate([beta * vst, (beta * e_cum) * kst], axis=1))
        wv_s[c] = w[:, :DN_HEAD_DIM]
        wk_s[c] = w[:, DN_HEAD_DIM:]
        qk_s[c] = _dot_nt(qst, kst) * decay
        qd_s[c] = e_cum * qst
        kd_s[c] = jnp.exp(c_last - cc) * kst
        cd_s[c] = jnp.broadcast_to(jnp.exp(c_last), (hl, DN_HEAD_DIM))

    for c in range(nb * n_chunk):
        s = c // n_chunk
        r0 = c * chunk
        states, us = [], []
        for hd in range(DN_HEADS):
            hr = slice(hd * chunk, (hd + 1) * chunk)
            state = dns_out[s, hd]
            states.append(state)
            us.append(wv_s[c, hr, :] - _dot(wk_s[c, hr, :], state))
        o_intra = _dot(qk_s[c], jnp.concatenate(us, axis=0))
        for hd in range(DN_HEADS):
            hr = slice(hd * chunk, (hd + 1) * chunk)
            lo = hd * DN_HEAD_DIM
            o_s[r0:r0 + chunk, lo:lo + DN_HEAD_DIM] = _dot(qd_s[c, hr, :], states[hd]) + o_intra[hr, :]
            dns_out[s, hd] = (cd_s[c, hd * chunk:hd * chunk + 1, :] * states[hd]
                              + _dot_tn(kd_s[c, hr, :], us[hd]))

    def attn_seq(s):
        r0 = pl.multiple_of(s * tt, tt)
        mq = mq_s[pl.ds(r0, tt), :]
        for hd in range(MEM_HEADS):
            lo = hd * MEM_HEAD_DIM
            qh = mq[:, lo:lo + MEM_HEAD_DIM].astype(BF16)
            kh = mk_ref[s, :, lo:lo + MEM_HEAD_DIM].astype(BF16)
            vh = mv_ref[s, :, lo:lo + MEM_HEAD_DIM].astype(BF16)
            sc_h = _dot_nt(qh, kh) * (MEM_HEAD_DIM ** -0.5)
            p = jnp.exp(sc_h - jnp.max(sc_h, axis=-1, keepdims=True))
            denom = jnp.sum(p, axis=-1, keepdims=True)
            ymem_s[pl.ds(r0, tt), lo:lo + MEM_HEAD_DIM] = _dot(p.astype(BF16), vh) / denom

    _for_each(nb, attn_seq)

    dn_gate = proj(OFF_DNG, DN_WIDTH)
    o_all = o_s[...]
    y_heads = []
    for hd in range(DN_HEADS):
        lo = hd * DN_HEAD_DIM
        oh = o_all[:, lo:lo + DN_HEAD_DIM]
        oh = oh * lax.rsqrt(jnp.mean(oh * oh, axis=-1, keepdims=True) + EPS) * dnw_ref[...]
        gh = dn_gate[:, lo:lo + DN_HEAD_DIM]
        y_heads.append(oh * (gh * _sigmoid(gh)))
    y_dn = jnp.concatenate(y_heads, axis=-1).astype(BF16)
    y_sc = (sc_b * ysc_s[...]).astype(BF16)
    y_mem = ymem_s[...].astype(BF16)

    merged = _sigmoid(proj(OFF_GATE, D_MODEL)) * _dot(y_dn, w_br[0:DN_WIDTH, :])
    merged = merged + _sigmoid(proj(OFF_GATE + D_MODEL, D_MODEL)) * _dot(y_sc, w_br[DN_WIDTH:DN_WIDTH + SC_WIDTH, :])
    merged = merged + _sigmoid(proj(OFF_GATE + 2 * D_MODEL, D_MODEL)) * _dot(y_mem, w_br[DN_WIDTH + SC_WIDTH:, :])
    h = x + _dot(merged.astype(BF16), w_o[...])
    h_ref[...] = h
    hn = h * lax.rsqrt(jnp.mean(h * h, axis=-1, keepdims=True) + EPS) * n2_ref[...]
    hn_ref[...] = hn
    lg_ref[...] = _dot(hn, wr_ref[...]) + br_ref[...]


def _mixer(x2d, dnc_in, dns_in, scc_in, mk, mv, weights, *, n_seq, seq_len, nb, tt):
    chunk = CHUNK if seq_len % CHUNK == 0 else seq_len
    rows = nb * tt
    n_t = seq_len // tt
    total_rows = n_seq * seq_len
    n_chunks = rows // chunk
    hl = DN_HEADS * chunk
    grid = (n_seq // nb, n_t)
    const = lambda b, t: (0, 0)
    seq3 = lambda b, t: (b, 0, 0)

    def tok(b, t):
        return (b * n_t + t, 0)

    tok_out = tok

    (w_in, w_br, w_o, n1, n2, dcw, scw, alog, dtb, dnw, wr, br) = weights
    in_specs = [
        pl.BlockSpec((rows, D_MODEL), tok),
        pl.BlockSpec((nb, DN_CONV - 1, QKV_WIDTH), seq3),
        pl.BlockSpec((nb, DN_HEADS, DN_HEAD_DIM, DN_HEAD_DIM), lambda b, t: (b, 0, 0, 0)),
        pl.BlockSpec((nb, SC_CONV - 1, SC_WIDTH), seq3),
        pl.BlockSpec((nb, MEM_LEN, MEM_WIDTH), seq3),
        pl.BlockSpec((nb, MEM_LEN, MEM_WIDTH), seq3),
        pl.BlockSpec(w_in.shape, const, pipeline_mode=pl.Buffered(1)),
        pl.BlockSpec(w_br.shape, const, pipeline_mode=pl.Buffered(1)),
        pl.BlockSpec(w_o.shape, const, pipeline_mode=pl.Buffered(1)),
        pl.BlockSpec(n1.shape, const),
        pl.BlockSpec(n2.shape, const),
        pl.BlockSpec(dcw.shape, const),
        pl.BlockSpec(scw.shape, const),
        pl.BlockSpec(alog.shape, const),
        pl.BlockSpec(dtb.shape, const),
        pl.BlockSpec(dnw.shape, const),
        pl.BlockSpec(wr.shape, const),
        pl.BlockSpec(br.shape, const),
    ]
    out_shape = [
        jax.ShapeDtypeStruct((total_rows, D_MODEL), F32),
        jax.ShapeDtypeStruct((total_rows, D_MODEL), F32),
        jax.ShapeDtypeStruct((total_rows, LANES), F32),
        jax.ShapeDtypeStruct((n_seq, DN_CONV - 1, QKV_WIDTH), F32),
        jax.ShapeDtypeStruct((n_seq, DN_HEADS, DN_HEAD_DIM, DN_HEAD_DIM), F32),
        jax.ShapeDtypeStruct((n_seq, SC_CONV - 1, SC_WIDTH), F32),
    ]
    out_specs = [
        pl.BlockSpec((rows, D_MODEL), tok_out),
        pl.BlockSpec((rows, D_MODEL), tok_out),
        pl.BlockSpec((rows, LANES), tok_out),
        pl.BlockSpec((nb, DN_CONV - 1, QKV_WIDTH), seq3),
        pl.BlockSpec((nb, DN_HEADS, DN_HEAD_DIM, DN_HEAD_DIM), lambda b, t: (b, 0, 0, 0)),
        pl.BlockSpec((nb, SC_CONV - 1, SC_WIDTH), seq3),
    ]
    args = [x2d, dnc_in, dns_in, scc_in, mk, mv, w_in, w_br, w_o, n1, n2, dcw, scw, alog, dtb, dnw, wr, br]
    scratch = [
        pltpu.VMEM((nb, CONV_PAD + tt, QKV_WIDTH), F32),
        pltpu.VMEM((nb, CONV_PAD + tt, SC_WIDTH), F32),
        pltpu.VMEM((rows, DN_WIDTH), F32),
        pltpu.VMEM((rows, DN_WIDTH), F32),
        pltpu.VMEM((rows, DN_WIDTH), F32),
        pltpu.VMEM((rows, LANES), F32),
        pltpu.VMEM((rows, DN_WIDTH), F32),
        pltpu.VMEM((rows, MEM_WIDTH), F32),
        pltpu.VMEM((rows, SC_WIDTH), F32),
        pltpu.VMEM((rows, MEM_WIDTH), F32),
        pltpu.VMEM((n_chunks, hl, DN_HEAD_DIM), F32),
        pltpu.VMEM((n_chunks, hl, DN_HEAD_DIM), F32),
        pltpu.VMEM((n_chunks, hl, hl), F32),
        pltpu.VMEM((n_chunks, hl, DN_HEAD_DIM), F32),
        pltpu.VMEM((n_chunks, hl, DN_HEAD_DIM), F32),
        pltpu.VMEM((n_chunks, hl, DN_HEAD_DIM), F32),
    ]
    return pl.pallas_call(
        functools.partial(_mixer_kernel, nb=nb, tt=tt, chunk=chunk),
        grid=grid,
        in_specs=in_specs,
        out_specs=out_specs,
        out_shape=out_shape,
        scratch_shapes=scratch,
        compiler_params=pltpu.CompilerParams(
            dimension_semantics=("arbitrary", "arbitrary"), vmem_limit_bytes=VMEM_LIMIT),
        name="mixer",
    )(*args)


def _router_kernel(lg_ref, idx_ref, gate_ref, rank_ref, cnt_ref, carry):
    i = pl.program_id(0)

    @pl.when(i == 0)
    def _():
        carry[...] = jnp.zeros_like(carry)

    work = lg_ref[...]
    tm = work.shape[0]
    lane = lax.broadcasted_iota(jnp.int32, (tm, LANES), 1).astype(F32)
    idxs, vals = [], []
    for _ in range(TOP_K):
        m = jnp.max(work, axis=-1, keepdims=True)
        ik = jnp.min(jnp.where(work == m, lane, float(LANES)), axis=-1, keepdims=True)
        idxs.append(ik)
        vals.append(m)
        work = jnp.where(lane == ik, -jnp.inf, work)
    exps = [jnp.exp(v - vals[0]) for v in vals]
    denom = exps[0] + exps[1] + exps[2] + exps[3]
    hot = jnp.zeros((tm, LANES), F32)
    for ik in idxs:
        hot = hot + (lane == ik).astype(F32)
    ri = lax.broadcasted_iota(jnp.int32, (tm, tm), 0)
    ci = lax.broadcasted_iota(jnp.int32, (tm, tm), 1)
    before = (ri > ci).astype(BF16)
    prefix = _dot(before, hot.astype(BF16)) + carry[...]
    idx_out = jnp.zeros((tm, LANES), F32)
    gate_out = jnp.zeros((tm, LANES), F32)
    rank_out = jnp.zeros((tm, LANES), F32)
    for k in range(TOP_K):
        rk = jnp.sum(jnp.where(lane == idxs[k], prefix, 0.0), axis=-1, keepdims=True)
        idx_out = jnp.where(lane == k, idxs[k], idx_out)
        gate_out = jnp.where(lane == k, exps[k] / denom, gate_out)
        rank_out = jnp.where(lane == k, rk, rank_out)
    idx_ref[...] = idx_out.astype(jnp.int32)
    gate_ref[...] = gate_out
    rank_ref[...] = rank_out.astype(jnp.int32)
    carry[...] = carry[...] + jnp.sum(hot, axis=0, keepdims=True)
    cnt_ref[...] = carry[...].astype(jnp.int32)


def _router(logits):
    n_tok = logits.shape[0]
    tile = lambda i: (i, 0)
    return pl.pallas_call(
        _router_kernel,
        grid=(n_tok // TOK_TILE,),
        in_specs=[pl.BlockSpec((TOK_TILE, LANES), tile)],
        out_specs=[pl.BlockSpec((TOK_TILE, LANES), tile)] * 3 + [pl.BlockSpec((1, LANES), lambda i: (0, 0))],
        out_shape=[
            jax.ShapeDtypeStruct((n_tok, LANES), jnp.int32),
            jax.ShapeDtypeStruct((n_tok, LANES), F32),
            jax.ShapeDtypeStruct((n_tok, LANES), jnp.int32),
            jax.ShapeDtypeStruct((1, LANES), jnp.int32),
        ],
        scratch_shapes=[pltpu.VMEM((1, LANES), F32)],
        compiler_params=pltpu.CompilerParams(dimension_semantics=("arbitrary",)),
        name="router",
    )(logits)


def _dispatch_kernel(pad_start, pad_n, n_valid, dest_ref, hp_ref, hs_ref, xs_ref, zeros, sem, zsem, *,
                     n_tiles_p, n_blocks):
    i = pl.program_id(0)

    def scatter_rows(src_ref):
        def issue(j, c):
            for dt in range(ISSUE_UNROLL):
                t = j * ISSUE_UNROLL + dt
                for k in range(TOP_K):
                    pltpu.make_async_copy(
                        src_ref.at[pl.ds(t, 1)], xs_ref.at[pl.ds(dest_ref[t * TOP_K + k], 1)], sem).start()
            return c

        lax.fori_loop(0, TOK_TILE // ISSUE_UNROLL, issue, 0)
        for _ in range(TOP_K):
            pltpu.make_async_copy(src_ref, xs_ref.at[pl.ds(0, TOK_TILE)], sem).wait()

    @pl.when(i < n_tiles_p)
    def _():
        scatter_rows(hp_ref)

    @pl.when(i >= n_tiles_p)
    def _():
        scatter_rows(hs_ref)

    @pl.when(i == pl.num_programs(0) - 1)
    def _():
        zeros[...] = jnp.zeros_like(zeros)

        def pad_expert(e, c):
            def zero_copy(r):
                return pltpu.make_async_copy(
                    zeros.at[pl.ds(0, 1)], xs_ref.at[pl.ds(pad_start[e] + r, 1)], zsem)

            def issue_pad(r, cc):
                zero_copy(r).start()
                return cc

            def drain_pad(r, cc):
                zero_copy(r).wait()
                return cc

            lax.fori_loop(0, pad_n[e], issue_pad, 0)
            lax.fori_loop(0, pad_n[e], drain_pad, 0)
            return c

        lax.fori_loop(0, N_EXPERTS, pad_expert, 0)

        def zero_block(b, c):
            cp = pltpu.make_async_copy(
                zeros, xs_ref.at[pl.ds(pl.multiple_of(b * ROW_BLOCK, ROW_BLOCK), ROW_BLOCK)], zsem)
            cp.start()
            cp.wait()
            return c

        lax.fori_loop(n_valid[0], n_blocks, zero_block, 0)


def _dispatch(pad_start, pad_n, n_valid, dest_flat, hn_p, hn_s, n_blocks):
    n_tiles_p = hn_p.shape[0] // TOK_TILE
    n_tiles_s = hn_s.shape[0] // TOK_TILE
    grid_spec = pltpu.PrefetchScalarGridSpec(
        num_scalar_prefetch=3,
        grid=(n_tiles_p + n_tiles_s,),
        in_specs=[
            pl.BlockSpec((TOK_TILE * TOP_K,), lambda i, *_: (i,), memory_space=pltpu.SMEM),
            pl.BlockSpec((TOK_TILE, D_MODEL), lambda i, *_: (jnp.minimum(i, n_tiles_p - 1), 0)),
            pl.BlockSpec((TOK_TILE, D_MODEL), lambda i, *_: (jnp.maximum(i - n_tiles_p, 0), 0)),
        ],
        out_specs=pl.BlockSpec(memory_space=pl.ANY),
        scratch_shapes=[
            pltpu.VMEM((ROW_BLOCK, D_MODEL), F32),
            pltpu.SemaphoreType.DMA(()),
            pltpu.SemaphoreType.DMA(()),
        ],
    )
    return pl.pallas_call(
        functools.partial(_dispatch_kernel, n_tiles_p=n_tiles_p, n_blocks=n_blocks),
        grid_spec=grid_spec,
        out_shape=jax.ShapeDtypeStruct((n_blocks * ROW_BLOCK, D_MODEL), F32),
        compiler_params=pltpu.CompilerParams(dimension_semantics=("arbitrary",)),
        name="dispatch",
    )(pad_start, pad_n, n_valid, dest_flat, hn_p, hn_s)


def _experts_kernel(blk_e, n_valid, xs_ref, wg_ref, bg_ref, wu_ref, bu_ref, wd_ref, bd_ref, ys_ref,
                    wg_bf, wu_bf, wd_bf):
    i = pl.program_id(0)
    e = blk_e[i]
    prev = blk_e[jnp.maximum(i - 1, 0)]

    @pl.when((i == 0) | (e != prev))
    def _():
        wg_bf[...] = wg_ref[0].astype(BF16)
        wu_bf[...] = wu_ref[0].astype(BF16)
        wd_bf[...] = wd_ref[0].astype(BF16)

    @pl.when(i < n_valid[0])
    def _():
        x = xs_ref[...].astype(BF16)
        gl = jnp.minimum(_dot(x, wg_bf[...]) + bg_ref[0], SWIGLU_LIMIT)
        ul = jnp.clip(_dot(x, wu_bf[...]) + bu_ref[0], -SWIGLU_LIMIT, SWIGLU_LIMIT)
        act = (ul + 1.0) * (gl * _sigmoid(SWIGLU_ALPHA * gl))
        ys_ref[...] = _dot(act.astype(BF16), wd_bf[...]) + bd_ref[0]

    @pl.when(i >= n_valid[0])
    def _():
        ys_ref[...] = jnp.zeros_like(ys_ref)


def _experts(blk_e, n_valid, xs, w_gate, b_gate, w_up, b_up, w_down, b_down):
    n_rows = xs.shape[0]
    n_blocks = blk_e.shape[0]
    rows = lambda i, be, nv: (i, 0)
    wsel = lambda i, be, nv: (be[i], 0, 0)
    d_ff = w_gate.shape[-1]
    grid_spec = pltpu.PrefetchScalarGridSpec(
        num_scalar_prefetch=2,
        grid=(n_blocks,),
        in_specs=[
            pl.BlockSpec((ROW_BLOCK, D_MODEL), rows),
            pl.BlockSpec((1, D_MODEL, d_ff), wsel),
            pl.BlockSpec((1, 1, d_ff), wsel),
            pl.BlockSpec((1, D_MODEL, d_ff), wsel),
            pl.BlockSpec((1, 1, d_ff), wsel),
            pl.BlockSpec((1, d_ff, D_MODEL), wsel),
            pl.BlockSpec((1, 1, D_MODEL), wsel),
        ],
        out_specs=pl.BlockSpec((ROW_BLOCK, D_MODEL), rows),
        scratch_shapes=[
            pltpu.VMEM((D_MODEL, d_ff), BF16),
            pltpu.VMEM((D_MODEL, d_ff), BF16),
            pltpu.VMEM((d_ff, D_MODEL), BF16),
        ],
    )
    return pl.pallas_call(
        _experts_kernel,
        grid_spec=grid_spec,
        out_shape=jax.ShapeDtypeStruct((n_rows, D_MODEL), F32),
        compiler_params=pltpu.CompilerParams(
            dimension_semantics=("arbitrary",), vmem_limit_bytes=VMEM_LIMIT),
        name="experts",
    )(blk_e, n_valid, xs, w_gate, b_gate, w_up, b_up, w_down, b_down)


def _combine_kernel(dest_ref, gate_ref, h_ref, fw_ref, ys_ref, y_ref, buf, sem):
    def issue(j, c):
        for dt in range(ISSUE_UNROLL):
            t = j * ISSUE_UNROLL + dt
            for k in range(TOP_K):
                pltpu.make_async_copy(
                    ys_ref.at[pl.ds(dest_ref[t * TOP_K + k], 1)], buf.at[k, pl.ds(t, 1)], sem).start()
        return c

    lax.fori_loop(0, TOK_TILE // ISSUE_UNROLL, issue, 0)
    for k in range(TOP_K):
        pltpu.make_async_copy(ys_ref.at[pl.ds(0, TOK_TILE)], buf.at[k], sem).wait()

    gate = gate_ref[...]
    acc = h_ref[...]
    for k in range(TOP_K):
        acc = acc + gate[:, k:k + 1] * buf[k]
    y_ref[...] = acc * lax.rsqrt(jnp.mean(acc * acc, axis=-1, keepdims=True) + EPS) * fw_ref[...]


def _combine(dest_flat, gates, h_group, final_w, ys, *, tile0):
    n_tiles = h_group.shape[0] // TOK_TILE
    return pl.pallas_call(
        _combine_kernel,
        grid=(n_tiles,),
        in_specs=[
            pl.BlockSpec((TOK_TILE * TOP_K,), lambda i: (tile0 + i,), memory_space=pltpu.SMEM),
            pl.BlockSpec((TOK_TILE, LANES), lambda i: (tile0 + i, 0)),
            pl.BlockSpec((TOK_TILE, D_MODEL), lambda i: (i, 0)),
            pl.BlockSpec((1, D_MODEL), lambda i: (0, 0)),
            pl.BlockSpec(memory_space=pl.ANY),
        ],
        out_specs=pl.BlockSpec((TOK_TILE, D_MODEL), lambda i: (i, 0)),
        out_shape=jax.ShapeDtypeStruct((n_tiles * TOK_TILE, D_MODEL), F32),
        scratch_shapes=[
            pltpu.VMEM((TOP_K, TOK_TILE, D_MODEL), F32),
            pltpu.SemaphoreType.DMA(()),
        ],
        compiler_params=pltpu.CompilerParams(dimension_semantics=("arbitrary",)),
        name="combine",
    )(dest_flat, gates, h_group, final_w, ys)


def _permute_w_in(w):
    o_a = QKV_WIDTH
    o_g = o_a + 2 * DN_HEADS
    o_rest = o_g + DN_WIDTH
    pad = jnp.zeros((w.shape[0], LANES - 2 * DN_HEADS), w.dtype)
    return jnp.concatenate([w[:, :o_a], w[:, o_g:o_rest], w[:, o_rest:], w[:, o_a:o_g], pad], axis=1)


def _lane_row(v, fill=0.0):
    return jnp.concatenate([v.astype(F32), jnp.full((LANES - v.shape[0],), fill, F32)]).reshape(1, LANES)


def kernel(x_prompt, x_sample, mem_prompt, state_dn, state_dn_conv, state_sc_conv, cache_mem_k, cache_mem_v, w_in, dn_conv_w, dn_A_log, dn_dt_bias, dn_norm_w, sc_conv_w, mem_norm_w, w_mem_kv, w_br, w_o, norm1_w, norm2_w, w_router, b_router, w_gate, b_gate, w_up, b_up, w_down, b_down, final_norm_w):
    assert w_in.shape[0] == 1, "one layer"
    bp, tp, _ = x_prompt.shape
    bs, ts, _ = x_sample.shape
    n_p, n_s = bp * tp, bs * ts
    n_tok = n_p + n_s
    assert n_p % TOK_TILE == 0 and n_s % TOK_TILE == 0 and tp % MIX_TILE == 0

    weights = (
        _permute_w_in(w_in[0]).astype(BF16),
        w_br[0].astype(BF16),
        w_o[0].astype(BF16),
        norm1_w[0].reshape(1, D_MODEL),
        norm2_w[0].reshape(1, D_MODEL),
        dn_conv_w[0],
        sc_conv_w[0],
        _lane_row(dn_A_log[0]),
        _lane_row(dn_dt_bias[0]),
        dn_norm_w[0].reshape(1, DN_HEAD_DIM),
        jnp.concatenate([w_router[0], jnp.zeros((D_MODEL, LANES - N_EXPERTS), F32)], axis=1),
        _lane_row(b_router[0], NEG_BIG),
    )

    mk2d, mv2d = _memkv(mem_prompt.reshape(bp * MEM_LEN, D_MODEL), mem_norm_w[0].reshape(1, D_MODEL),
                        w_mem_kv[0].astype(BF16))

    h_p, hn_p, lg_p, p_dnc, p_dns, p_scc = _mixer(
        x_prompt.reshape(n_p, D_MODEL),
        jnp.zeros((bp, DN_CONV - 1, QKV_WIDTH), F32),
        jnp.zeros((bp, DN_HEADS, DN_HEAD_DIM, DN_HEAD_DIM), F32),
        jnp.zeros((bp, SC_CONV - 1, SC_WIDTH), F32),
        mk2d.reshape(bp, MEM_LEN, MEM_WIDTH), mv2d.reshape(bp, MEM_LEN, MEM_WIDTH),
        weights, n_seq=bp, seq_len=tp, nb=1, tt=MIX_TILE)
    h_s, hn_s, lg_s, s_dnc, s_dns, s_scc = _mixer(
        x_sample.reshape(n_s, D_MODEL), state_dn_conv[0], state_dn[0], state_sc_conv[0],
        cache_mem_k[0].reshape(bs, MEM_LEN, MEM_WIDTH), cache_mem_v[0].reshape(bs, MEM_LEN, MEM_WIDTH),
        weights, n_seq=bs, seq_len=ts, nb=bs, tt=ts)

    idx, gates, rank, counts = _router(jnp.concatenate([lg_p, lg_s], axis=0))
    counts = counts[0, :N_EXPERTS]
    n_blk_e = (counts + ROW_BLOCK - 1) // ROW_BLOCK
    blk_end = jnp.cumsum(n_blk_e)
    blk_start = blk_end - n_blk_e
    row_start = blk_start * ROW_BLOCK
    experts = jnp.arange(N_EXPERTS, dtype=jnp.int32)
    start_of = jnp.sum(jnp.where(idx[:, :TOP_K, None] == experts, row_start, 0), axis=-1)
    dest = (start_of + rank[:, :TOP_K]).reshape(n_tok * TOP_K)

    n_blocks = (n_tok * TOP_K) // ROW_BLOCK + N_EXPERTS
    bi = jnp.arange(n_blocks, dtype=jnp.int32)
    n_valid = blk_end[-1].astype(jnp.int32)
    bclip = jnp.minimum(bi, n_valid - 1)
    blk_e = jnp.sum((bclip[:, None] >= blk_end[None, :]).astype(jnp.int32), axis=1)
    pad_start = (row_start + counts).astype(jnp.int32)
    pad_n = (n_blk_e * ROW_BLOCK - counts).astype(jnp.int32)

    dest = dest.astype(jnp.int32)
    n_valid = n_valid.reshape(1)

    xs = _dispatch(pad_start, pad_n, n_valid, dest, hn_p, hn_s, n_blocks)
    ys = _experts(blk_e, n_valid, xs,
                  w_gate[0], b_gate[0].reshape(N_EXPERTS, 1, -1), w_up[0], b_up[0].reshape(N_EXPERTS, 1, -1),
                  w_down[0], b_down[0].reshape(N_EXPERTS, 1, -1))
    fw = final_norm_w.reshape(1, D_MODEL)
    y_p = _combine(dest, gates, h_p, fw, ys, tile0=0)
    y_s = _combine(dest, gates, h_s, fw, ys, tile0=n_p // TOK_TILE)

    return (y_p.reshape(bp, tp, D_MODEL), y_s.reshape(bs, ts, D_MODEL),
            p_dns[None], p_dnc[None], p_scc[None],
            mk2d.reshape(1, bp, MEM_LEN, MEM_HEADS, MEM_HEAD_DIM), mv2d.reshape(1, bp, MEM_LEN, MEM_HEADS, MEM_HEAD_DIM),
            s_dns[None], s_dnc[None], s_scc[None])
```

```python
import functools

import jax
import jax.numpy as jnp
from jax import lax
from jax.experimental import pallas as pl
from jax.experimental.pallas import tpu as pltpu

F32 = jnp.float32
BF16 = jnp.bfloat16

D_MODEL = 1024
CHUNK = 64
EPS = 1e-6
DN_HEADS = 4
DN_HEAD_DIM = 128
DN_WIDTH = DN_HEADS * DN_HEAD_DIM
QKV_WIDTH = 3 * DN_WIDTH
DN_CONV = 4
SC_WIDTH = 256
SC_CONV = 3
MEM_LEN = 256
MEM_HEADS = 4
MEM_HEAD_DIM = 64
MEM_WIDTH = MEM_HEADS * MEM_HEAD_DIM
N_EXPERTS = 32
TOP_K = 4
SWIGLU_ALPHA = 1.702
SWIGLU_LIMIT = 7.0

LANES = 128
CONV_PAD = 8

OFF_QKV = 0
OFF_DNG = OFF_QKV + QKV_WIDTH
OFF_SC = OFF_DNG + DN_WIDTH
OFF_MQ = OFF_SC + 3 * SC_WIDTH
OFF_GATE = OFF_MQ + MEM_WIDTH
OFF_AB = OFF_GATE + 3 * D_MODEL
IN_PERM_WIDTH = OFF_AB + LANES

MIX_TILE = 512
TOK_TILE = 256
ROW_BLOCK = 256
INVERT_UNROLL = 8
TRASH_TOK = 256
PAIR_TOK_BITS = 16
PAIR_TOK_MASK = (1 << PAIR_TOK_BITS) - 1
VMEM_LIMIT = 56 * 1024 * 1024
NEG_BIG = -1e30


def _dot(a, b):
    return jnp.dot(a, b, preferred_element_type=F32)


def _dot_nt(a, b):
    return lax.dot_general(a, b, (((1,), (1,)), ((), ())), preferred_element_type=F32)


def _dot_tn(a, b):
    return lax.dot_general(a, b, (((0,), (0,)), ((), ())), preferred_element_type=F32)


def _sigmoid(x):
    return 1.0 / (1.0 + jnp.exp(-x))


def _softplus(x):
    return jnp.maximum(x, 0.0) + jnp.log1p(jnp.exp(-jnp.abs(x)))


def _for_each(n, body):
    if n == 1:
        body(0)
    else:
        def step(i, carry):
            body(i)
            return carry
        lax.fori_loop(0, n, step, 0)


def _memkv_kernel(mem_ref, nw_ref, w_ref, k_ref, v_ref):
    x = mem_ref[...]
    xn = x * lax.rsqrt(jnp.mean(x * x, axis=-1, keepdims=True) + EPS) * nw_ref[...]
    kv = _dot(xn.astype(BF16), w_ref[...])
    k_ref[...] = kv[:, :MEM_WIDTH]
    v_ref[...] = kv[:, MEM_WIDTH:]


def _memkv(mem2d, norm_w, w_kv_bf16):
    rows = mem2d.shape[0]
    grid = rows // MEM_LEN
    return pl.pallas_call(
        _memkv_kernel,
        grid=(grid,),
        in_specs=[
            pl.BlockSpec((MEM_LEN, D_MODEL), lambda i: (i, 0)),
            pl.BlockSpec((1, D_MODEL), lambda i: (0, 0)),
            pl.BlockSpec((D_MODEL, 2 * MEM_WIDTH), lambda i: (0, 0)),
        ],
        out_specs=[
            pl.BlockSpec((MEM_LEN, MEM_WIDTH), lambda i: (i, 0)),
            pl.BlockSpec((MEM_LEN, MEM_WIDTH), lambda i: (i, 0)),
        ],
        out_shape=[jax.ShapeDtypeStruct((rows, MEM_WIDTH), F32)] * 2,
        name="memkv",
    )(mem2d, norm_w, w_kv_bf16)


def _unit_lower_inverse(a, eye, size):
    inv = eye - a
    power = a
    span = 2
    while span < size:
        power = _dot(power, power)
        inv = _dot(inv, eye + power)
        span *= 2
    return inv


def _mixer_kernel(x_ref, dnc_in, dns_in, scc_in, mk_ref, mv_ref, w_in, w_br, w_o, n1_ref, n2_ref, dcw_ref, scw_ref,
                  alog_ref, dtb_ref, dnw_ref, wr_ref, br_ref, hn_all_ref,
                  h_ref, hn_ref, lg_ref, dnc_out, dns_out, scc_out,
                  xp, scp, q_s, k_s, v_s, gb_s, o_s, mq_s, ysc_s, ymem_s, wv_s, wk_s, qk_s, qd_s, kd_s, cd_s,
                  *, nb, tt, chunk):
    t_idx = pl.program_id(1)
    rows = nb * tt
    n_chunk = tt // chunk

    @pl.when(t_idx == 0)
    def _():
        xp[:, CONV_PAD - (DN_CONV - 1):CONV_PAD, :] = dnc_in[...]
        scp[:, CONV_PAD - (SC_CONV - 1):CONV_PAD, :] = scc_in[...]
        dns_out[...] = dns_in[...]

    x = x_ref[...]
    xn = (x * lax.rsqrt(jnp.mean(x * x, axis=-1, keepdims=True) + EPS) * n1_ref[...]).astype(BF16)

    def proj(off, width):
        return _dot(xn, w_in[:, off:off + width])

    qkv_pre = proj(OFF_QKV, QKV_WIDTH)
    for s in range(nb):
        xp[s, CONV_PAD:CONV_PAD + tt, :] = qkv_pre[s * tt:(s + 1) * tt, :]
    sc = proj(OFF_SC, 3 * SC_WIDTH)
    sc_b = sc[:, :SC_WIDTH]
    sc_ch = sc[:, SC_WIDTH:2 * SC_WIDTH] * sc[:, 2 * SC_WIDTH:]
    for s in range(nb):
        scp[s, CONV_PAD:CONV_PAD + tt, :] = sc_ch[s * tt:(s + 1) * tt, :]
    mq_s[...] = proj(OFF_MQ, MEM_WIDTH)

    ab = proj(OFF_AB, LANES)
    lane = lax.broadcasted_iota(jnp.int32, (rows, LANES), 1)
    g_log = -jnp.exp(alog_ref[...]) * _softplus(ab + dtb_ref[...])
    gb_s[...] = jnp.where(lane < DN_HEADS, g_log, _sigmoid(ab))

    def conv_seq(s):
        base = CONV_PAD - (DN_CONV - 1)
        acc = dcw_ref[0:1, :] * xp[s, pl.ds(base, tt), :]
        for j in range(1, DN_CONV):
            acc = acc + dcw_ref[j:j + 1, :] * xp[s, pl.ds(base + j, tt), :]
        act = acc * _sigmoid(acc)
        r0 = pl.multiple_of(s * tt, tt)
        for hd in range(DN_HEADS):
            lo = hd * DN_HEAD_DIM
            qh = act[:, lo:lo + DN_HEAD_DIM]
            kh = act[:, DN_WIDTH + lo:DN_WIDTH + lo + DN_HEAD_DIM]
            q_s[pl.ds(r0, tt), lo:lo + DN_HEAD_DIM] = (
                qh * lax.rsqrt(jnp.sum(qh * qh, axis=-1, keepdims=True) + EPS) * (DN_HEAD_DIM ** -0.5))
            k_s[pl.ds(r0, tt), lo:lo + DN_HEAD_DIM] = (
                kh * lax.rsqrt(jnp.sum(kh * kh, axis=-1, keepdims=True) + EPS))
        v_s[pl.ds(r0, tt), :] = act[:, 2 * DN_WIDTH:]
        tail = xp[s, pl.ds(tt + base, DN_CONV - 1), :]
        dnc_out[s] = tail
        xp[s, pl.ds(base, DN_CONV - 1), :] = tail

        base2 = CONV_PAD - (SC_CONV - 1)
        acc2 = scw_ref[0:1, :] * scp[s, pl.ds(base2, tt), :]
        for j in range(1, SC_CONV):
            acc2 = acc2 + scw_ref[j:j + 1, :] * scp[s, pl.ds(base2 + j, tt), :]
        ysc_s[pl.ds(r0, tt), :] = acc2
        tail2 = scp[s, pl.ds(tt + base2, SC_CONV - 1), :]
        scc_out[s] = tail2
        scp[s, pl.ds(base2, SC_CONV - 1), :] = tail2

    _for_each(nb, conv_seq)

    hl = DN_HEADS * chunk
    shift = chunk.bit_length() - 1
    ri = lax.broadcasted_iota(jnp.int32, (hl, hl), 0)
    ci = lax.broadcasted_iota(jnp.int32, (hl, hl), 1)
    same_head = lax.shift_right_logical(ri, shift) == lax.shift_right_logical(ci, shift)
    causal = same_head & (ri >= ci)
    strict = same_head & (ri > ci)
    eye = (ri == ci).astype(F32)
    ri1 = lax.broadcasted_iota(jnp.int32, (chunk, chunk), 0)
    ci1 = lax.broadcasted_iota(jnp.int32, (chunk, chunk), 1)
    tril = (ri1 >= ci1).astype(F32)
    triu = (ri1 <= ci1).astype(F32)

    def stack_heads(ref, r0):
        return jnp.concatenate(
            [ref[r0:r0 + chunk, hd * DN_HEAD_DIM:(hd + 1) * DN_HEAD_DIM] for hd in range(DN_HEADS)], axis=0)

    for c in range(nb * n_chunk):
        r0 = c * chunk
        gb = gb_s[r0:r0 + chunk, :]
        cum = _dot(tril, gb)
        cum_t = _dot_tn(gb, triu)
        cc = jnp.concatenate([cum[:, hd:hd + 1] for hd in range(DN_HEADS)], axis=0)
        cr = jnp.concatenate([cum_t[hd:hd + 1, :] for hd in range(DN_HEADS)], axis=1)
        beta = jnp.concatenate([gb[:, DN_HEADS + hd:DN_HEADS + hd + 1] for hd in range(DN_HEADS)], axis=0)
        c_last = jnp.concatenate(
            [jnp.broadcast_to(cum[chunk - 1:chunk, hd:hd + 1], (chunk, 1)) for hd in range(DN_HEADS)], axis=0)
        qst, kst, vst = stack_heads(q_s, r0), stack_heads(k_s, r0), stack_heads(v_s, r0)
        decay = jnp.where(causal, jnp.exp(jnp.where(causal, cc - cr, 0.0)), 0.0)
        a_mat = jnp.where(strict, decay * _dot_nt(kst, kst), 0.0) * beta
        t_inv = _unit_lower_inverse(a_mat, eye, chunk)
        e_cum = jnp.exp(cc)
        w = _dot(t_inv, jnp.concatenate([beta * vst, (beta * e_cum) * kst], axis=1))
        wv_s[c] = w[:, :DN_HEAD_DIM]
        wk_s[c] = w[:, DN_HEAD_DIM:]
        qk_s[c] = _dot_nt(qst, kst) * decay
        qd_s[c] = e_cum * qst
        kd_s[c] = jnp.exp(c_last - cc) * kst
        cd_s[c] = jnp.broadcast_to(jnp.exp(c_last), (hl, DN_HEAD_DIM))

    for c in range(nb * n_chunk):
        s = c // n_chunk
        r0 = c * chunk
        states, us = [], []
        for hd in range(DN_HEADS):
            hr = slice(hd * chunk, (hd + 1) * chunk)
            state = dns_out[s, hd]
            states.append(state)
            us.append(wv_s[c, hr, :] - _dot(wk_s[c, hr, :], state))
        o_intra = _dot(qk_s[c], jnp.concatenate(us, axis=0))
        for hd in range(DN_HEADS):
            hr = slice(hd * chunk, (hd + 1) * chunk)
            lo = hd * DN_HEAD_DIM
            o_s[r0:r0 + chunk, lo:lo + DN_HEAD_DIM] = _dot(qd_s[c, hr, :], states[hd]) + o_intra[hr, :]
            dns_out[s, hd] = (cd_s[c, hd * chunk:hd * chunk + 1, :] * states[hd]
                              + _dot_tn(kd_s[c, hr, :], us[hd]))

    def attn_seq(s):
        r0 = pl.multiple_of(s * tt, tt)
        mq = mq_s[pl.ds(r0, tt), :]
        for hd in range(MEM_HEADS):
            lo = hd * MEM_HEAD_DIM
            qh = mq[:, lo:lo + MEM_HEAD_DIM].astype(BF16)
            kh = mk_ref[s, :, lo:lo + MEM_HEAD_DIM].astype(BF16)
            vh = mv_ref[s, :, lo:lo + MEM_HEAD_DIM].astype(BF16)
            sc_h = _dot_nt(qh, kh) * (MEM_HEAD_DIM ** -0.5)
            p = jnp.exp(sc_h - jnp.max(sc_h, axis=-1, keepdims=True))
            denom = jnp.sum(p, axis=-1, keepdims=True)
            ymem_s[pl.ds(r0, tt), lo:lo + MEM_HEAD_DIM] = _dot(p.astype(BF16), vh) / denom

    _for_each(nb, attn_seq)

    dn_gate = proj(OFF_DNG, DN_WIDTH)
    o_all = o_s[...]
    y_heads = []
    for hd in range(DN_HEADS):
        lo = hd * DN_HEAD_DIM
        oh = o_all[:, lo:lo + DN_HEAD_DIM]
        oh = oh * lax.rsqrt(jnp.mean(oh * oh, axis=-1, keepdims=True) + EPS) * dnw_ref[...]
        gh = dn_gate[:, lo:lo + DN_HEAD_DIM]
        y_heads.append(oh * (gh * _sigmoid(gh)))
    y_dn = jnp.concatenate(y_heads, axis=-1).astype(BF16)
    y_sc = (sc_b * ysc_s[...]).astype(BF16)
    y_mem = ymem_s[...].astype(BF16)

    merged = _sigmoid(proj(OFF_GATE, D_MODEL)) * _dot(y_dn, w_br[0:DN_WIDTH, :])
    merged = merged + _sigmoid(proj(OFF_GATE + D_MODEL, D_MODEL)) * _dot(y_sc, w_br[DN_WIDTH:DN_WIDTH + SC_WIDTH, :])
    merged = merged + _sigmoid(proj(OFF_GATE + 2 * D_MODEL, D_MODEL)) * _dot(y_mem, w_br[DN_WIDTH + SC_WIDTH:, :])
    h = x + _dot(merged.astype(BF16), w_o[...])
    h_ref[...] = h
    hn = h * lax.rsqrt(jnp.mean(h * h, axis=-1, keepdims=True) + EPS) * n2_ref[...]
    hn_ref[...] = hn
    lg_ref[...] = _dot(hn, wr_ref[...]) + br_ref[...]


def _mixer(x2d, dnc_in, dns_in, scc_in, mk, mv, weights, hn_all, *, n_seq, seq_len, nb, tt, row0):
    chunk = CHUNK if seq_len % CHUNK == 0 else seq_len
    rows = nb * tt
    n_t = seq_len // tt
    total_rows = n_seq * seq_len
    n_chunks = rows // chunk
    hl = DN_HEADS * chunk
    grid = (n_seq // nb, n_t)
    const = lambda b, t: (0, 0)
    seq3 = lambda b, t: (b, 0, 0)
    assert row0 % rows == 0

    def tok(b, t):
        return (b * n_t + t, 0)

    def tok_all(b, t):
        return (row0 // rows + b * n_t + t, 0)

    (w_in, w_br, w_o, n1, n2, dcw, scw, alog, dtb, dnw, wr, br) = weights
    in_specs = [
        pl.BlockSpec((rows, D_MODEL), tok),
        pl.BlockSpec((nb, DN_CONV - 1, QKV_WIDTH), seq3),
        pl.BlockSpec((nb, DN_HEADS, DN_HEAD_DIM, DN_HEAD_DIM), lambda b, t: (b, 0, 0, 0)),
        pl.BlockSpec((nb, SC_CONV - 1, SC_WIDTH), seq3),
        pl.BlockSpec((nb, MEM_LEN, MEM_WIDTH), seq3),
        pl.BlockSpec((nb, MEM_LEN, MEM_WIDTH), seq3),
        pl.BlockSpec(w_in.shape, const, pipeline_mode=pl.Buffered(1)),
        pl.BlockSpec(w_br.shape, const, pipeline_mode=pl.Buffered(1)),
        pl.BlockSpec(w_o.shape, const, pipeline_mode=pl.Buffered(1)),
        pl.BlockSpec(n1.shape, const),
        pl.BlockSpec(n2.shape, const),
        pl.BlockSpec(dcw.shape, const),
        pl.BlockSpec(scw.shape, const),
        pl.BlockSpec(alog.shape, const),
        pl.BlockSpec(dtb.shape, const),
        pl.BlockSpec(dnw.shape, const),
        pl.BlockSpec(wr.shape, const),
        pl.BlockSpec(br.shape, const),
        pl.BlockSpec(memory_space=pl.ANY),
    ]
    out_shape = [
        jax.ShapeDtypeStruct((total_rows, D_MODEL), F32),
        jax.ShapeDtypeStruct(hn_all.shape, F32),
        jax.ShapeDtypeStruct((total_rows, LANES), F32),
        jax.ShapeDtypeStruct((n_seq, DN_CONV - 1, QKV_WIDTH), F32),
        jax.ShapeDtypeStruct((n_seq, DN_HEADS, DN_HEAD_DIM, DN_HEAD_DIM), F32),
        jax.ShapeDtypeStruct((n_seq, SC_CONV - 1, SC_WIDTH), F32),
    ]
    out_specs = [
        pl.BlockSpec((rows, D_MODEL), tok),
        pl.BlockSpec((rows, D_MODEL), tok_all),
        pl.BlockSpec((rows, LANES), tok),
        pl.BlockSpec((nb, DN_CONV - 1, QKV_WIDTH), seq3),
        pl.BlockSpec((nb, DN_HEADS, DN_HEAD_DIM, DN_HEAD_DIM), lambda b, t: (b, 0, 0, 0)),
        pl.BlockSpec((nb, SC_CONV - 1, SC_WIDTH), seq3),
    ]
    args = [x2d, dnc_in, dns_in, scc_in, mk, mv, w_in, w_br, w_o, n1, n2, dcw, scw, alog, dtb, dnw, wr, br, hn_all]
    scratch = [
        pltpu.VMEM((nb, CONV_PAD + tt, QKV_WIDTH), F32),
        pltpu.VMEM((nb, CONV_PAD + tt, SC_WIDTH), F32),
        pltpu.VMEM((rows, DN_WIDTH), F32),
        pltpu.VMEM((rows, DN_WIDTH), F32),
        pltpu.VMEM((rows, DN_WIDTH), F32),
        pltpu.VMEM((rows, LANES), F32),
        pltpu.VMEM((rows, DN_WIDTH), F32),
        pltpu.VMEM((rows, MEM_WIDTH), F32),
        pltpu.VMEM((rows, SC_WIDTH), F32),
        pltpu.VMEM((rows, MEM_WIDTH), F32),
        pltpu.VMEM((n_chunks, hl, DN_HEAD_DIM), F32),
        pltpu.VMEM((n_chunks, hl, DN_HEAD_DIM), F32),
        pltpu.VMEM((n_chunks, hl, hl), F32),
        pltpu.VMEM((n_chunks, hl, DN_HEAD_DIM), F32),
        pltpu.VMEM((n_chunks, hl, DN_HEAD_DIM), F32),
        pltpu.VMEM((n_chunks, hl, DN_HEAD_DIM), F32),
    ]
    return pl.pallas_call(
        functools.partial(_mixer_kernel, nb=nb, tt=tt, chunk=chunk),
        grid=grid,
        in_specs=in_specs,
        out_specs=out_specs,
        out_shape=out_shape,
        scratch_shapes=scratch,
        input_output_aliases={len(args) - 1: 1},
        compiler_params=pltpu.CompilerParams(
            dimension_semantics=("arbitrary", "arbitrary"), vmem_limit_bytes=VMEM_LIMIT),
        name="mixer",
    )(*args)


def _router_kernel(lg_ref, idx_ref, gate_ref, rank_ref, cnt_ref, carry):
    i = pl.program_id(0)

    @pl.when(i == 0)
    def _():
        carry[...] = jnp.zeros_like(carry)

    work = lg_ref[...]
    tm = work.shape[0]
    lane = lax.broadcasted_iota(jnp.int32, (tm, LANES), 1).astype(F32)
    idxs, vals = [], []
    for _ in range(TOP_K):
        m = jnp.max(work, axis=-1, keepdims=True)
        ik = jnp.min(jnp.where(work == m, lane, float(LANES)), axis=-1, keepdims=True)
        idxs.append(ik)
        vals.append(m)
        work = jnp.where(lane == ik, -jnp.inf, work)
    exps = [jnp.exp(v - vals[0]) for v in vals]
    denom = exps[0] + exps[1] + exps[2] + exps[3]
    hot = jnp.zeros((tm, LANES), F32)
    for ik in idxs:
        hot = hot + (lane == ik).astype(F32)
    ri = lax.broadcasted_iota(jnp.int32, (tm, tm), 0)
    ci = lax.broadcasted_iota(jnp.int32, (tm, tm), 1)
    before = (ri > ci).astype(BF16)
    prefix = _dot(before, hot.astype(BF16)) + carry[...]
    idx_out = jnp.zeros((tm, LANES), F32)
    gate_out = jnp.zeros((tm, LANES), F32)
    rank_out = jnp.zeros((tm, LANES), F32)
    for k in range(TOP_K):
        rk = jnp.sum(jnp.where(lane == idxs[k], prefix, 0.0), axis=-1, keepdims=True)
        idx_out = jnp.where(lane == k, idxs[k], idx_out)
        gate_out = jnp.where(lane == k, exps[k] / denom, gate_out)
        rank_out = jnp.where(lane == k, rk, rank_out)
    idx_ref[...] = idx_out.astype(jnp.int32)
    gate_ref[...] = gate_out
    rank_ref[...] = rank_out.astype(jnp.int32)
    carry[...] = carry[...] + jnp.sum(hot, axis=0, keepdims=True)
    cnt_ref[...] = carry[...].astype(jnp.int32)


def _router(logits):
    n_tok = logits.shape[0]
    tile = lambda i: (i, 0)
    return pl.pallas_call(
        _router_kernel,
        grid=(n_tok // TOK_TILE,),
        in_specs=[pl.BlockSpec((TOK_TILE, LANES), tile)],
        out_specs=[pl.BlockSpec((TOK_TILE, LANES), tile)] * 3 + [pl.BlockSpec((1, LANES), lambda i: (0, 0))],
        out_shape=[
            jax.ShapeDtypeStruct((n_tok, LANES), jnp.int32),
            jax.ShapeDtypeStruct((n_tok, LANES), F32),
            jax.ShapeDtypeStruct((n_tok, LANES), jnp.int32),
            jax.ShapeDtypeStruct((1, LANES), jnp.int32),
        ],
        scratch_shapes=[pltpu.VMEM((1, LANES), F32)],
        compiler_params=pltpu.CompilerParams(dimension_semantics=("arbitrary",)),
        name="router",
    )(logits)


def _invert_kernel(dest_ref, code_ref, default_ref, inv_ref, sem):
    i = pl.program_id(0)

    @pl.when(i == 0)
    def _():
        cp = pltpu.make_async_copy(default_ref, inv_ref, sem)
        cp.start()
        cp.wait()

    def body(j, c):
        for d in range(INVERT_UNROLL):
            q = j * INVERT_UNROLL + d
            inv_ref[dest_ref[q]] = code_ref[q]
        return c

    lax.fori_loop(0, TOK_TILE * TOP_K // INVERT_UNROLL, body, 0)


def _invert(dest_flat, pair_code, default_inv):
    n_pairs = TOK_TILE * TOP_K
    return pl.pallas_call(
        _invert_kernel,
        grid=(dest_flat.shape[0] // n_pairs,),
        in_specs=[
            pl.BlockSpec((n_pairs,), lambda i: (i,), memory_space=pltpu.SMEM),
            pl.BlockSpec((n_pairs,), lambda i: (i,), memory_space=pltpu.SMEM),
            pl.BlockSpec(memory_space=pl.ANY),
        ],
        out_specs=pl.BlockSpec(memory_space=pltpu.SMEM),
        out_shape=jax.ShapeDtypeStruct(default_inv.shape, jnp.int32),
        scratch_shapes=[pltpu.SemaphoreType.DMA(())],
        compiler_params=pltpu.CompilerParams(dimension_semantics=("arbitrary",)),
        name="invert",
    )(dest_flat, pair_code, default_inv)


def _experts_kernel(blk_e, n_valid, inv, hn_ref, wg_ref, bg_ref, wu_ref, bu_ref, wd_ref, bd_ref, yp_ref,
                    wg_bf, wu_bf, wd_bf, xbuf0, xbuf1, ybuf0, ybuf1, zeros, gsem, ssem, zsem, *, n_tok):
    i = pl.program_id(0)
    nv = n_valid[0]
    parity = lax.rem(i, 2)
    xbuf = (xbuf0, xbuf1)
    ybuf = (ybuf0, ybuf1)
    plane_rows = n_tok + TRASH_TOK

    def gather(block, s):
        base = block * ROW_BLOCK
        for r in range(ROW_BLOCK):
            tok = inv[base + r] & PAIR_TOK_MASK
            pltpu.make_async_copy(
                hn_ref.at[pl.ds(tok, 1)], xbuf[s].at[pl.ds(r, 1)], gsem.at[s]).start(priority=r % 2)

    def scatter(block, s):
        base = block * ROW_BLOCK
        for r in range(ROW_BLOCK):
            pair = inv[base + r]
            row = lax.shift_right_logical(pair, PAIR_TOK_BITS) * plane_rows + (pair & PAIR_TOK_MASK)
            pltpu.make_async_copy(
                ybuf[s].at[pl.ds(r, 1)], yp_ref.at[pl.ds(row, 1)], ssem.at[s]).start(priority=(r + 1) % 2)

    def wait_gather(s):
        pltpu.make_async_copy(hn_ref.at[pl.ds(0, ROW_BLOCK)], xbuf[s], gsem.at[s]).wait()

    def wait_scatter(s):
        pltpu.make_async_copy(ybuf[s], yp_ref.at[pl.ds(0, ROW_BLOCK)], ssem.at[s]).wait()

    @pl.when(i == 0)
    def _():
        zeros[...] = jnp.zeros_like(zeros)
        for k in range(TOP_K):
            cp = pltpu.make_async_copy(zeros, yp_ref.at[pl.ds(k * plane_rows + n_tok, TRASH_TOK)], zsem)
            cp.start()
            cp.wait()
        gather(0, 0)

    e = blk_e[i]
    prev = blk_e[jnp.maximum(i - 1, 0)]

    @pl.when(((i == 0) | (e != prev)) & (i < nv))
    def _():
        wg_bf[...] = wg_ref[0].astype(BF16)
        wu_bf[...] = wu_ref[0].astype(BF16)
        wd_bf[...] = wd_ref[0].astype(BF16)

    def step(slot, scatter_previous):
        other = 1 - slot
        wait_gather(slot)
        gather(jnp.minimum(i + 1, nv - 1), other)
        if scatter_previous:
            scatter(i - 1, other)
        x = xbuf[slot][...].astype(BF16)
        gl = jnp.minimum(_dot(x, wg_bf[...]) + bg_ref[0], SWIGLU_LIMIT)
        ul = jnp.clip(_dot(x, wu_bf[...]) + bu_ref[0], -SWIGLU_LIMIT, SWIGLU_LIMIT)
        act = (ul + 1.0) * (gl * _sigmoid(SWIGLU_ALPHA * gl))
        ybuf[slot][...] = _dot(act.astype(BF16), wd_bf[...]) + bd_ref[0]

    def finish(slot):
        other = 1 - slot
        scatter(i, slot)
        wait_gather(other)

        @pl.when(i >= 1)
        def _():
            wait_scatter(other)

        wait_scatter(slot)

    @pl.when((i == 0) & (i < nv))
    def _():
        step(0, False)

    for slot in range(2):
        mine = (parity == slot) & (i < nv)

        @pl.when(mine & (i >= 2))
        def _():
            wait_scatter(slot)

        @pl.when(mine & (i > 0))
        def _():
            step(slot, True)

        @pl.when(mine & (i == nv - 1))
        def _():
            finish(slot)


def _experts(blk_e, n_valid, inv, hn_all, w_gate, b_gate, w_up, b_up, w_down, b_down, *, n_tok):
    n_blocks = blk_e.shape[0]
    wsel = lambda i, be, nv, iv: (be[i], 0, 0)
    d_ff = w_gate.shape[-1]
    grid_spec = pltpu.PrefetchScalarGridSpec(
        num_scalar_prefetch=3,
        grid=(n_blocks,),
        in_specs=[
            pl.BlockSpec(memory_space=pl.ANY),
            pl.BlockSpec((1, D_MODEL, d_ff), wsel),
            pl.BlockSpec((1, 1, d_ff), wsel),
            pl.BlockSpec((1, D_MODEL, d_ff), wsel),
            pl.BlockSpec((1, 1, d_ff), wsel),
            pl.BlockSpec((1, d_ff, D_MODEL), wsel),
            pl.BlockSpec((1, 1, D_MODEL), wsel),
        ],
        out_specs=pl.BlockSpec(memory_space=pl.ANY),
        scratch_shapes=[
            pltpu.VMEM((D_MODEL, d_ff), BF16),
            pltpu.VMEM((D_MODEL, d_ff), BF16),
            pltpu.VMEM((d_ff, D_MODEL), BF16),
            pltpu.VMEM((ROW_BLOCK, D_MODEL), F32),
            pltpu.VMEM((ROW_BLOCK, D_MODEL), F32),
            pltpu.VMEM((ROW_BLOCK, D_MODEL), F32),
            pltpu.VMEM((ROW_BLOCK, D_MODEL), F32),
            pltpu.VMEM((TRASH_TOK, D_MODEL), F32),
            pltpu.SemaphoreType.DMA((2,)),
            pltpu.SemaphoreType.DMA((2,)),
            pltpu.SemaphoreType.DMA(()),
        ],
    )
    return pl.pallas_call(
        functools.partial(_experts_kernel, n_tok=n_tok),
        grid_spec=grid_spec,
        out_shape=jax.ShapeDtypeStruct((TOP_K * (n_tok + TRASH_TOK), D_MODEL), F32),
        compiler_params=pltpu.CompilerParams(
            dimension_semantics=("arbitrary",), vmem_limit_bytes=VMEM_LIMIT),
        name="experts",
    )(blk_e, n_valid, inv, hn_all, w_gate, b_gate, w_up, b_up, w_down, b_down)


def _combine_kernel(gate_ref, h_ref, fw_ref, *refs):
    y_ref = refs[TOP_K]
    gate = gate_ref[...]
    acc = h_ref[...]
    for k in range(TOP_K):
        acc = acc + gate[:, k:k + 1] * refs[k][...]
    y_ref[...] = acc * lax.rsqrt(jnp.mean(acc * acc, axis=-1, keepdims=True) + EPS) * fw_ref[...]


def _combine(gates, h_group, final_w, y_pairs, *, tile0, plane_rows):
    n_tiles = h_group.shape[0] // TOK_TILE
    plane_tiles = plane_rows // TOK_TILE

    def plane(k):
        return pl.BlockSpec((TOK_TILE, D_MODEL), lambda i: (k * plane_tiles + tile0 + i, 0))

    return pl.pallas_call(
        _combine_kernel,
        grid=(n_tiles,),
        in_specs=[
            pl.BlockSpec((TOK_TILE, LANES), lambda i: (tile0 + i, 0)),
            pl.BlockSpec((TOK_TILE, D_MODEL), lambda i: (i, 0)),
            pl.BlockSpec((1, D_MODEL), lambda i: (0, 0)),
        ] + [plane(k) for k in range(TOP_K)],
        out_specs=pl.BlockSpec((TOK_TILE, D_MODEL), lambda i: (i, 0)),
        out_shape=jax.ShapeDtypeStruct((n_tiles * TOK_TILE, D_MODEL), F32),
        compiler_params=pltpu.CompilerParams(dimension_semantics=("arbitrary",)),
        name="combine",
    )(gates, h_group, final_w, *([y_pairs] * TOP_K))


def _permute_w_in(w):
    o_a = QKV_WIDTH
    o_g = o_a + 2 * DN_HEADS
    o_rest = o_g + DN_WIDTH
    pad = jnp.zeros((w.shape[0], LANES - 2 * DN_HEADS), w.dtype)
    return jnp.concatenate([w[:, :o_a], w[:, o_g:o_rest], w[:, o_rest:], w[:, o_a:o_g], pad], axis=1)


def _lane_row(v, fill=0.0):
    return jnp.concatenate([v.astype(F32), jnp.full((LANES - v.shape[0],), fill, F32)]).reshape(1, LANES)


def kernel(x_prompt, x_sample, mem_prompt, state_dn, state_dn_conv, state_sc_conv, cache_mem_k, cache_mem_v, w_in, dn_conv_w, dn_A_log, dn_dt_bias, dn_norm_w, sc_conv_w, mem_norm_w, w_mem_kv, w_br, w_o, norm1_w, norm2_w, w_router, b_router, w_gate, b_gate, w_up, b_up, w_down, b_down, final_norm_w):
    assert w_in.shape[0] == 1, "one layer"
    bp, tp, _ = x_prompt.shape
    bs, ts, _ = x_sample.shape
    n_p, n_s = bp * tp, bs * ts
    n_tok = n_p + n_s
    assert n_p % TOK_TILE == 0 and n_s % TOK_TILE == 0 and tp % MIX_TILE == 0

    weights = (
        _permute_w_in(w_in[0]).astype(BF16),
        w_br[0].astype(BF16),
        w_o[0].astype(BF16),
        norm1_w[0].reshape(1, D_MODEL),
        norm2_w[0].reshape(1, D_MODEL),
        dn_conv_w[0],
        sc_conv_w[0],
        _lane_row(dn_A_log[0]),
        _lane_row(dn_dt_bias[0]),
        dn_norm_w[0].reshape(1, DN_HEAD_DIM),
        jnp.concatenate([w_router[0], jnp.zeros((D_MODEL, LANES - N_EXPERTS), F32)], axis=1),
        _lane_row(b_router[0], NEG_BIG),
    )

    mk2d, mv2d = _memkv(mem_prompt.reshape(bp * MEM_LEN, D_MODEL), mem_norm_w[0].reshape(1, D_MODEL),
                        w_mem_kv[0].astype(BF16))

    assert n_tok + TRASH_TOK <= PAIR_TOK_MASK + 1 and (n_tok + TRASH_TOK) % MIX_TILE == 0
    hn_all = jnp.zeros((n_tok + TRASH_TOK, D_MODEL), F32)
    h_s, hn_all, lg_s, s_dnc, s_dns, s_scc = _mixer(
        x_sample.reshape(n_s, D_MODEL), state_dn_conv[0], state_dn[0], state_sc_conv[0],
        cache_mem_k[0].reshape(bs, MEM_LEN, MEM_WIDTH), cache_mem_v[0].reshape(bs, MEM_LEN, MEM_WIDTH),
        weights, hn_all, n_seq=bs, seq_len=ts, nb=bs, tt=ts, row0=n_p)
    h_p, hn_all, lg_p, p_dnc, p_dns, p_scc = _mixer(
        x_prompt.reshape(n_p, D_MODEL),
        jnp.zeros((bp, DN_CONV - 1, QKV_WIDTH), F32),
        jnp.zeros((bp, DN_HEADS, DN_HEAD_DIM, DN_HEAD_DIM), F32),
        jnp.zeros((bp, SC_CONV - 1, SC_WIDTH), F32),
        mk2d.reshape(bp, MEM_LEN, MEM_WIDTH), mv2d.reshape(bp, MEM_LEN, MEM_WIDTH),
        weights, hn_all, n_seq=bp, seq_len=tp, nb=1, tt=MIX_TILE, row0=0)

    idx, gates, rank, counts = _router(jnp.concatenate([lg_p, lg_s], axis=0))
    counts = counts[0, :N_EXPERTS]
    n_blk_e = (counts + ROW_BLOCK - 1) // ROW_BLOCK
    blk_end = jnp.cumsum(n_blk_e)
    blk_start = blk_end - n_blk_e
    row_start = blk_start * ROW_BLOCK
    experts = jnp.arange(N_EXPERTS, dtype=jnp.int32)
    start_of = jnp.sum(jnp.where(idx[:, :TOP_K, None] == experts, row_start, 0), axis=-1)
    dest = (start_of + rank[:, :TOP_K]).reshape(n_tok * TOP_K)

    n_blocks = (n_tok * TOP_K) // ROW_BLOCK + N_EXPERTS
    bi = jnp.arange(n_blocks, dtype=jnp.int32)
    n_valid = blk_end[-1].astype(jnp.int32)
    bclip = jnp.minimum(bi, n_valid - 1)
    blk_e = jnp.sum((bclip[:, None] >= blk_end[None, :]).astype(jnp.int32), axis=1)
    n_valid = n_valid.reshape(1)

    slots = jnp.arange(n_blocks * ROW_BLOCK, dtype=jnp.int32)
    default_inv = n_tok + (slots & (TRASH_TOK - 1))
    pairs = jnp.arange(n_tok * TOP_K, dtype=jnp.int32)
    pair_code = ((pairs % TOP_K) << PAIR_TOK_BITS) | (pairs // TOP_K)
    inv = _invert(dest.astype(jnp.int32), pair_code, default_inv)

    y_pairs = _experts(blk_e, n_valid, inv, hn_all,
                       w_gate[0], b_gate[0].reshape(N_EXPERTS, 1, -1), w_up[0], b_up[0].reshape(N_EXPERTS, 1, -1),
                       w_down[0], b_down[0].reshape(N_EXPERTS, 1, -1), n_tok=n_tok)
    fw = final_norm_w.reshape(1, D_MODEL)
    y_p = _combine(gates, h_p, fw, y_pairs, tile0=0, plane_rows=n_tok + TRASH_TOK)
    y_s = _combine(gates, h_s, fw, y_pairs, tile0=n_p // TOK_TILE, plane_rows=n_tok + TRASH_TOK)

    return (y_p.reshape(bp, tp, D_MODEL), y_s.reshape(bs, ts, D_MODEL),
            p_dns[None], p_dnc[None], p_scc[None],
            mk2d.reshape(1, bp, MEM_LEN, MEM_HEADS, MEM_HEAD_DIM), mv2d.reshape(1, bp, MEM_LEN, MEM_HEADS, MEM_HEAD_DIM),
            s_dns[None], s_dnc[None], s_scc[None])
```

```python
import functools

import jax
import jax.numpy as jnp
from jax import lax
from jax.experimental import pallas as pl
from jax.experimental.pallas import tpu as pltpu

F32 = jnp.float32
BF16 = jnp.bfloat16

D_MODEL = 1024
CHUNK = 64
EPS = 1e-6
DN_HEADS = 4
DN_HEAD_DIM = 128
DN_WIDTH = DN_HEADS * DN_HEAD_DIM
QKV_WIDTH = 3 * DN_WIDTH
DN_CONV = 4
SC_WIDTH = 256
SC_CONV = 3
MEM_LEN = 256
MEM_HEADS = 4
MEM_HEAD_DIM = 64
MEM_WIDTH = MEM_HEADS * MEM_HEAD_DIM
N_EXPERTS = 32
TOP_K = 4
SWIGLU_ALPHA = 1.702
SWIGLU_LIMIT = 7.0

LANES = 128
SUBLANES = 8
ROW_TILES = D_MODEL // LANES
assert ROW_TILES == SUBLANES
CONV_PAD = 8

OFF_QKV = 0
OFF_DNG = OFF_QKV + QKV_WIDTH
OFF_SC = OFF_DNG + DN_WIDTH
OFF_MQ = OFF_SC + 3 * SC_WIDTH
OFF_GATE = OFF_MQ + MEM_WIDTH
OFF_AB = OFF_GATE + 3 * D_MODEL
IN_PERM_WIDTH = OFF_AB + LANES

MIX_TILE = 512
TOK_TILE = 256
ROW_BLOCK = 256
INVERT_UNROLL = 8
TRASH_TOK = 256
RING = 3
PAIR_TOK_BITS = 16
PAIR_TOK_MASK = (1 << PAIR_TOK_BITS) - 1
VMEM_LIMIT = 56 * 1024 * 1024
NEG_BIG = -1e30


def _dot(a, b):
    return jnp.dot(a, b, preferred_element_type=F32)


def _dot_nt(a, b):
    return lax.dot_general(a, b, (((1,), (1,)), ((), ())), preferred_element_type=F32)


def _dot_tn(a, b):
    return lax.dot_general(a, b, (((0,), (0,)), ((), ())), preferred_element_type=F32)


def _sigmoid(x):
    return 1.0 / (1.0 + jnp.exp(-x))


def _softplus(x):
    return jnp.maximum(x, 0.0) + jnp.log1p(jnp.exp(-jnp.abs(x)))


def _for_each(n, body):
    if n == 1:
        body(0)
    else:
        def step(i, carry):
            body(i)
            return carry
        lax.fori_loop(0, n, step, 0)


def _memkv_kernel(mem_ref, nw_ref, w_ref, k_ref, v_ref):
    x = mem_ref[...]
    xn = x * lax.rsqrt(jnp.mean(x * x, axis=-1, keepdims=True) + EPS) * nw_ref[...]
    kv = _dot(xn.astype(BF16), w_ref[...])
    k_ref[...] = kv[:, :MEM_WIDTH]
    v_ref[...] = kv[:, MEM_WIDTH:]


def _memkv(mem2d, norm_w, w_kv_bf16):
    rows = mem2d.shape[0]
    grid = rows // MEM_LEN
    return pl.pallas_call(
        _memkv_kernel,
        grid=(grid,),
        in_specs=[
            pl.BlockSpec((MEM_LEN, D_MODEL), lambda i: (i, 0)),
            pl.BlockSpec((1, D_MODEL), lambda i: (0, 0)),
            pl.BlockSpec((D_MODEL, 2 * MEM_WIDTH), lambda i: (0, 0)),
        ],
        out_specs=[
            pl.BlockSpec((MEM_LEN, MEM_WIDTH), lambda i: (i, 0)),
            pl.BlockSpec((MEM_LEN, MEM_WIDTH), lambda i: (i, 0)),
        ],
        out_shape=[jax.ShapeDtypeStruct((rows, MEM_WIDTH), F32)] * 2,
        name="memkv",
    )(mem2d, norm_w, w_kv_bf16)


def _unit_lower_inverse(a, eye, size):
    inv = eye - a
    power = a
    span = 2
    while span < size:
        power = _dot(power, power)
        inv = _dot(inv, eye + power)
        span *= 2
    return inv


def _mixer_kernel(x_ref, dnc_in, dns_in, scc_in, mk_ref, mv_ref, w_in, w_br, w_o, n1_ref, n2_ref, dcw_ref, scw_ref,
                  alog_ref, dtb_ref, dnw_ref, wr_ref, br_ref, hn_all_ref,
                  h_ref, hn_ref, lg_ref, dnc_out, dns_out, scc_out,
                  xp, scp, q_s, k_s, v_s, gb_s, o_s, mq_s, ysc_s, ymem_s, wv_s, wk_s, qk_s, qd_s, kd_s, cd_s,
                  *, nb, tt, chunk):
    t_idx = pl.program_id(1)
    rows = nb * tt
    n_chunk = tt // chunk

    @pl.when(t_idx == 0)
    def _():
        xp[:, CONV_PAD - (DN_CONV - 1):CONV_PAD, :] = dnc_in[...]
        scp[:, CONV_PAD - (SC_CONV - 1):CONV_PAD, :] = scc_in[...]
        dns_out[...] = dns_in[...]

    x = x_ref[...]
    xn = (x * lax.rsqrt(jnp.mean(x * x, axis=-1, keepdims=True) + EPS) * n1_ref[...]).astype(BF16)

    def proj(off, width):
        return _dot(xn, w_in[:, off:off + width])

    qkv_pre = proj(OFF_QKV, QKV_WIDTH)
    for s in range(nb):
        xp[s, CONV_PAD:CONV_PAD + tt, :] = qkv_pre[s * tt:(s + 1) * tt, :]
    sc = proj(OFF_SC, 3 * SC_WIDTH)
    sc_b = sc[:, :SC_WIDTH]
    sc_ch = sc[:, SC_WIDTH:2 * SC_WIDTH] * sc[:, 2 * SC_WIDTH:]
    for s in range(nb):
        scp[s, CONV_PAD:CONV_PAD + tt, :] = sc_ch[s * tt:(s + 1) * tt, :]
    mq_s[...] = proj(OFF_MQ, MEM_WIDTH)

    ab = proj(OFF_AB, LANES)
    lane = lax.broadcasted_iota(jnp.int32, (rows, LANES), 1)
    g_log = -jnp.exp(alog_ref[...]) * _softplus(ab + dtb_ref[...])
    gb_s[...] = jnp.where(lane < DN_HEADS, g_log, _sigmoid(ab))

    def conv_seq(s):
        base = CONV_PAD - (DN_CONV - 1)
        acc = dcw_ref[0:1, :] * xp[s, pl.ds(base, tt), :]
        for j in range(1, DN_CONV):
            acc = acc + dcw_ref[j:j + 1, :] * xp[s, pl.ds(base + j, tt), :]
        act = acc * _sigmoid(acc)
        r0 = pl.multiple_of(s * tt, tt)
        for hd in range(DN_HEADS):
            lo = hd * DN_HEAD_DIM
            qh = act[:, lo:lo + DN_HEAD_DIM]
            kh = act[:, DN_WIDTH + lo:DN_WIDTH + lo + DN_HEAD_DIM]
            q_s[pl.ds(r0, tt), lo:lo + DN_HEAD_DIM] = (
                qh * lax.rsqrt(jnp.sum(qh * qh, axis=-1, keepdims=True) + EPS) * (DN_HEAD_DIM ** -0.5))
            k_s[pl.ds(r0, tt), lo:lo + DN_HEAD_DIM] = (
                kh * lax.rsqrt(jnp.sum(kh * kh, axis=-1, keepdims=True) + EPS))
        v_s[pl.ds(r0, tt), :] = act[:, 2 * DN_WIDTH:]
        tail = xp[s, pl.ds(tt + base, DN_CONV - 1), :]
        dnc_out[s] = tail
        xp[s, pl.ds(base, DN_CONV - 1), :] = tail

        base2 = CONV_PAD - (SC_CONV - 1)
        acc2 = scw_ref[0:1, :] * scp[s, pl.ds(base2, tt), :]
        for j in range(1, SC_CONV):
            acc2 = acc2 + scw_ref[j:j + 1, :] * scp[s, pl.ds(base2 + j, tt), :]
        ysc_s[pl.ds(r0, tt), :] = acc2
        tail2 = scp[s, pl.ds(tt + base2, SC_CONV - 1), :]
        scc_out[s] = tail2
        scp[s, pl.ds(base2, SC_CONV - 1), :] = tail2

    _for_each(nb, conv_seq)

    hl = DN_HEADS * chunk
    shift = chunk.bit_length() - 1
    ri = lax.broadcasted_iota(jnp.int32, (hl, hl), 0)
    ci = lax.broadcasted_iota(jnp.int32, (hl, hl), 1)
    same_head = lax.shift_right_logical(ri, shift) == lax.shift_right_logical(ci, shift)
    causal = same_head & (ri >= ci)
    strict = same_head & (ri > ci)
    eye = (ri == ci).astype(F32)
    ri1 = lax.broadcasted_iota(jnp.int32, (chunk, chunk), 0)
    ci1 = lax.broadcasted_iota(jnp.int32, (chunk, chunk), 1)
    tril = (ri1 >= ci1).astype(F32)
    triu = (ri1 <= ci1).astype(F32)

    def stack_heads(ref, r0):
        return jnp.concatenate(
            [ref[r0:r0 + chunk, hd * DN_HEAD_DIM:(hd + 1) * DN_HEAD_DIM] for hd in range(DN_HEADS)], axis=0)

    for c in range(nb * n_chunk):
        r0 = c * chunk
        gb = gb_s[r0:r0 + chunk, :]
        cum = _dot(tril, gb)
        cum_t = _dot_tn(gb, triu)
        cc = jnp.concatenate([cum[:, hd:hd + 1] for hd in range(DN_HEADS)], axis=0)
        cr = jnp.concatenate([cum_t[hd:hd + 1, :] for hd in range(DN_HEADS)], axis=1)
        beta = jnp.concatenate([gb[:, DN_HEADS + hd:DN_HEADS + hd + 1] for hd in range(DN_HEADS)], axis=0)
        c_last = jnp.concatenate(
            [jnp.broadcast_to(cum[chunk - 1:chunk, hd:hd + 1], (chunk, 1)) for hd in range(DN_HEADS)], axis=0)
        qst, kst, vst = stack_heads(q_s, r0), stack_heads(k_s, r0), stack_heads(v_s, r0)
        decay = jnp.where(causal, jnp.exp(jnp.where(causal, cc - cr, 0.0)), 0.0)
        a_mat = jnp.where(strict, decay * _dot_nt(kst, kst), 0.0) * beta
        t_inv = _unit_lower_inverse(a_mat, eye, chunk)
        e_cum = jnp.exp(cc)
        w = _dot(t_inv, jnp.concatenate([beta * vst, (beta * e_cum) * kst], axis=1))
        wv_s[c] = w[:, :DN_HEAD_DIM]
        wk_s[c] = w[:, DN_HEAD_DIM:]
        qk_s[c] = _dot_nt(qst, kst) * decay
        qd_s[c] = e_cum * qst
        kd_s[c] = jnp.exp(c_last - cc) * kst
        cd_s[c] = jnp.broadcast_to(jnp.exp(c_last), (hl, DN_HEAD_DIM))

    for c in range(nb * n_chunk):
        s = c // n_chunk
        r0 = c * chunk
        states, us = [], []
        for hd in range(DN_HEADS):
            hr = slice(hd * chunk, (hd + 1) * chunk)
            state = dns_out[s, hd]
            states.append(state)
            us.append(wv_s[c, hr, :] - _dot(wk_s[c, hr, :], state))
        o_intra = _dot(qk_s[c], jnp.concatenate(us, axis=0))
        for hd in range(DN_HEADS):
            hr = slice(hd * chunk, (hd + 1) * chunk)
            lo = hd * DN_HEAD_DIM
            o_s[r0:r0 + chunk, lo:lo + DN_HEAD_DIM] = _dot(qd_s[c, hr, :], states[hd]) + o_intra[hr, :]
            dns_out[s, hd] = (cd_s[c, hd * chunk:hd * chunk + 1, :] * states[hd]
                              + _dot_tn(kd_s[c, hr, :], us[hd]))

    def attn_seq(s):
        r0 = pl.multiple_of(s * tt, tt)
        mq = mq_s[pl.ds(r0, tt), :]
        for hd in range(MEM_HEADS):
            lo = hd * MEM_HEAD_DIM
            qh = mq[:, lo:lo + MEM_HEAD_DIM].astype(BF16)
            kh = mk_ref[s, :, lo:lo + MEM_HEAD_DIM].astype(BF16)
            vh = mv_ref[s, :, lo:lo + MEM_HEAD_DIM].astype(BF16)
            sc_h = _dot_nt(qh, kh) * (MEM_HEAD_DIM ** -0.5)
            p = jnp.exp(sc_h - jnp.max(sc_h, axis=-1, keepdims=True))
            denom = jnp.sum(p, axis=-1, keepdims=True)
            ymem_s[pl.ds(r0, tt), lo:lo + MEM_HEAD_DIM] = _dot(p.astype(BF16), vh) / denom

    _for_each(nb, attn_seq)

    dn_gate = proj(OFF_DNG, DN_WIDTH)
    o_all = o_s[...]
    y_heads = []
    for hd in range(DN_HEADS):
        lo = hd * DN_HEAD_DIM
        oh = o_all[:, lo:lo + DN_HEAD_DIM]
        oh = oh * lax.rsqrt(jnp.mean(oh * oh, axis=-1, keepdims=True) + EPS) * dnw_ref[...]
        gh = dn_gate[:, lo:lo + DN_HEAD_DIM]
        y_heads.append(oh * (gh * _sigmoid(gh)))
    y_dn = jnp.concatenate(y_heads, axis=-1).astype(BF16)
    y_sc = (sc_b * ysc_s[...]).astype(BF16)
    y_mem = ymem_s[...].astype(BF16)

    merged = _sigmoid(proj(OFF_GATE, D_MODEL)) * _dot(y_dn, w_br[0:DN_WIDTH, :])
    merged = merged + _sigmoid(proj(OFF_GATE + D_MODEL, D_MODEL)) * _dot(y_sc, w_br[DN_WIDTH:DN_WIDTH + SC_WIDTH, :])
    merged = merged + _sigmoid(proj(OFF_GATE + 2 * D_MODEL, D_MODEL)) * _dot(y_mem, w_br[DN_WIDTH + SC_WIDTH:, :])
    h = x + _dot(merged.astype(BF16), w_o[...])
    h_ref[...] = h
    hn = h * lax.rsqrt(jnp.mean(h * h, axis=-1, keepdims=True) + EPS) * n2_ref[...]
    for c in range(ROW_TILES):
        hn_ref[pl.ds(c, rows, stride=ROW_TILES), :] = hn[:, c * LANES:(c + 1) * LANES]
    lg_ref[...] = _dot(hn, wr_ref[...]) + br_ref[...]


def _mixer(x2d, dnc_in, dns_in, scc_in, mk, mv, weights, hn_all, *, n_seq, seq_len, nb, tt, row0):
    chunk = CHUNK if seq_len % CHUNK == 0 else seq_len
    rows = nb * tt
    n_t = seq_len // tt
    total_rows = n_seq * seq_len
    n_chunks = rows // chunk
    hl = DN_HEADS * chunk
    grid = (n_seq // nb, n_t)
    const = lambda b, t: (0, 0)
    seq3 = lambda b, t: (b, 0, 0)
    assert row0 % rows == 0

    def tok(b, t):
        return (b * n_t + t, 0)

    def tok_all(b, t):
        return (row0 // rows + b * n_t + t, 0)

    (w_in, w_br, w_o, n1, n2, dcw, scw, alog, dtb, dnw, wr, br) = weights
    in_specs = [
        pl.BlockSpec((rows, D_MODEL), tok),
        pl.BlockSpec((nb, DN_CONV - 1, QKV_WIDTH), seq3),
        pl.BlockSpec((nb, DN_HEADS, DN_HEAD_DIM, DN_HEAD_DIM), lambda b, t: (b, 0, 0, 0)),
        pl.BlockSpec((nb, SC_CONV - 1, SC_WIDTH), seq3),
        pl.BlockSpec((nb, MEM_LEN, MEM_WIDTH), seq3),
        pl.BlockSpec((nb, MEM_LEN, MEM_WIDTH), seq3),
        pl.BlockSpec(w_in.shape, const, pipeline_mode=pl.Buffered(1)),
        pl.BlockSpec(w_br.shape, const, pipeline_mode=pl.Buffered(1)),
        pl.BlockSpec(w_o.shape, const, pipeline_mode=pl.Buffered(1)),
        pl.BlockSpec(n1.shape, const),
        pl.BlockSpec(n2.shape, const),
        pl.BlockSpec(dcw.shape, const),
        pl.BlockSpec(scw.shape, const),
        pl.BlockSpec(alog.shape, const),
        pl.BlockSpec(dtb.shape, const),
        pl.BlockSpec(dnw.shape, const),
        pl.BlockSpec(wr.shape, const),
        pl.BlockSpec(br.shape, const),
        pl.BlockSpec(memory_space=pl.ANY),
    ]
    out_shape = [
        jax.ShapeDtypeStruct((total_rows, D_MODEL), F32),
        jax.ShapeDtypeStruct(hn_all.shape, F32),
        jax.ShapeDtypeStruct((total_rows, LANES), F32),
        jax.ShapeDtypeStruct((n_seq, DN_CONV - 1, QKV_WIDTH), F32),
        jax.ShapeDtypeStruct((n_seq, DN_HEADS, DN_HEAD_DIM, DN_HEAD_DIM), F32),
        jax.ShapeDtypeStruct((n_seq, SC_CONV - 1, SC_WIDTH), F32),
    ]
    out_specs = [
        pl.BlockSpec((rows, D_MODEL), tok),
        pl.BlockSpec((rows * ROW_TILES, LANES), tok_all),
        pl.BlockSpec((rows, LANES), tok),
        pl.BlockSpec((nb, DN_CONV - 1, QKV_WIDTH), seq3),
        pl.BlockSpec((nb, DN_HEADS, DN_HEAD_DIM, DN_HEAD_DIM), lambda b, t: (b, 0, 0, 0)),
        pl.BlockSpec((nb, SC_CONV - 1, SC_WIDTH), seq3),
    ]
    args = [x2d, dnc_in, dns_in, scc_in, mk, mv, w_in, w_br, w_o, n1, n2, dcw, scw, alog, dtb, dnw, wr, br, hn_all]
    scratch = [
        pltpu.VMEM((nb, CONV_PAD + tt, QKV_WIDTH), F32),
        pltpu.VMEM((nb, CONV_PAD + tt, SC_WIDTH), F32),
        pltpu.VMEM((rows, DN_WIDTH), F32),
        pltpu.VMEM((rows, DN_WIDTH), F32),
        pltpu.VMEM((rows, DN_WIDTH), F32),
        pltpu.VMEM((rows, LANES), F32),
        pltpu.VMEM((rows, DN_WIDTH), F32),
        pltpu.VMEM((rows, MEM_WIDTH), F32),
        pltpu.VMEM((rows, SC_WIDTH), F32),
        pltpu.VMEM((rows, MEM_WIDTH), F32),
        pltpu.VMEM((n_chunks, hl, DN_HEAD_DIM), F32),
        pltpu.VMEM((n_chunks, hl, DN_HEAD_DIM), F32),
        pltpu.VMEM((n_chunks, hl, hl), F32),
        pltpu.VMEM((n_chunks, hl, DN_HEAD_DIM), F32),
        pltpu.VMEM((n_chunks, hl, DN_HEAD_DIM), F32),
        pltpu.VMEM((n_chunks, hl, DN_HEAD_DIM), F32),
    ]
    return pl.pallas_call(
        functools.partial(_mixer_kernel, nb=nb, tt=tt, chunk=chunk),
        grid=grid,
        in_specs=in_specs,
        out_specs=out_specs,
        out_shape=out_shape,
        scratch_shapes=scratch,
        input_output_aliases={len(args) - 1: 1},
        compiler_params=pltpu.CompilerParams(
            dimension_semantics=("arbitrary", "arbitrary"), vmem_limit_bytes=VMEM_LIMIT),
        name="mixer",
    )(*args)


def _router_kernel(lg_ref, idx_ref, gate_ref, rank_ref, cnt_ref, carry):
    i = pl.program_id(0)

    @pl.when(i == 0)
    def _():
        carry[...] = jnp.zeros_like(carry)

    work = lg_ref[...]
    tm = work.shape[0]
    lane = lax.broadcasted_iota(jnp.int32, (tm, LANES), 1).astype(F32)
    idxs, vals = [], []
    for _ in range(TOP_K):
        m = jnp.max(work, axis=-1, keepdims=True)
        ik = jnp.min(jnp.where(work == m, lane, float(LANES)), axis=-1, keepdims=True)
        idxs.append(ik)
        vals.append(m)
        work = jnp.where(lane == ik, -jnp.inf, work)
    exps = [jnp.exp(v - vals[0]) for v in vals]
    denom = exps[0] + exps[1] + exps[2] + exps[3]
    hot = jnp.zeros((tm, LANES), F32)
    for ik in idxs:
        hot = hot + (lane == ik).astype(F32)
    ri = lax.broadcasted_iota(jnp.int32, (tm, tm), 0)
    ci = lax.broadcasted_iota(jnp.int32, (tm, tm), 1)
    before = (ri > ci).astype(BF16)
    prefix = _dot(before, hot.astype(BF16)) + carry[...]
    idx_out = jnp.zeros((tm, LANES), F32)
    gate_out = jnp.zeros((tm, LANES), F32)
    rank_out = jnp.zeros((tm, LANES), F32)
    for k in range(TOP_K):
        rk = jnp.sum(jnp.where(lane == idxs[k], prefix, 0.0), axis=-1, keepdims=True)
        idx_out = jnp.where(lane == k, idxs[k], idx_out)
        gate_out = jnp.where(lane == k, exps[k] / denom, gate_out)
        rank_out = jnp.where(lane == k, rk, rank_out)
    idx_ref[...] = idx_out.astype(jnp.int32)
    gate_ref[...] = gate_out
    rank_ref[...] = rank_out.astype(jnp.int32)
    carry[...] = carry[...] + jnp.sum(hot, axis=0, keepdims=True)
    cnt_ref[...] = carry[...].astype(jnp.int32)


def _router(logits):
    n_tok = logits.shape[0]
    tile = lambda i: (i, 0)
    return pl.pallas_call(
        _router_kernel,
        grid=(n_tok // TOK_TILE,),
        in_specs=[pl.BlockSpec((TOK_TILE, LANES), tile)],
        out_specs=[pl.BlockSpec((TOK_TILE, LANES), tile)] * 3 + [pl.BlockSpec((1, LANES), lambda i: (0, 0))],
        out_shape=[
            jax.ShapeDtypeStruct((n_tok, LANES), jnp.int32),
            jax.ShapeDtypeStruct((n_tok, LANES), F32),
            jax.ShapeDtypeStruct((n_tok, LANES), jnp.int32),
            jax.ShapeDtypeStruct((1, LANES), jnp.int32),
        ],
        scratch_shapes=[pltpu.VMEM((1, LANES), F32)],
        compiler_params=pltpu.CompilerParams(dimension_semantics=("arbitrary",)),
        name="router",
    )(logits)


def _invert_kernel(dest_ref, code_ref, default_ref, inv_ref, sem):
    i = pl.program_id(0)

    @pl.when(i == 0)
    def _():
        cp = pltpu.make_async_copy(default_ref, inv_ref, sem)
        cp.start()
        cp.wait()

    def body(j, c):
        for d in range(INVERT_UNROLL):
            q = j * INVERT_UNROLL + d
            inv_ref[dest_ref[q]] = code_ref[q]
        return c

    lax.fori_loop(0, TOK_TILE * TOP_K // INVERT_UNROLL, body, 0)


def _invert(dest_flat, pair_code, default_inv):
    n_pairs = TOK_TILE * TOP_K
    return pl.pallas_call(
        _invert_kernel,
        grid=(dest_flat.shape[0] // n_pairs,),
        in_specs=[
            pl.BlockSpec((n_pairs,), lambda i: (i,), memory_space=pltpu.SMEM),
            pl.BlockSpec((n_pairs,), lambda i: (i,), memory_space=pltpu.SMEM),
            pl.BlockSpec(memory_space=pl.ANY),
        ],
        out_specs=pl.BlockSpec(memory_space=pltpu.SMEM),
        out_shape=jax.ShapeDtypeStruct(default_inv.shape, jnp.int32),
        scratch_shapes=[pltpu.SemaphoreType.DMA(())],
        compiler_params=pltpu.CompilerParams(dimension_semantics=("arbitrary",)),
        name="invert",
    )(dest_flat, pair_code, default_inv)


def _experts_kernel(blk_e, n_valid, inv, hn_ref, wg_ref, bg_ref, wu_ref, bu_ref, wd_ref, bd_ref, yp_ref,
                    wg_bf, wu_bf, wd_bf, xbuf0, xbuf1, xbuf2, ybuf0, ybuf1, ybuf2, zeros, gsem, ssem, zsem,
                    *, n_tok):
    i = pl.program_id(0)
    nv = n_valid[0]
    phase = lax.rem(i, RING)
    xbuf = (xbuf0, xbuf1, xbuf2)
    ybuf = (ybuf0, ybuf1, ybuf2)
    plane_rows = n_tok + TRASH_TOK

    def tile_rows(row, n=1):
        return pl.ds(pl.multiple_of(row * ROW_TILES, ROW_TILES), n * ROW_TILES)

    def gather(block, s):
        base = block * ROW_BLOCK
        for r in range(ROW_BLOCK):
            tok = inv[base + r] & PAIR_TOK_MASK
            pltpu.make_async_copy(
                hn_ref.at[tile_rows(tok)], xbuf[s].at[tile_rows(r)], gsem.at[s]).start(priority=r % 2)

    def scatter(block, s):
        base = block * ROW_BLOCK
        for r in range(ROW_BLOCK):
            pair = inv[base + r]
            row = lax.shift_right_logical(pair, PAIR_TOK_BITS) * plane_rows + (pair & PAIR_TOK_MASK)
            pltpu.make_async_copy(
                ybuf[s].at[tile_rows(r)], yp_ref.at[tile_rows(row)], ssem.at[s]).start(priority=(r + 1) % 2)

    def wait_gather(s):
        pltpu.make_async_copy(hn_ref.at[tile_rows(0, ROW_BLOCK)], xbuf[s], gsem.at[s]).wait()

    def wait_scatter(s):
        pltpu.make_async_copy(ybuf[s], yp_ref.at[tile_rows(0, ROW_BLOCK)], ssem.at[s]).wait()

    @pl.when(i == 0)
    def _():
        zeros[...] = jnp.zeros_like(zeros)
        for k in range(TOP_K):
            cp = pltpu.make_async_copy(zeros, yp_ref.at[tile_rows(k * plane_rows + n_tok, TRASH_TOK)], zsem)
            cp.start()
            cp.wait()
        gather(0, 0)
        gather(jnp.minimum(1, nv - 1), 1)

    e = blk_e[i]
    prev = blk_e[jnp.maximum(i - 1, 0)]

    @pl.when(((i == 0) | (e != prev)) & (i < nv))
    def _():
        wg_bf[...] = wg_ref[0].astype(BF16)
        wu_bf[...] = wu_ref[0].astype(BF16)
        wd_bf[...] = wd_ref[0].astype(BF16)

    def step(slot, scatter_previous):
        ahead, behind = (slot + 2) % RING, (slot - 1) % RING
        wait_gather(slot)
        gather(jnp.minimum(i + 2, nv - 1), ahead)
        if scatter_previous:
            scatter(i - 1, behind)
        x = jnp.concatenate(
            [xbuf[slot][pl.ds(c, ROW_BLOCK, stride=ROW_TILES), :] for c in range(ROW_TILES)], axis=1).astype(BF16)
        gl = jnp.minimum(_dot(x, wg_bf[...]) + bg_ref[0], SWIGLU_LIMIT)
        ul = jnp.clip(_dot(x, wu_bf[...]) + bu_ref[0], -SWIGLU_LIMIT, SWIGLU_LIMIT)
        act = (ul + 1.0) * (gl * _sigmoid(SWIGLU_ALPHA * gl))
        y = _dot(act.astype(BF16), wd_bf[...]) + bd_ref[0]
        for c in range(ROW_TILES):
            ybuf[slot][pl.ds(c, ROW_BLOCK, stride=ROW_TILES), :] = y[:, c * LANES:(c + 1) * LANES]

    def finish(slot):
        ahead, behind = (slot + 2) % RING, (slot - 1) % RING
        scatter(i, slot)
        wait_gather((slot + 1) % RING)
        wait_gather(ahead)

        @pl.when(i >= 2)
        def _():
            wait_scatter((slot - 2) % RING)

        @pl.when(i >= 1)
        def _():
            wait_scatter(behind)

        wait_scatter(slot)

    @pl.when((i == 0) & (i < nv))
    def _():
        step(0, False)

    for slot in range(RING):
        mine = (phase == slot) & (i < nv)

        @pl.when(mine & (i >= RING))
        def _():
            wait_scatter(slot)

        @pl.when(mine & (i > 0))
        def _():
            step(slot, True)

        @pl.when(mine & (i == nv - 1))
        def _():
            finish(slot)


def _experts(blk_e, n_valid, inv, hn_all, w_gate, b_gate, w_up, b_up, w_down, b_down, *, n_tok):
    n_blocks = blk_e.shape[0]
    wsel = lambda i, be, nv, iv: (be[i], 0, 0)
    d_ff = w_gate.shape[-1]
    grid_spec = pltpu.PrefetchScalarGridSpec(
        num_scalar_prefetch=3,
        grid=(n_blocks,),
        in_specs=[
            pl.BlockSpec(memory_space=pl.ANY),
            pl.BlockSpec((1, D_MODEL, d_ff), wsel),
            pl.BlockSpec((1, 1, d_ff), wsel),
            pl.BlockSpec((1, D_MODEL, d_ff), wsel),
            pl.BlockSpec((1, 1, d_ff), wsel),
            pl.BlockSpec((1, d_ff, D_MODEL), wsel),
            pl.BlockSpec((1, 1, D_MODEL), wsel),
        ],
        out_specs=pl.BlockSpec(memory_space=pl.ANY),
        scratch_shapes=[
            pltpu.VMEM((D_MODEL, d_ff), BF16),
            pltpu.VMEM((D_MODEL, d_ff), BF16),
            pltpu.VMEM((d_ff, D_MODEL), BF16),
        ] + [pltpu.VMEM((ROW_BLOCK * ROW_TILES, LANES), F32)] * (2 * RING) + [
            pltpu.VMEM((TRASH_TOK * ROW_TILES, LANES), F32),
            pltpu.SemaphoreType.DMA((RING,)),
            pltpu.SemaphoreType.DMA((RING,)),
            pltpu.SemaphoreType.DMA(()),
        ],
    )
    return pl.pallas_call(
        functools.partial(_experts_kernel, n_tok=n_tok),
        grid_spec=grid_spec,
        out_shape=jax.ShapeDtypeStruct((TOP_K * (n_tok + TRASH_TOK) * ROW_TILES, LANES), F32),
        compiler_params=pltpu.CompilerParams(
            dimension_semantics=("arbitrary",), vmem_limit_bytes=VMEM_LIMIT),
        name="experts",
    )(blk_e, n_valid, inv, hn_all, w_gate, b_gate, w_up, b_up, w_down, b_down)


def _combine_kernel(gate_ref, h_ref, fw_ref, *refs):
    y_ref = refs[TOP_K]
    gate = gate_ref[...]
    acc = h_ref[...]
    for k in range(TOP_K):
        y_k = jnp.concatenate(
            [refs[k][pl.ds(c, TOK_TILE, stride=ROW_TILES), :] for c in range(ROW_TILES)], axis=1)
        acc = acc + gate[:, k:k + 1] * y_k
    y_ref[...] = acc * lax.rsqrt(jnp.mean(acc * acc, axis=-1, keepdims=True) + EPS) * fw_ref[...]


def _combine(gates, h_group, final_w, y_pairs, *, tile0, plane_rows):
    n_tiles = h_group.shape[0] // TOK_TILE
    plane_tiles = plane_rows // TOK_TILE

    def plane(k):
        return pl.BlockSpec((TOK_TILE * ROW_TILES, LANES), lambda i: (k * plane_tiles + tile0 + i, 0))

    return pl.pallas_call(
        _combine_kernel,
        grid=(n_tiles,),
        in_specs=[
            pl.BlockSpec((TOK_TILE, LANES), lambda i: (tile0 + i, 0)),
            pl.BlockSpec((TOK_TILE, D_MODEL), lambda i: (i, 0)),
            pl.BlockSpec((1, D_MODEL), lambda i: (0, 0)),
        ] + [plane(k) for k in range(TOP_K)],
        out_specs=pl.BlockSpec((TOK_TILE, D_MODEL), lambda i: (i, 0)),
        out_shape=jax.ShapeDtypeStruct((n_tiles * TOK_TILE, D_MODEL), F32),
        compiler_params=pltpu.CompilerParams(dimension_semantics=("arbitrary",)),
        name="combine",
    )(gates, h_group, final_w, *([y_pairs] * TOP_K))


def _permute_w_in(w):
    o_a = QKV_WIDTH
    o_g = o_a + 2 * DN_HEADS
    o_rest = o_g + DN_WIDTH
    pad = jnp.zeros((w.shape[0], LANES - 2 * DN_HEADS), w.dtype)
    return jnp.concatenate([w[:, :o_a], w[:, o_g:o_rest], w[:, o_rest:], w[:, o_a:o_g], pad], axis=1)


def _lane_row(v, fill=0.0):
    return jnp.concatenate([v.astype(F32), jnp.full((LANES - v.shape[0],), fill, F32)]).reshape(1, LANES)


def kernel(x_prompt, x_sample, mem_prompt, state_dn, state_dn_conv, state_sc_conv, cache_mem_k, cache_mem_v, w_in, dn_conv_w, dn_A_log, dn_dt_bias, dn_norm_w, sc_conv_w, mem_norm_w, w_mem_kv, w_br, w_o, norm1_w, norm2_w, w_router, b_router, w_gate, b_gate, w_up, b_up, w_down, b_down, final_norm_w):
    assert w_in.shape[0] == 1, "one layer"
    bp, tp, _ = x_prompt.shape
    bs, ts, _ = x_sample.shape
    n_p, n_s = bp * tp, bs * ts
    n_tok = n_p + n_s
    assert n_p % TOK_TILE == 0 and n_s % TOK_TILE == 0 and tp % MIX_TILE == 0

    weights = (
        _permute_w_in(w_in[0]).astype(BF16),
        w_br[0].astype(BF16),
        w_o[0].astype(BF16),
        norm1_w[0].reshape(1, D_MODEL),
        norm2_w[0].reshape(1, D_MODEL),
        dn_conv_w[0],
        sc_conv_w[0],
        _lane_row(dn_A_log[0]),
        _lane_row(dn_dt_bias[0]),
        dn_norm_w[0].reshape(1, DN_HEAD_DIM),
        jnp.concatenate([w_router[0], jnp.zeros((D_MODEL, LANES - N_EXPERTS), F32)], axis=1),
        _lane_row(b_router[0], NEG_BIG),
    )

    mk2d, mv2d = _memkv(mem_prompt.reshape(bp * MEM_LEN, D_MODEL), mem_norm_w[0].reshape(1, D_MODEL),
                        w_mem_kv[0].astype(BF16))

    assert n_tok + TRASH_TOK <= PAIR_TOK_MASK + 1 and (n_tok + TRASH_TOK) % MIX_TILE == 0
    hn_all = jnp.zeros(((n_tok + TRASH_TOK) * ROW_TILES, LANES), F32)
    h_s, hn_all, lg_s, s_dnc, s_dns, s_scc = _mixer(
        x_sample.reshape(n_s, D_MODEL), state_dn_conv[0], state_dn[0], state_sc_conv[0],
        cache_mem_k[0].reshape(bs, MEM_LEN, MEM_WIDTH), cache_mem_v[0].reshape(bs, MEM_LEN, MEM_WIDTH),
        weights, hn_all, n_seq=bs, seq_len=ts, nb=bs, tt=ts, row0=n_p)
    h_p, hn_all, lg_p, p_dnc, p_dns, p_scc = _mixer(
        x_prompt.reshape(n_p, D_MODEL),
        jnp.zeros((bp, DN_CONV - 1, QKV_WIDTH), F32),
        jnp.zeros((bp, DN_HEADS, DN_HEAD_DIM, DN_HEAD_DIM), F32),
        jnp.zeros((bp, SC_CONV - 1, SC_WIDTH), F32),
        mk2d.reshape(bp, MEM_LEN, MEM_WIDTH), mv2d.reshape(bp, MEM_LEN, MEM_WIDTH),
        weights, hn_all, n_seq=bp, seq_len=tp, nb=1, tt=MIX_TILE, row0=0)

    idx, gates, rank, counts = _router(jnp.concatenate([lg_p, lg_s], axis=0))
    counts = counts[0, :N_EXPERTS]
    n_blk_e = (counts + ROW_BLOCK - 1) // ROW_BLOCK
    blk_end = jnp.cumsum(n_blk_e)
    blk_start = blk_end - n_blk_e
    row_start = blk_start * ROW_BLOCK
    experts = jnp.arange(N_EXPERTS, dtype=jnp.int32)
    start_of = jnp.sum(jnp.where(idx[:, :TOP_K, None] == experts, row_start, 0), axis=-1)
    dest = (start_of + rank[:, :TOP_K]).reshape(n_tok * TOP_K)

    n_blocks = (n_tok * TOP_K) // ROW_BLOCK + N_EXPERTS
    bi = jnp.arange(n_blocks, dtype=jnp.int32)
    n_valid = blk_end[-1].astype(jnp.int32)
    bclip = jnp.minimum(bi, n_valid - 1)
    blk_e = jnp.sum((bclip[:, None] >= blk_end[None, :]).astype(jnp.int32), axis=1)
    n_valid = n_valid.reshape(1)

    slots = jnp.arange(n_blocks * ROW_BLOCK, dtype=jnp.int32)
    default_inv = n_tok + (slots & (TRASH_TOK - 1))
    pairs = jnp.arange(n_tok * TOP_K, dtype=jnp.int32)
    pair_code = ((pairs % TOP_K) << PAIR_TOK_BITS) | (pairs // TOP_K)
    inv = _invert(dest.astype(jnp.int32), pair_code, default_inv)

    y_pairs = _experts(blk_e, n_valid, inv, hn_all,
                       w_gate[0], b_gate[0].reshape(N_EXPERTS, 1, -1), w_up[0], b_up[0].reshape(N_EXPERTS, 1, -1),
                       w_down[0], b_down[0].reshape(N_EXPERTS, 1, -1), n_tok=n_tok)
    fw = final_norm_w.reshape(1, D_MODEL)
    y_p = _combine(gates, h_p, fw, y_pairs, tile0=0, plane_rows=n_tok + TRASH_TOK)
    y_s = _combine(gates, h_s, fw, y_pairs, tile0=n_p // TOK_TILE, plane_rows=n_tok + TRASH_TOK)

    return (y_p.reshape(bp, tp, D_MODEL), y_s.reshape(bs, ts, D_MODEL),
            p_dns[None], p_dnc[None], p_scc[None],
            mk2d.reshape(1, bp, MEM_LEN, MEM_HEADS, MEM_HEAD_DIM), mv2d.reshape(1, bp, MEM_LEN, MEM_HEADS, MEM_HEAD_DIM),
            s_dns[None], s_dnc[None], s_scc[None])
```

```python
import functools

import jax
import jax.numpy as jnp
from jax import lax
from jax.experimental import pallas as pl
from jax.experimental.pallas import tpu as pltpu

F32 = jnp.float32
BF16 = jnp.bfloat16

D_MODEL = 1024
CHUNK = 64
EPS = 1e-6
DN_HEADS = 4
DN_HEAD_DIM = 128
DN_WIDTH = DN_HEADS * DN_HEAD_DIM
QKV_WIDTH = 3 * DN_WIDTH
DN_CONV = 4
SC_WIDTH = 256
SC_CONV = 3
MEM_LEN = 256
MEM_HEADS = 4
MEM_HEAD_DIM = 64
MEM_WIDTH = MEM_HEADS * MEM_HEAD_DIM
N_EXPERTS = 32
TOP_K = 4
SWIGLU_ALPHA = 1.702
SWIGLU_LIMIT = 7.0

LANES = 128
SUBLANES = 8
ROW_TILES = D_MODEL // LANES
assert ROW_TILES == SUBLANES
CONV_PAD = 8

OFF_QKV = 0
OFF_DNG = OFF_QKV + QKV_WIDTH
OFF_SC = OFF_DNG + DN_WIDTH
OFF_MQ = OFF_SC + 3 * SC_WIDTH
OFF_GATE = OFF_MQ + MEM_WIDTH
OFF_AB = OFF_GATE + 3 * D_MODEL
IN_PERM_WIDTH = OFF_AB + LANES

MIX_TILE = 512
TOK_TILE = 256
ROW_BLOCK = 256
INVERT_UNROLL = 32
TRASH_TOK = 256
RING = 3
PAIR_TOK_BITS = 16
PAIR_TOK_MASK = (1 << PAIR_TOK_BITS) - 1
VMEM_LIMIT = 56 * 1024 * 1024
NEG_BIG = -1e30


def _dot(a, b):
    return jnp.dot(a, b, preferred_element_type=F32)


def _dot_nt(a, b):
    return lax.dot_general(a, b, (((1,), (1,)), ((), ())), preferred_element_type=F32)


def _dot_tn(a, b):
    return lax.dot_general(a, b, (((0,), (0,)), ((), ())), preferred_element_type=F32)


def _sigmoid(x):
    return 1.0 / (1.0 + jnp.exp(-x))


def _softplus(x):
    return jnp.maximum(x, 0.0) + jnp.log1p(jnp.exp(-jnp.abs(x)))


def _for_each(n, body):
    if n == 1:
        body(0)
    else:
        def step(i, carry):
            body(i)
            return carry
        lax.fori_loop(0, n, step, 0)


def _memkv_kernel(mem_ref, nw_ref, w_ref, k_ref, v_ref):
    x = mem_ref[...]
    xn = x * lax.rsqrt(jnp.mean(x * x, axis=-1, keepdims=True) + EPS) * nw_ref[...]
    kv = _dot(xn.astype(BF16), w_ref[...])
    k_ref[...] = kv[:, :MEM_WIDTH]
    v_ref[...] = kv[:, MEM_WIDTH:]


def _memkv(mem2d, norm_w, w_kv_bf16):
    rows = mem2d.shape[0]
    grid = rows // MEM_LEN
    return pl.pallas_call(
        _memkv_kernel,
        grid=(grid,),
        in_specs=[
            pl.BlockSpec((MEM_LEN, D_MODEL), lambda i: (i, 0)),
            pl.BlockSpec((1, D_MODEL), lambda i: (0, 0)),
            pl.BlockSpec((D_MODEL, 2 * MEM_WIDTH), lambda i: (0, 0)),
        ],
        out_specs=[
            pl.BlockSpec((MEM_LEN, MEM_WIDTH), lambda i: (i, 0)),
            pl.BlockSpec((MEM_LEN, MEM_WIDTH), lambda i: (i, 0)),
        ],
        out_shape=[jax.ShapeDtypeStruct((rows, MEM_WIDTH), F32)] * 2,
        name="memkv",
    )(mem2d, norm_w, w_kv_bf16)


def _unit_lower_inverse(a, eye, size):
    inv = eye - a
    power = a
    span = 2
    while span < size:
        power = _dot(power, power)
        inv = _dot(inv, eye + power)
        span *= 2
    return inv


def _mixer_kernel(x_ref, dnc_in, dns_in, scc_in, mk_ref, mv_ref, w_in, w_br, w_o, n1_ref, n2_ref, dcw_ref, scw_ref,
                  alog_ref, dtb_ref, dnw_ref, wr_ref, br_ref, hn_all_ref,
                  h_ref, hn_ref, lg_ref, dnc_out, dns_out, scc_out,
                  xp, scp, q_s, k_s, v_s, gb_s, o_s, mq_s, ysc_s, ymem_s, wv_s, wk_s, qk_s, qd_s, kd_s, cd_s,
                  *, nb, tt, chunk):
    t_idx = pl.program_id(1)
    rows = nb * tt
    n_chunk = tt // chunk

    @pl.when(t_idx == 0)
    def _():
        xp[:, CONV_PAD - (DN_CONV - 1):CONV_PAD, :] = dnc_in[...]
        scp[:, CONV_PAD - (SC_CONV - 1):CONV_PAD, :] = scc_in[...]
        dns_out[...] = dns_in[...]

    x = x_ref[...]
    xn = (x * lax.rsqrt(jnp.mean(x * x, axis=-1, keepdims=True) + EPS) * n1_ref[...]).astype(BF16)

    def proj(off, width):
        return _dot(xn, w_in[:, off:off + width])

    qkv_pre = proj(OFF_QKV, QKV_WIDTH)
    for s in range(nb):
        xp[s, CONV_PAD:CONV_PAD + tt, :] = qkv_pre[s * tt:(s + 1) * tt, :]
    sc = proj(OFF_SC, 3 * SC_WIDTH)
    sc_b = sc[:, :SC_WIDTH]
    sc_ch = sc[:, SC_WIDTH:2 * SC_WIDTH] * sc[:, 2 * SC_WIDTH:]
    for s in range(nb):
        scp[s, CONV_PAD:CONV_PAD + tt, :] = sc_ch[s * tt:(s + 1) * tt, :]
    mq_s[...] = proj(OFF_MQ, MEM_WIDTH)

    ab = proj(OFF_AB, LANES)
    lane = lax.broadcasted_iota(jnp.int32, (rows, LANES), 1)
    g_log = -jnp.exp(alog_ref[...]) * _softplus(ab + dtb_ref[...])
    gb_s[...] = jnp.where(lane < DN_HEADS, g_log, _sigmoid(ab))

    def conv_seq(s):
        base = CONV_PAD - (DN_CONV - 1)
        acc = dcw_ref[0:1, :] * xp[s, pl.ds(base, tt), :]
        for j in range(1, DN_CONV):
            acc = acc + dcw_ref[j:j + 1, :] * xp[s, pl.ds(base + j, tt), :]
        act = acc * _sigmoid(acc)
        r0 = pl.multiple_of(s * tt, tt)
        for hd in range(DN_HEADS):
            lo = hd * DN_HEAD_DIM
            qh = act[:, lo:lo + DN_HEAD_DIM]
            kh = act[:, DN_WIDTH + lo:DN_WIDTH + lo + DN_HEAD_DIM]
            q_s[pl.ds(r0, tt), lo:lo + DN_HEAD_DIM] = (
                qh * lax.rsqrt(jnp.sum(qh * qh, axis=-1, keepdims=True) + EPS) * (DN_HEAD_DIM ** -0.5))
            k_s[pl.ds(r0, tt), lo:lo + DN_HEAD_DIM] = (
                kh * lax.rsqrt(jnp.sum(kh * kh, axis=-1, keepdims=True) + EPS))
        v_s[pl.ds(r0, tt), :] = act[:, 2 * DN_WIDTH:]
        tail = xp[s, pl.ds(tt + base, DN_CONV - 1), :]
        dnc_out[s] = tail
        xp[s, pl.ds(base, DN_CONV - 1), :] = tail

        base2 = CONV_PAD - (SC_CONV - 1)
        acc2 = scw_ref[0:1, :] * scp[s, pl.ds(base2, tt), :]
        for j in range(1, SC_CONV):
            acc2 = acc2 + scw_ref[j:j + 1, :] * scp[s, pl.ds(base2 + j, tt), :]
        ysc_s[pl.ds(r0, tt), :] = acc2
        tail2 = scp[s, pl.ds(tt + base2, SC_CONV - 1), :]
        scc_out[s] = tail2
        scp[s, pl.ds(base2, SC_CONV - 1), :] = tail2

    _for_each(nb, conv_seq)

    hl = DN_HEADS * chunk
    shift = chunk.bit_length() - 1
    ri = lax.broadcasted_iota(jnp.int32, (hl, hl), 0)
    ci = lax.broadcasted_iota(jnp.int32, (hl, hl), 1)
    same_head = lax.shift_right_logical(ri, shift) == lax.shift_right_logical(ci, shift)
    causal = same_head & (ri >= ci)
    strict = same_head & (ri > ci)
    eye = (ri == ci).astype(F32)
    ri1 = lax.broadcasted_iota(jnp.int32, (chunk, chunk), 0)
    ci1 = lax.broadcasted_iota(jnp.int32, (chunk, chunk), 1)
    tril = (ri1 >= ci1).astype(F32)
    triu = (ri1 <= ci1).astype(F32)

    def stack_heads(ref, r0):
        return jnp.concatenate(
            [ref[r0:r0 + chunk, hd * DN_HEAD_DIM:(hd + 1) * DN_HEAD_DIM] for hd in range(DN_HEADS)], axis=0)

    for c in range(nb * n_chunk):
        r0 = c * chunk
        gb = gb_s[r0:r0 + chunk, :]
        cum = _dot(tril, gb)
        cum_t = _dot_tn(gb, triu)
        cc = jnp.concatenate([cum[:, hd:hd + 1] for hd in range(DN_HEADS)], axis=0)
        cr = jnp.concatenate([cum_t[hd:hd + 1, :] for hd in range(DN_HEADS)], axis=1)
        beta = jnp.concatenate([gb[:, DN_HEADS + hd:DN_HEADS + hd + 1] for hd in range(DN_HEADS)], axis=0)
        c_last = jnp.concatenate(
            [jnp.broadcast_to(cum[chunk - 1:chunk, hd:hd + 1], (chunk, 1)) for hd in range(DN_HEADS)], axis=0)
        qst, kst, vst = stack_heads(q_s, r0), stack_heads(k_s, r0), stack_heads(v_s, r0)
        decay = jnp.where(causal, jnp.exp(jnp.where(causal, cc - cr, 0.0)), 0.0)
        a_mat = jnp.where(strict, decay * _dot_nt(kst, kst), 0.0) * beta
        t_inv = _unit_lower_inverse(a_mat, eye, chunk)
        e_cum = jnp.exp(cc)
        w = _dot(t_inv, jnp.concatenate([beta * vst, (beta * e_cum) * kst], axis=1))
        wv_s[c] = w[:, :DN_HEAD_DIM]
        wk_s[c] = w[:, DN_HEAD_DIM:]
        qk_s[c] = _dot_nt(qst, kst) * decay
        qd_s[c] = e_cum * qst
        kd_s[c] = jnp.exp(c_last - cc) * kst
        cd_s[c] = jnp.broadcast_to(jnp.exp(c_last), (hl, DN_HEAD_DIM))

    for c in range(nb * n_chunk):
        s = c // n_chunk
        r0 = c * chunk
        states, us = [], []
        for hd in range(DN_HEADS):
            hr = slice(hd * chunk, (hd + 1) * chunk)
            state = dns_out[s, hd]
            states.append(state)
            us.append(wv_s[c, hr, :] - _dot(wk_s[c, hr, :], state))
        o_intra = _dot(qk_s[c], jnp.concatenate(us, axis=0))
        for hd in range(DN_HEADS):
            hr = slice(hd * chunk, (hd + 1) * chunk)
            lo = hd * DN_HEAD_DIM
            o_s[r0:r0 + chunk, lo:lo + DN_HEAD_DIM] = _dot(qd_s[c, hr, :], states[hd]) + o_intra[hr, :]
            dns_out[s, hd] = (cd_s[c, hd * chunk:hd * chunk + 1, :] * states[hd]
                              + _dot_tn(kd_s[c, hr, :], us[hd]))

    def attn_seq(s):
        r0 = pl.multiple_of(s * tt, tt)
        mq = mq_s[pl.ds(r0, tt), :]
        for hd in range(MEM_HEADS):
            lo = hd * MEM_HEAD_DIM
            qh = mq[:, lo:lo + MEM_HEAD_DIM].astype(BF16)
            kh = mk_ref[s, :, lo:lo + MEM_HEAD_DIM].astype(BF16)
            vh = mv_ref[s, :, lo:lo + MEM_HEAD_DIM].astype(BF16)
            sc_h = _dot_nt(qh, kh) * (MEM_HEAD_DIM ** -0.5)
            p = jnp.exp(sc_h - jnp.max(sc_h, axis=-1, keepdims=True))
            denom = jnp.sum(p, axis=-1, keepdims=True)
            ymem_s[pl.ds(r0, tt), lo:lo + MEM_HEAD_DIM] = _dot(p.astype(BF16), vh) / denom

    _for_each(nb, attn_seq)

    dn_gate = proj(OFF_DNG, DN_WIDTH)
    o_all = o_s[...]
    y_heads = []
    for hd in range(DN_HEADS):
        lo = hd * DN_HEAD_DIM
        oh = o_all[:, lo:lo + DN_HEAD_DIM]
        oh = oh * lax.rsqrt(jnp.mean(oh * oh, axis=-1, keepdims=True) + EPS) * dnw_ref[...]
        gh = dn_gate[:, lo:lo + DN_HEAD_DIM]
        y_heads.append(oh * (gh * _sigmoid(gh)))
    y_dn = jnp.concatenate(y_heads, axis=-1).astype(BF16)
    y_sc = (sc_b * ysc_s[...]).astype(BF16)
    y_mem = ymem_s[...].astype(BF16)

    merged = _sigmoid(proj(OFF_GATE, D_MODEL)) * _dot(y_dn, w_br[0:DN_WIDTH, :])
    merged = merged + _sigmoid(proj(OFF_GATE + D_MODEL, D_MODEL)) * _dot(y_sc, w_br[DN_WIDTH:DN_WIDTH + SC_WIDTH, :])
    merged = merged + _sigmoid(proj(OFF_GATE + 2 * D_MODEL, D_MODEL)) * _dot(y_mem, w_br[DN_WIDTH + SC_WIDTH:, :])
    h = x + _dot(merged.astype(BF16), w_o[...])
    h_ref[...] = h
    hn = h * lax.rsqrt(jnp.mean(h * h, axis=-1, keepdims=True) + EPS) * n2_ref[...]
    for c in range(ROW_TILES):
        hn_ref[pl.ds(c, rows, stride=ROW_TILES), :] = hn[:, c * LANES:(c + 1) * LANES]
    lg_ref[...] = _dot(hn, wr_ref[...]) + br_ref[...]


def _mixer(x2d, dnc_in, dns_in, scc_in, mk, mv, weights, hn_all, *, n_seq, seq_len, nb, tt, row0):
    chunk = CHUNK if seq_len % CHUNK == 0 else seq_len
    rows = nb * tt
    n_t = seq_len // tt
    total_rows = n_seq * seq_len
    n_chunks = rows // chunk
    hl = DN_HEADS * chunk
    grid = (n_seq // nb, n_t)
    const = lambda b, t: (0, 0)
    seq3 = lambda b, t: (b, 0, 0)
    assert row0 % rows == 0

    def tok(b, t):
        return (b * n_t + t, 0)

    def tok_all(b, t):
        return (row0 // rows + b * n_t + t, 0)

    (w_in, w_br, w_o, n1, n2, dcw, scw, alog, dtb, dnw, wr, br) = weights
    in_specs = [
        pl.BlockSpec((rows, D_MODEL), tok),
        pl.BlockSpec((nb, DN_CONV - 1, QKV_WIDTH), seq3),
        pl.BlockSpec((nb, DN_HEADS, DN_HEAD_DIM, DN_HEAD_DIM), lambda b, t: (b, 0, 0, 0)),
        pl.BlockSpec((nb, SC_CONV - 1, SC_WIDTH), seq3),
        pl.BlockSpec((nb, MEM_LEN, MEM_WIDTH), seq3),
        pl.BlockSpec((nb, MEM_LEN, MEM_WIDTH), seq3),
        pl.BlockSpec(w_in.shape, const, pipeline_mode=pl.Buffered(1)),
        pl.BlockSpec(w_br.shape, const, pipeline_mode=pl.Buffered(1)),
        pl.BlockSpec(w_o.shape, const, pipeline_mode=pl.Buffered(1)),
        pl.BlockSpec(n1.shape, const),
        pl.BlockSpec(n2.shape, const),
        pl.BlockSpec(dcw.shape, const),
        pl.BlockSpec(scw.shape, const),
        pl.BlockSpec(alog.shape, const),
        pl.BlockSpec(dtb.shape, const),
        pl.BlockSpec(dnw.shape, const),
        pl.BlockSpec(wr.shape, const),
        pl.BlockSpec(br.shape, const),
        pl.BlockSpec(memory_space=pl.ANY),
    ]
    out_shape = [
        jax.ShapeDtypeStruct((total_rows, D_MODEL), F32),
        jax.ShapeDtypeStruct(hn_all.shape, F32),
        jax.ShapeDtypeStruct((total_rows, LANES), F32),
        jax.ShapeDtypeStruct((n_seq, DN_CONV - 1, QKV_WIDTH), F32),
        jax.ShapeDtypeStruct((n_seq, DN_HEADS, DN_HEAD_DIM, DN_HEAD_DIM), F32),
        jax.ShapeDtypeStruct((n_seq, SC_CONV - 1, SC_WIDTH), F32),
    ]
    out_specs = [
        pl.BlockSpec((rows, D_MODEL), tok),
        pl.BlockSpec((rows * ROW_TILES, LANES), tok_all),
        pl.BlockSpec((rows, LANES), tok),
        pl.BlockSpec((nb, DN_CONV - 1, QKV_WIDTH), seq3),
        pl.BlockSpec((nb, DN_HEADS, DN_HEAD_DIM, DN_HEAD_DIM), lambda b, t: (b, 0, 0, 0)),
        pl.BlockSpec((nb, SC_CONV - 1, SC_WIDTH), seq3),
    ]
    args = [x2d, dnc_in, dns_in, scc_in, mk, mv, w_in, w_br, w_o, n1, n2, dcw, scw, alog, dtb, dnw, wr, br, hn_all]
    scratch = [
        pltpu.VMEM((nb, CONV_PAD + tt, QKV_WIDTH), F32),
        pltpu.VMEM((nb, CONV_PAD + tt, SC_WIDTH), F32),
        pltpu.VMEM((rows, DN_WIDTH), F32),
        pltpu.VMEM((rows, DN_WIDTH), F32),
        pltpu.VMEM((rows, DN_WIDTH), F32),
        pltpu.VMEM((rows, LANES), F32),
        pltpu.VMEM((rows, DN_WIDTH), F32),
        pltpu.VMEM((rows, MEM_WIDTH), F32),
        pltpu.VMEM((rows, SC_WIDTH), F32),
        pltpu.VMEM((rows, MEM_WIDTH), F32),
        pltpu.VMEM((n_chunks, hl, DN_HEAD_DIM), F32),
        pltpu.VMEM((n_chunks, hl, DN_HEAD_DIM), F32),
        pltpu.VMEM((n_chunks, hl, hl), F32),
        pltpu.VMEM((n_chunks, hl, DN_HEAD_DIM), F32),
        pltpu.VMEM((n_chunks, hl, DN_HEAD_DIM), F32),
        pltpu.VMEM((n_chunks, hl, DN_HEAD_DIM), F32),
    ]
    return pl.pallas_call(
        functools.partial(_mixer_kernel, nb=nb, tt=tt, chunk=chunk),
        grid=grid,
        in_specs=in_specs,
        out_specs=out_specs,
        out_shape=out_shape,
        scratch_shapes=scratch,
        input_output_aliases={len(args) - 1: 1},
        compiler_params=pltpu.CompilerParams(
            dimension_semantics=("arbitrary", "arbitrary"), vmem_limit_bytes=VMEM_LIMIT),
        name="mixer",
    )(*args)


def _router_kernel(lg_ref, idx_ref, gate_ref, rank_ref, cnt_ref, carry):
    i = pl.program_id(0)

    @pl.when(i == 0)
    def _():
        carry[...] = jnp.zeros_like(carry)

    work = lg_ref[...]
    tm = work.shape[0]
    lane = lax.broadcasted_iota(jnp.int32, (tm, LANES), 1).astype(F32)
    idxs, vals = [], []
    for _ in range(TOP_K):
        m = jnp.max(work, axis=-1, keepdims=True)
        ik = jnp.min(jnp.where(work == m, lane, float(LANES)), axis=-1, keepdims=True)
        idxs.append(ik)
        vals.append(m)
        work = jnp.where(lane == ik, -jnp.inf, work)
    exps = [jnp.exp(v - vals[0]) for v in vals]
    denom = exps[0] + exps[1] + exps[2] + exps[3]
    hot = jnp.zeros((tm, LANES), F32)
    for ik in idxs:
        hot = hot + (lane == ik).astype(F32)
    ri = lax.broadcasted_iota(jnp.int32, (tm, tm), 0)
    ci = lax.broadcasted_iota(jnp.int32, (tm, tm), 1)
    before = (ri > ci).astype(BF16)
    prefix = _dot(before, hot.astype(BF16)) + carry[...]
    idx_out = jnp.zeros((tm, LANES), F32)
    gate_out = jnp.zeros((tm, LANES), F32)
    rank_out = jnp.zeros((tm, LANES), F32)
    for k in range(TOP_K):
        rk = jnp.sum(jnp.where(lane == idxs[k], prefix, 0.0), axis=-1, keepdims=True)
        idx_out = jnp.where(lane == k, idxs[k], idx_out)
        gate_out = jnp.where(lane == k, exps[k] / denom, gate_out)
        rank_out = jnp.where(lane == k, rk, rank_out)
    idx_ref[...] = idx_out.astype(jnp.int32)
    gate_ref[...] = gate_out
    rank_ref[...] = rank_out.astype(jnp.int32)
    carry[...] = carry[...] + jnp.sum(hot, axis=0, keepdims=True)
    cnt_ref[...] = carry[...].astype(jnp.int32)


def _router(logits):
    n_tok = logits.shape[0]
    tile = lambda i: (i, 0)
    return pl.pallas_call(
        _router_kernel,
        grid=(n_tok // TOK_TILE,),
        in_specs=[pl.BlockSpec((TOK_TILE, LANES), tile)],
        out_specs=[pl.BlockSpec((TOK_TILE, LANES), tile)] * 3 + [pl.BlockSpec((1, LANES), lambda i: (0, 0))],
        out_shape=[
            jax.ShapeDtypeStruct((n_tok, LANES), jnp.int32),
            jax.ShapeDtypeStruct((n_tok, LANES), F32),
            jax.ShapeDtypeStruct((n_tok, LANES), jnp.int32),
            jax.ShapeDtypeStruct((1, LANES), jnp.int32),
        ],
        scratch_shapes=[pltpu.VMEM((1, LANES), F32)],
        compiler_params=pltpu.CompilerParams(dimension_semantics=("arbitrary",)),
        name="router",
    )(logits)


def _invert_kernel(dest_ref, code_ref, default_ref, inv_ref, sem):
    i = pl.program_id(0)

    @pl.when(i == 0)
    def _():
        cp = pltpu.make_async_copy(default_ref, inv_ref, sem)
        cp.start()
        cp.wait()

    def body(j, c):
        for d in range(INVERT_UNROLL):
            q = j * INVERT_UNROLL + d
            inv_ref[dest_ref[q]] = code_ref[q]
        return c

    lax.fori_loop(0, TOK_TILE * TOP_K // INVERT_UNROLL, body, 0)


def _invert(dest_flat, pair_code, default_inv):
    n_pairs = TOK_TILE * TOP_K
    return pl.pallas_call(
        _invert_kernel,
        grid=(dest_flat.shape[0] // n_pairs,),
        in_specs=[
            pl.BlockSpec((n_pairs,), lambda i: (i,), memory_space=pltpu.SMEM),
            pl.BlockSpec((n_pairs,), lambda i: (i,), memory_space=pltpu.SMEM),
            pl.BlockSpec(memory_space=pl.ANY),
        ],
        out_specs=pl.BlockSpec(memory_space=pltpu.SMEM),
        out_shape=jax.ShapeDtypeStruct(default_inv.shape, jnp.int32),
        scratch_shapes=[pltpu.SemaphoreType.DMA(())],
        compiler_params=pltpu.CompilerParams(dimension_semantics=("arbitrary",)),
        name="invert",
    )(dest_flat, pair_code, default_inv)


def _experts_kernel(blk_e, n_valid, inv, hn_ref, wg_ref, bg_ref, wu_ref, bu_ref, wd_ref, bd_ref, yp_ref,
                    wg_bf, wu_bf, wd_bf, xbuf0, xbuf1, xbuf2, ybuf0, ybuf1, ybuf2, zeros, gsem, ssem, zsem,
                    *, n_tok):
    i = pl.program_id(0)
    nv = n_valid[0]
    phase = lax.rem(i, RING)
    xbuf = (xbuf0, xbuf1, xbuf2)
    ybuf = (ybuf0, ybuf1, ybuf2)
    plane_rows = n_tok + TRASH_TOK

    def tile_rows(row, n=1):
        return pl.ds(pl.multiple_of(row * ROW_TILES, ROW_TILES), n * ROW_TILES)

    def gather(block, s):
        base = block * ROW_BLOCK
        for r in range(ROW_BLOCK):
            tok = inv[base + r] & PAIR_TOK_MASK
            pltpu.make_async_copy(
                hn_ref.at[tile_rows(tok)], xbuf[s].at[tile_rows(r)], gsem.at[s]).start(priority=r % 2)

    def scatter(block, s):
        base = block * ROW_BLOCK
        for r in range(ROW_BLOCK):
            pair = inv[base + r]
            row = lax.shift_right_logical(pair, PAIR_TOK_BITS) * plane_rows + (pair & PAIR_TOK_MASK)
            pltpu.make_async_copy(
                ybuf[s].at[tile_rows(r)], yp_ref.at[tile_rows(row)], ssem.at[s]).start(priority=(r + 1) % 2)

    def wait_gather(s):
        pltpu.make_async_copy(hn_ref.at[tile_rows(0, ROW_BLOCK)], xbuf[s], gsem.at[s]).wait()

    def wait_scatter(s):
        pltpu.make_async_copy(ybuf[s], yp_ref.at[tile_rows(0, ROW_BLOCK)], ssem.at[s]).wait()

    @pl.when(i == 0)
    def _():
        zeros[...] = jnp.zeros_like(zeros)
        for k in range(TOP_K):
            cp = pltpu.make_async_copy(zeros, yp_ref.at[tile_rows(k * plane_rows + n_tok, TRASH_TOK)], zsem)
            cp.start()
            cp.wait()
        gather(0, 0)
        gather(jnp.minimum(1, nv - 1), 1)

    e = blk_e[i]
    prev = blk_e[jnp.maximum(i - 1, 0)]

    @pl.when(((i == 0) | (e != prev)) & (i < nv))
    def _():
        wg_bf[...] = wg_ref[0].astype(BF16)
        wu_bf[...] = wu_ref[0].astype(BF16)
        wd_bf[...] = wd_ref[0].astype(BF16)

    def compute(slot):
        wait_gather(slot)
        x = jnp.concatenate(
            [xbuf[slot][pl.ds(c, ROW_BLOCK, stride=ROW_TILES), :] for c in range(ROW_TILES)], axis=1).astype(BF16)
        gl = jnp.minimum(_dot(x, wg_bf[...]) + bg_ref[0], SWIGLU_LIMIT)
        ul = jnp.clip(_dot(x, wu_bf[...]) + bu_ref[0], -SWIGLU_LIMIT, SWIGLU_LIMIT)
        act = (ul + 1.0) * (gl * _sigmoid(SWIGLU_ALPHA * gl))
        y = _dot(act.astype(BF16), wd_bf[...]) + bd_ref[0]
        for c in range(ROW_TILES):
            ybuf[slot][pl.ds(c, ROW_BLOCK, stride=ROW_TILES), :] = y[:, c * LANES:(c + 1) * LANES]

    def finish(slot):
        ahead, behind = (slot + 2) % RING, (slot - 1) % RING
        scatter(i, slot)
        wait_gather((slot + 1) % RING)
        wait_gather(ahead)

        @pl.when(i >= 2)
        def _():
            wait_scatter((slot - 2) % RING)

        @pl.when(i >= 1)
        def _():
            wait_scatter(behind)

        wait_scatter(slot)

    for slot in range(RING):
        mine = (phase == slot) & (i < nv)

        @pl.when(mine & (i >= RING))
        def _():
            wait_scatter(slot)

        @pl.when(mine)
        def _():
            gather(jnp.minimum(i + 2, nv - 1), (slot + 2) % RING)

        @pl.when(mine)
        def _():
            compute(slot)

        @pl.when(mine & (i > 0))
        def _():
            scatter(i - 1, (slot - 1) % RING)

        @pl.when(mine & (i == nv - 1))
        def _():
            finish(slot)


def _experts(blk_e, n_valid, inv, hn_all, w_gate, b_gate, w_up, b_up, w_down, b_down, *, n_tok):
    n_blocks = blk_e.shape[0]
    wsel = lambda i, be, nv, iv: (be[i], 0, 0)
    d_ff = w_gate.shape[-1]
    grid_spec = pltpu.PrefetchScalarGridSpec(
        num_scalar_prefetch=3,
        grid=(n_blocks,),
        in_specs=[
            pl.BlockSpec(memory_space=pl.ANY),
            pl.BlockSpec((1, D_MODEL, d_ff), wsel),
            pl.BlockSpec((1, 1, d_ff), wsel),
            pl.BlockSpec((1, D_MODEL, d_ff), wsel),
            pl.BlockSpec((1, 1, d_ff), wsel),
            pl.BlockSpec((1, d_ff, D_MODEL), wsel),
            pl.BlockSpec((1, 1, D_MODEL), wsel),
        ],
        out_specs=pl.BlockSpec(memory_space=pl.ANY),
        scratch_shapes=[
            pltpu.VMEM((D_MODEL, d_ff), BF16),
            pltpu.VMEM((D_MODEL, d_ff), BF16),
            pltpu.VMEM((d_ff, D_MODEL), BF16),
        ] + [pltpu.VMEM((ROW_BLOCK * ROW_TILES, LANES), F32)] * (2 * RING) + [
            pltpu.VMEM((TRASH_TOK * ROW_TILES, LANES), F32),
            pltpu.SemaphoreType.DMA((RING,)),
            pltpu.SemaphoreType.DMA((RING,)),
            pltpu.SemaphoreType.DMA(()),
        ],
    )
    return pl.pallas_call(
        functools.partial(_experts_kernel, n_tok=n_tok),
        grid_spec=grid_spec,
        out_shape=jax.ShapeDtypeStruct((TOP_K * (n_tok + TRASH_TOK) * ROW_TILES, LANES), F32),
        compiler_params=pltpu.CompilerParams(
            dimension_semantics=("arbitrary",), vmem_limit_bytes=VMEM_LIMIT),
        name="experts",
    )(blk_e, n_valid, inv, hn_all, w_gate, b_gate, w_up, b_up, w_down, b_down)


def _combine_kernel(gate_ref, h_ref, fw_ref, *refs):
    y_ref = refs[TOP_K]
    gate = gate_ref[...]
    acc = h_ref[...]
    for k in range(TOP_K):
        y_k = jnp.concatenate(
            [refs[k][pl.ds(c, TOK_TILE, stride=ROW_TILES), :] for c in range(ROW_TILES)], axis=1)
        acc = acc + gate[:, k:k + 1] * y_k
    y_ref[...] = acc * lax.rsqrt(jnp.mean(acc * acc, axis=-1, keepdims=True) + EPS) * fw_ref[...]


def _combine(gates, h_group, final_w, y_pairs, *, tile0, plane_rows):
    n_tiles = h_group.shape[0] // TOK_TILE
    plane_tiles = plane_rows // TOK_TILE

    def plane(k):
        return pl.BlockSpec((TOK_TILE * ROW_TILES, LANES), lambda i: (k * plane_tiles + tile0 + i, 0))

    return pl.pallas_call(
        _combine_kernel,
        grid=(n_tiles,),
        in_specs=[
            pl.BlockSpec((TOK_TILE, LANES), lambda i: (tile0 + i, 0)),
            pl.BlockSpec((TOK_TILE, D_MODEL), lambda i: (i, 0)),
            pl.BlockSpec((1, D_MODEL), lambda i: (0, 0)),
        ] + [plane(k) for k in range(TOP_K)],
        out_specs=pl.BlockSpec((TOK_TILE, D_MODEL), lambda i: (i, 0)),
        out_shape=jax.ShapeDtypeStruct((n_tiles * TOK_TILE, D_MODEL), F32),
        compiler_params=pltpu.CompilerParams(dimension_semantics=("arbitrary",)),
        name="combine",
    )(gates, h_group, final_w, *([y_pairs] * TOP_K))


def _permute_w_in(w):
    o_a = QKV_WIDTH
    o_g = o_a + 2 * DN_HEADS
    o_rest = o_g + DN_WIDTH
    pad = jnp.zeros((w.shape[0], LANES - 2 * DN_HEADS), BF16)
    pieces = [w[:, :o_a], w[:, o_g:o_rest], w[:, o_rest:], w[:, o_a:o_g]]
    return jnp.concatenate([p.astype(BF16) for p in pieces] + [pad], axis=1)


def _lane_row(v, fill=0.0):
    return jnp.concatenate([v.astype(F32), jnp.full((LANES - v.shape[0],), fill, F32)]).reshape(1, LANES)


def kernel(x_prompt, x_sample, mem_prompt, state_dn, state_dn_conv, state_sc_conv, cache_mem_k, cache_mem_v, w_in, dn_conv_w, dn_A_log, dn_dt_bias, dn_norm_w, sc_conv_w, mem_norm_w, w_mem_kv, w_br, w_o, norm1_w, norm2_w, w_router, b_router, w_gate, b_gate, w_up, b_up, w_down, b_down, final_norm_w):
    assert w_in.shape[0] == 1, "one layer"
    bp, tp, _ = x_prompt.shape
    bs, ts, _ = x_sample.shape
    n_p, n_s = bp * tp, bs * ts
    n_tok = n_p + n_s
    assert n_p % TOK_TILE == 0 and n_s % TOK_TILE == 0 and tp % MIX_TILE == 0

    weights = (
        _permute_w_in(w_in[0]),
        w_br[0].astype(BF16),
        w_o[0].astype(BF16),
        norm1_w[0].reshape(1, D_MODEL),
        norm2_w[0].reshape(1, D_MODEL),
        dn_conv_w[0],
        sc_conv_w[0],
        _lane_row(dn_A_log[0]),
        _lane_row(dn_dt_bias[0]),
        dn_norm_w[0].reshape(1, DN_HEAD_DIM),
        jnp.concatenate([w_router[0], jnp.zeros((D_MODEL, LANES - N_EXPERTS), F32)], axis=1),
        _lane_row(b_router[0], NEG_BIG),
    )

    mk2d, mv2d = _memkv(mem_prompt.reshape(bp * MEM_LEN, D_MODEL), mem_norm_w[0].reshape(1, D_MODEL),
                        w_mem_kv[0].astype(BF16))

    assert n_tok + TRASH_TOK <= PAIR_TOK_MASK + 1 and (n_tok + TRASH_TOK) % MIX_TILE == 0
    hn_all = jnp.zeros(((n_tok + TRASH_TOK) * ROW_TILES, LANES), F32)
    h_s, hn_all, lg_s, s_dnc, s_dns, s_scc = _mixer(
        x_sample.reshape(n_s, D_MODEL), state_dn_conv[0], state_dn[0], state_sc_conv[0],
        cache_mem_k[0].reshape(bs, MEM_LEN, MEM_WIDTH), cache_mem_v[0].reshape(bs, MEM_LEN, MEM_WIDTH),
        weights, hn_all, n_seq=bs, seq_len=ts, nb=bs, tt=ts, row0=n_p)
    h_p, hn_all, lg_p, p_dnc, p_dns, p_scc = _mixer(
        x_prompt.reshape(n_p, D_MODEL),
        jnp.zeros((bp, DN_CONV - 1, QKV_WIDTH), F32),
        jnp.zeros((bp, DN_HEADS, DN_HEAD_DIM, DN_HEAD_DIM), F32),
        jnp.zeros((bp, SC_CONV - 1, SC_WIDTH), F32),
        mk2d.reshape(bp, MEM_LEN, MEM_WIDTH), mv2d.reshape(bp, MEM_LEN, MEM_WIDTH),
        weights, hn_all, n_seq=bp, seq_len=tp, nb=1, tt=MIX_TILE, row0=0)

    idx, gates, rank, counts = _router(jnp.concatenate([lg_p, lg_s], axis=0))
    counts = counts[0, :N_EXPERTS]
    n_blk_e = (counts + ROW_BLOCK - 1) // ROW_BLOCK
    blk_end = jnp.cumsum(n_blk_e)
    blk_start = blk_end - n_blk_e
    row_start = blk_start * ROW_BLOCK
    experts = jnp.arange(N_EXPERTS, dtype=jnp.int32)
    start_of = jnp.sum(jnp.where(idx[:, :TOP_K, None] == experts, row_start, 0), axis=-1)
    dest = (start_of + rank[:, :TOP_K]).reshape(n_tok * TOP_K)

    n_blocks = (n_tok * TOP_K) // ROW_BLOCK + N_EXPERTS
    bi = jnp.arange(n_blocks, dtype=jnp.int32)
    n_valid = blk_end[-1].astype(jnp.int32)
    bclip = jnp.minimum(bi, n_valid - 1)
    blk_e = jnp.sum((bclip[:, None] >= blk_end[None, :]).astype(jnp.int32), axis=1)
    n_valid = n_valid.reshape(1)

    slots = jnp.arange(n_blocks * ROW_BLOCK, dtype=jnp.int32)
    default_inv = n_tok + (slots & (TRASH_TOK - 1))
    pairs = jnp.arange(n_tok * TOP_K, dtype=jnp.int32)
    pair_code = ((pairs % TOP_K) << PAIR_TOK_BITS) | (pairs // TOP_K)
    inv = _invert(dest.astype(jnp.int32), pair_code, default_inv)

    y_pairs = _experts(blk_e, n_valid, inv, hn_all,
                       w_gate[0], b_gate[0].reshape(N_EXPERTS, 1, -1), w_up[0], b_up[0].reshape(N_EXPERTS, 1, -1),
                       w_down[0], b_down[0].reshape(N_EXPERTS, 1, -1), n_tok=n_tok)
    fw = final_norm_w.reshape(1, D_MODEL)
    y_p = _combine(gates, h_p, fw, y_pairs, tile0=0, plane_rows=n_tok + TRASH_TOK)
    y_s = _combine(gates, h_s, fw, y_pairs, tile0=n_p // TOK_TILE, plane_rows=n_tok + TRASH_TOK)

    return (y_p.reshape(bp, tp, D_MODEL), y_s.reshape(bs, ts, D_MODEL),
            p_dns[None], p_dnc[None], p_scc[None],
            mk2d.reshape(1, bp, MEM_LEN, MEM_HEADS, MEM_HEAD_DIM), mv2d.reshape(1, bp, MEM_LEN, MEM_HEADS, MEM_HEAD_DIM),
            s_dns[None], s_dnc[None], s_scc[None])
```

```python
import functools

import jax
import jax.numpy as jnp
from jax import lax
from jax.experimental import pallas as pl
from jax.experimental.pallas import tpu as pltpu

F32 = jnp.float32
BF16 = jnp.bfloat16

D_MODEL = 1024
CHUNK = 64
EPS = 1e-6
DN_HEADS = 4
DN_HEAD_DIM = 128
DN_WIDTH = DN_HEADS * DN_HEAD_DIM
QKV_WIDTH = 3 * DN_WIDTH
DN_CONV = 4
SC_WIDTH = 256
SC_CONV = 3
MEM_LEN = 256
MEM_HEADS = 4
MEM_HEAD_DIM = 64
MEM_WIDTH = MEM_HEADS * MEM_HEAD_DIM
N_EXPERTS = 32
TOP_K = 4
SWIGLU_ALPHA = 1.702
SWIGLU_LIMIT = 7.0

LANES = 128
SUBLANES = 8
ROW_TILES = D_MODEL // LANES
assert ROW_TILES == SUBLANES
CONV_PAD = 8

OFF_QKV = 0
OFF_DNG = OFF_QKV + QKV_WIDTH
OFF_SC = OFF_DNG + DN_WIDTH
OFF_MQ = OFF_SC + 3 * SC_WIDTH
OFF_GATE = OFF_MQ + MEM_WIDTH
OFF_AB = OFF_GATE + 3 * D_MODEL
IN_PERM_WIDTH = OFF_AB + LANES

MIX_TILE = 512
TOK_TILE = 256
ROW_BLOCK = 256
INVERT_UNROLL = 32
TRASH_TOK = 256
RING = 3
FETCH_UNROLL = 4
VMEM_LIMIT = 56 * 1024 * 1024
NEG_BIG = -1e30


def _dot(a, b):
    return jnp.dot(a, b, preferred_element_type=F32)


def _dot_nt(a, b):
    return lax.dot_general(a, b, (((1,), (1,)), ((), ())), preferred_element_type=F32)


def _dot_tn(a, b):
    return lax.dot_general(a, b, (((0,), (0,)), ((), ())), preferred_element_type=F32)


def _sigmoid(x):
    return 1.0 / (1.0 + jnp.exp(-x))


def _softplus(x):
    return jnp.maximum(x, 0.0) + jnp.log1p(jnp.exp(-jnp.abs(x)))


def _for_each(n, body):
    if n == 1:
        body(0)
    else:
        def step(i, carry):
            body(i)
            return carry
        lax.fori_loop(0, n, step, 0)


def _memkv_kernel(mem_ref, nw_ref, w_ref, k_ref, v_ref):
    x = mem_ref[...]
    xn = x * lax.rsqrt(jnp.mean(x * x, axis=-1, keepdims=True) + EPS) * nw_ref[...]
    kv = _dot(xn.astype(BF16), w_ref[...])
    k_ref[...] = kv[:, :MEM_WIDTH]
    v_ref[...] = kv[:, MEM_WIDTH:]


def _memkv(mem2d, norm_w, w_kv_bf16):
    rows = mem2d.shape[0]
    grid = rows // MEM_LEN
    return pl.pallas_call(
        _memkv_kernel,
        grid=(grid,),
        in_specs=[
            pl.BlockSpec((MEM_LEN, D_MODEL), lambda i: (i, 0)),
            pl.BlockSpec((1, D_MODEL), lambda i: (0, 0)),
            pl.BlockSpec((D_MODEL, 2 * MEM_WIDTH), lambda i: (0, 0)),
        ],
        out_specs=[
            pl.BlockSpec((MEM_LEN, MEM_WIDTH), lambda i: (i, 0)),
            pl.BlockSpec((MEM_LEN, MEM_WIDTH), lambda i: (i, 0)),
        ],
        out_shape=[jax.ShapeDtypeStruct((rows, MEM_WIDTH), F32)] * 2,
        name="memkv",
    )(mem2d, norm_w, w_kv_bf16)


def _unit_lower_inverse(a, eye, size):
    inv = eye - a
    power = a
    span = 2
    while span < size:
        power = _dot(power, power)
        inv = _dot(inv, eye + power)
        span *= 2
    return inv


def _mixer_kernel(x_ref, dnc_in, dns_in, scc_in, mk_ref, mv_ref, w_in, w_br, w_o, n1_ref, n2_ref, dcw_ref, scw_ref,
                  alog_ref, dtb_ref, dnw_ref, wr_ref, br_ref, hn_all_ref,
                  h_ref, hn_ref, lg_ref, dnc_out, dns_out, scc_out,
                  xp, scp, q_s, k_s, v_s, gb_s, o_s, mq_s, ysc_s, ymem_s, wv_s, wk_s, qk_s, qd_s, kd_s, cd_s,
                  *, nb, tt, chunk):
    t_idx = pl.program_id(1)
    rows = nb * tt
    n_chunk = tt // chunk

    @pl.when(t_idx == 0)
    def _():
        xp[:, CONV_PAD - (DN_CONV - 1):CONV_PAD, :] = dnc_in[...]
        scp[:, CONV_PAD - (SC_CONV - 1):CONV_PAD, :] = scc_in[...]
        dns_out[...] = dns_in[...]

    x = x_ref[...]
    xn = (x * lax.rsqrt(jnp.mean(x * x, axis=-1, keepdims=True) + EPS) * n1_ref[...]).astype(BF16)

    def proj(off, width):
        return _dot(xn, w_in[:, off:off + width])

    qkv_pre = proj(OFF_QKV, QKV_WIDTH)
    for s in range(nb):
        xp[s, CONV_PAD:CONV_PAD + tt, :] = qkv_pre[s * tt:(s + 1) * tt, :]
    sc = proj(OFF_SC, 3 * SC_WIDTH)
    sc_b = sc[:, :SC_WIDTH]
    sc_ch = sc[:, SC_WIDTH:2 * SC_WIDTH] * sc[:, 2 * SC_WIDTH:]
    for s in range(nb):
        scp[s, CONV_PAD:CONV_PAD + tt, :] = sc_ch[s * tt:(s + 1) * tt, :]
    mq_s[...] = proj(OFF_MQ, MEM_WIDTH)

    ab = proj(OFF_AB, LANES)
    lane = lax.broadcasted_iota(jnp.int32, (rows, LANES), 1)
    g_log = -jnp.exp(alog_ref[...]) * _softplus(ab + dtb_ref[...])
    gb_s[...] = jnp.where(lane < DN_HEADS, g_log, _sigmoid(ab))

    def conv_seq(s):
        base = CONV_PAD - (DN_CONV - 1)
        acc = dcw_ref[0:1, :] * xp[s, pl.ds(base, tt), :]
        for j in range(1, DN_CONV):
            acc = acc + dcw_ref[j:j + 1, :] * xp[s, pl.ds(base + j, tt), :]
        act = acc * _sigmoid(acc)
        r0 = pl.multiple_of(s * tt, tt)
        for hd in range(DN_HEADS):
            lo = hd * DN_HEAD_DIM
            qh = act[:, lo:lo + DN_HEAD_DIM]
            kh = act[:, DN_WIDTH + lo:DN_WIDTH + lo + DN_HEAD_DIM]
            q_s[pl.ds(r0, tt), lo:lo + DN_HEAD_DIM] = (
                qh * lax.rsqrt(jnp.sum(qh * qh, axis=-1, keepdims=True) + EPS) * (DN_HEAD_DIM ** -0.5))
            k_s[pl.ds(r0, tt), lo:lo + DN_HEAD_DIM] = (
                kh * lax.rsqrt(jnp.sum(kh * kh, axis=-1, keepdims=True) + EPS))
        v_s[pl.ds(r0, tt), :] = act[:, 2 * DN_WIDTH:]
        tail = xp[s, pl.ds(tt + base, DN_CONV - 1), :]
        dnc_out[s] = tail
        xp[s, pl.ds(base, DN_CONV - 1), :] = tail

        base2 = CONV_PAD - (SC_CONV - 1)
        acc2 = scw_ref[0:1, :] * scp[s, pl.ds(base2, tt), :]
        for j in range(1, SC_CONV):
            acc2 = acc2 + scw_ref[j:j + 1, :] * scp[s, pl.ds(base2 + j, tt), :]
        ysc_s[pl.ds(r0, tt), :] = acc2
        tail2 = scp[s, pl.ds(tt + base2, SC_CONV - 1), :]
        scc_out[s] = tail2
        scp[s, pl.ds(base2, SC_CONV - 1), :] = tail2

    _for_each(nb, conv_seq)

    hl = DN_HEADS * chunk
    shift = chunk.bit_length() - 1
    ri = lax.broadcasted_iota(jnp.int32, (hl, hl), 0)
    ci = lax.broadcasted_iota(jnp.int32, (hl, hl), 1)
    same_head = lax.shift_right_logical(ri, shift) == lax.shift_right_logical(ci, shift)
    causal = same_head & (ri >= ci)
    strict = same_head & (ri > ci)
    eye = (ri == ci).astype(F32)
    ri1 = lax.broadcasted_iota(jnp.int32, (chunk, chunk), 0)
    ci1 = lax.broadcasted_iota(jnp.int32, (chunk, chunk), 1)
    tril = (ri1 >= ci1).astype(F32)
    triu = (ri1 <= ci1).astype(F32)

    def stack_heads(ref, r0):
        return jnp.concatenate(
            [ref[r0:r0 + chunk, hd * DN_HEAD_DIM:(hd + 1) * DN_HEAD_DIM] for hd in range(DN_HEADS)], axis=0)

    for c in range(nb * n_chunk):
        r0 = c * chunk
        gb = gb_s[r0:r0 + chunk, :]
        cum = _dot(tril, gb)
        cum_t = _dot_tn(gb, triu)
        cc = jnp.concatenate([cum[:, hd:hd + 1] for hd in range(DN_HEADS)], axis=0)
        cr = jnp.concatenate([cum_t[hd:hd + 1, :] for hd in range(DN_HEADS)], axis=1)
        beta = jnp.concatenate([gb[:, DN_HEADS + hd:DN_HEADS + hd + 1] for hd in range(DN_HEADS)], axis=0)
        c_last = jnp.concatenate(
            [jnp.broadcast_to(cum[chunk - 1:chunk, hd:hd + 1], (chunk, 1)) for hd in range(DN_HEADS)], axis=0)
        qst, kst, vst = stack_heads(q_s, r0), stack_heads(k_s, r0), stack_heads(v_s, r0)
        decay = jnp.where(causal, jnp.exp(jnp.where(causal, cc - cr, 0.0)), 0.0)
        a_mat = jnp.where(strict, decay * _dot_nt(kst, kst), 0.0) * beta
        t_inv = _unit_lower_inverse(a_mat, eye, chunk)
        e_cum = jnp.exp(cc)
        w = _dot(t_inv, jnp.concatenate([beta * vst, (beta * e_cum) * kst], axis=1))
        wv_s[c] = w[:, :DN_HEAD_DIM]
        wk_s[c] = w[:, DN_HEAD_DIM:]
        qk_s[c] = _dot_nt(qst, kst) * decay
        qd_s[c] = e_cum * qst
        kd_s[c] = jnp.exp(c_last - cc) * kst
        cd_s[c] = jnp.broadcast_to(jnp.exp(c_last), (hl, DN_HEAD_DIM))

    for c in range(nb * n_chunk):
        s = c // n_chunk
        r0 = c * chunk
        states, us = [], []
        for hd in range(DN_HEADS):
            hr = slice(hd * chunk, (hd + 1) * chunk)
            state = dns_out[s, hd]
            states.append(state)
            us.append(wv_s[c, hr, :] - _dot(wk_s[c, hr, :], state))
        o_intra = _dot(qk_s[c], jnp.concatenate(us, axis=0))
        for hd in range(DN_HEADS):
            hr = slice(hd * chunk, (hd + 1) * chunk)
            lo = hd * DN_HEAD_DIM
            o_s[r0:r0 + chunk, lo:lo + DN_HEAD_DIM] = _dot(qd_s[c, hr, :], states[hd]) + o_intra[hr, :]
            dns_out[s, hd] = (cd_s[c, hd * chunk:hd * chunk + 1, :] * states[hd]
                              + _dot_tn(kd_s[c, hr, :], us[hd]))

    def attn_seq(s):
        r0 = pl.multiple_of(s * tt, tt)
        mq = mq_s[pl.ds(r0, tt), :]
        for hd in range(MEM_HEADS):
            lo = hd * MEM_HEAD_DIM
            qh = mq[:, lo:lo + MEM_HEAD_DIM].astype(BF16)
            kh = mk_ref[s, :, lo:lo + MEM_HEAD_DIM].astype(BF16)
            vh = mv_ref[s, :, lo:lo + MEM_HEAD_DIM].astype(BF16)
            sc_h = _dot_nt(qh, kh) * (MEM_HEAD_DIM ** -0.5)
            p = jnp.exp(sc_h - jnp.max(sc_h, axis=-1, keepdims=True))
            denom = jnp.sum(p, axis=-1, keepdims=True)
            ymem_s[pl.ds(r0, tt), lo:lo + MEM_HEAD_DIM] = _dot(p.astype(BF16), vh) / denom

    _for_each(nb, attn_seq)

    dn_gate = proj(OFF_DNG, DN_WIDTH)
    o_all = o_s[...]
    y_heads = []
    for hd in range(DN_HEADS):
        lo = hd * DN_HEAD_DIM
        oh = o_all[:, lo:lo + DN_HEAD_DIM]
        oh = oh * lax.rsqrt(jnp.mean(oh * oh, axis=-1, keepdims=True) + EPS) * dnw_ref[...]
        gh = dn_gate[:, lo:lo + DN_HEAD_DIM]
        y_heads.append(oh * (gh * _sigmoid(gh)))
    y_dn = jnp.concatenate(y_heads, axis=-1).astype(BF16)
    y_sc = (sc_b * ysc_s[...]).astype(BF16)
    y_mem = ymem_s[...].astype(BF16)

    merged = _sigmoid(proj(OFF_GATE, D_MODEL)) * _dot(y_dn, w_br[0:DN_WIDTH, :])
    merged = merged + _sigmoid(proj(OFF_GATE + D_MODEL, D_MODEL)) * _dot(y_sc, w_br[DN_WIDTH:DN_WIDTH + SC_WIDTH, :])
    merged = merged + _sigmoid(proj(OFF_GATE + 2 * D_MODEL, D_MODEL)) * _dot(y_mem, w_br[DN_WIDTH + SC_WIDTH:, :])
    h = x + _dot(merged.astype(BF16), w_o[...])
    h_ref[...] = h
    hn = h * lax.rsqrt(jnp.mean(h * h, axis=-1, keepdims=True) + EPS) * n2_ref[...]
    for c in range(ROW_TILES):
        hn_ref[pl.ds(c, rows, stride=ROW_TILES), :] = hn[:, c * LANES:(c + 1) * LANES]
    lg_ref[...] = _dot(hn, wr_ref[...]) + br_ref[...]


def _mixer(x2d, dnc_in, dns_in, scc_in, mk, mv, weights, hn_all, *, n_seq, seq_len, nb, tt, row0):
    chunk = CHUNK if seq_len % CHUNK == 0 else seq_len
    rows = nb * tt
    n_t = seq_len // tt
    total_rows = n_seq * seq_len
    n_chunks = rows // chunk
    hl = DN_HEADS * chunk
    grid = (n_seq // nb, n_t)
    const = lambda b, t: (0, 0)
    seq3 = lambda b, t: (b, 0, 0)
    assert row0 % rows == 0

    def tok(b, t):
        return (b * n_t + t, 0)

    def tok_all(b, t):
        return (row0 // rows + b * n_t + t, 0)

    (w_in, w_br, w_o, n1, n2, dcw, scw, alog, dtb, dnw, wr, br) = weights
    in_specs = [
        pl.BlockSpec((rows, D_MODEL), tok),
        pl.BlockSpec((nb, DN_CONV - 1, QKV_WIDTH), seq3),
        pl.BlockSpec((nb, DN_HEADS, DN_HEAD_DIM, DN_HEAD_DIM), lambda b, t: (b, 0, 0, 0)),
        pl.BlockSpec((nb, SC_CONV - 1, SC_WIDTH), seq3),
        pl.BlockSpec((nb, MEM_LEN, MEM_WIDTH), seq3),
        pl.BlockSpec((nb, MEM_LEN, MEM_WIDTH), seq3),
        pl.BlockSpec(w_in.shape, const, pipeline_mode=pl.Buffered(1)),
        pl.BlockSpec(w_br.shape, const, pipeline_mode=pl.Buffered(1)),
        pl.BlockSpec(w_o.shape, const, pipeline_mode=pl.Buffered(1)),
        pl.BlockSpec(n1.shape, const),
        pl.BlockSpec(n2.shape, const),
        pl.BlockSpec(dcw.shape, const),
        pl.BlockSpec(scw.shape, const),
        pl.BlockSpec(alog.shape, const),
        pl.BlockSpec(dtb.shape, const),
        pl.BlockSpec(dnw.shape, const),
        pl.BlockSpec(wr.shape, const),
        pl.BlockSpec(br.shape, const),
        pl.BlockSpec(memory_space=pl.ANY),
    ]
    out_shape = [
        jax.ShapeDtypeStruct((total_rows, D_MODEL), F32),
        jax.ShapeDtypeStruct(hn_all.shape, F32),
        jax.ShapeDtypeStruct((total_rows, LANES), F32),
        jax.ShapeDtypeStruct((n_seq, DN_CONV - 1, QKV_WIDTH), F32),
        jax.ShapeDtypeStruct((n_seq, DN_HEADS, DN_HEAD_DIM, DN_HEAD_DIM), F32),
        jax.ShapeDtypeStruct((n_seq, SC_CONV - 1, SC_WIDTH), F32),
    ]
    out_specs = [
        pl.BlockSpec((rows, D_MODEL), tok),
        pl.BlockSpec((rows * ROW_TILES, LANES), tok_all),
        pl.BlockSpec((rows, LANES), tok),
        pl.BlockSpec((nb, DN_CONV - 1, QKV_WIDTH), seq3),
        pl.BlockSpec((nb, DN_HEADS, DN_HEAD_DIM, DN_HEAD_DIM), lambda b, t: (b, 0, 0, 0)),
        pl.BlockSpec((nb, SC_CONV - 1, SC_WIDTH), seq3),
    ]
    args = [x2d, dnc_in, dns_in, scc_in, mk, mv, w_in, w_br, w_o, n1, n2, dcw, scw, alog, dtb, dnw, wr, br, hn_all]
    scratch = [
        pltpu.VMEM((nb, CONV_PAD + tt, QKV_WIDTH), F32),
        pltpu.VMEM((nb, CONV_PAD + tt, SC_WIDTH), F32),
        pltpu.VMEM((rows, DN_WIDTH), F32),
        pltpu.VMEM((rows, DN_WIDTH), F32),
        pltpu.VMEM((rows, DN_WIDTH), F32),
        pltpu.VMEM((rows, LANES), F32),
        pltpu.VMEM((rows, DN_WIDTH), F32),
        pltpu.VMEM((rows, MEM_WIDTH), F32),
        pltpu.VMEM((rows, SC_WIDTH), F32),
        pltpu.VMEM((rows, MEM_WIDTH), F32),
        pltpu.VMEM((n_chunks, hl, DN_HEAD_DIM), F32),
        pltpu.VMEM((n_chunks, hl, DN_HEAD_DIM), F32),
        pltpu.VMEM((n_chunks, hl, hl), F32),
        pltpu.VMEM((n_chunks, hl, DN_HEAD_DIM), F32),
        pltpu.VMEM((n_chunks, hl, DN_HEAD_DIM), F32),
        pltpu.VMEM((n_chunks, hl, DN_HEAD_DIM), F32),
    ]
    return pl.pallas_call(
        functools.partial(_mixer_kernel, nb=nb, tt=tt, chunk=chunk),
        grid=grid,
        in_specs=in_specs,
        out_specs=out_specs,
        out_shape=out_shape,
        scratch_shapes=scratch,
        input_output_aliases={len(args) - 1: 1},
        compiler_params=pltpu.CompilerParams(
            dimension_semantics=("arbitrary", "arbitrary"), vmem_limit_bytes=VMEM_LIMIT),
        name="mixer",
    )(*args)


def _router_kernel(lg_ref, idx_ref, gate_ref, rank_ref, cnt_ref, carry):
    i = pl.program_id(0)

    @pl.when(i == 0)
    def _():
        carry[...] = jnp.zeros_like(carry)

    work = lg_ref[...]
    tm = work.shape[0]
    lane = lax.broadcasted_iota(jnp.int32, (tm, LANES), 1).astype(F32)
    idxs, vals = [], []
    for _ in range(TOP_K):
        m = jnp.max(work, axis=-1, keepdims=True)
        ik = jnp.min(jnp.where(work == m, lane, float(LANES)), axis=-1, keepdims=True)
        idxs.append(ik)
        vals.append(m)
        work = jnp.where(lane == ik, -jnp.inf, work)
    exps = [jnp.exp(v - vals[0]) for v in vals]
    denom = exps[0] + exps[1] + exps[2] + exps[3]
    hot = jnp.zeros((tm, LANES), F32)
    for ik in idxs:
        hot = hot + (lane == ik).astype(F32)
    ri = lax.broadcasted_iota(jnp.int32, (tm, tm), 0)
    ci = lax.broadcasted_iota(jnp.int32, (tm, tm), 1)
    before = (ri > ci).astype(BF16)
    prefix = _dot(before, hot.astype(BF16)) + carry[...]
    idx_out = jnp.zeros((tm, LANES), F32)
    gate_out = jnp.zeros((tm, LANES), F32)
    rank_out = jnp.zeros((tm, LANES), F32)
    for k in range(TOP_K):
        rk = jnp.sum(jnp.where(lane == idxs[k], prefix, 0.0), axis=-1, keepdims=True)
        idx_out = jnp.where(lane == k, idxs[k], idx_out)
        gate_out = jnp.where(lane == k, exps[k] / denom, gate_out)
        rank_out = jnp.where(lane == k, rk, rank_out)
    idx_ref[...] = idx_out.astype(jnp.int32)
    gate_ref[...] = gate_out
    rank_ref[...] = rank_out.astype(jnp.int32)
    carry[...] = carry[...] + jnp.sum(hot, axis=0, keepdims=True)
    cnt_ref[...] = carry[...].astype(jnp.int32)


def _router(logits):
    n_tok = logits.shape[0]
    tile = lambda i: (i, 0)
    return pl.pallas_call(
        _router_kernel,
        grid=(n_tok // TOK_TILE,),
        in_specs=[pl.BlockSpec((TOK_TILE, LANES), tile)],
        out_specs=[pl.BlockSpec((TOK_TILE, LANES), tile)] * 3 + [pl.BlockSpec((1, LANES), lambda i: (0, 0))],
        out_shape=[
            jax.ShapeDtypeStruct((n_tok, LANES), jnp.int32),
            jax.ShapeDtypeStruct((n_tok, LANES), F32),
            jax.ShapeDtypeStruct((n_tok, LANES), jnp.int32),
            jax.ShapeDtypeStruct((1, LANES), jnp.int32),
        ],
        scratch_shapes=[pltpu.VMEM((1, LANES), F32)],
        compiler_params=pltpu.CompilerParams(dimension_semantics=("arbitrary",)),
        name="router",
    )(logits)


def _invert_kernel(dest_ref, code_ref, default_ref, inv_ref, sem):
    i = pl.program_id(0)

    @pl.when(i == 0)
    def _():
        cp = pltpu.make_async_copy(default_ref, inv_ref, sem)
        cp.start()
        cp.wait()

    def body(j, c):
        for d in range(INVERT_UNROLL):
            q = j * INVERT_UNROLL + d
            inv_ref[dest_ref[q]] = code_ref[q]
        return c

    lax.fori_loop(0, TOK_TILE * TOP_K // INVERT_UNROLL, body, 0)


def _invert(dest_flat, pair_code, default_inv):
    n_pairs = TOK_TILE * TOP_K
    return pl.pallas_call(
        _invert_kernel,
        grid=(dest_flat.shape[0] // n_pairs,),
        in_specs=[
            pl.BlockSpec((n_pairs,), lambda i: (i,), memory_space=pltpu.SMEM),
            pl.BlockSpec((n_pairs,), lambda i: (i,), memory_space=pltpu.SMEM),
            pl.BlockSpec(memory_space=pl.ANY),
        ],
        out_specs=pl.BlockSpec(memory_space=pltpu.SMEM),
        out_shape=jax.ShapeDtypeStruct(default_inv.shape, jnp.int32),
        scratch_shapes=[pltpu.SemaphoreType.DMA(())],
        compiler_params=pltpu.CompilerParams(dimension_semantics=("arbitrary",)),
        name="invert",
    )(dest_flat, pair_code, default_inv)


def _tile_rows(row, n=1):
    return pl.ds(pl.multiple_of(row * ROW_TILES, ROW_TILES), n * ROW_TILES)


def _experts_kernel(blk_e, n_valid, inv, hn_ref, wg_ref, bg_ref, wu_ref, bu_ref, wd_ref, bd_ref, ys_ref,
                    wg_bf, wu_bf, wd_bf, xbuf0, xbuf1, xbuf2, gsem):
    i = pl.program_id(0)
    nv = n_valid[0]
    phase = lax.rem(i, RING)
    xbuf = (xbuf0, xbuf1, xbuf2)

    def gather(block, s):
        base = block * ROW_BLOCK
        for r in range(ROW_BLOCK):
            pltpu.make_async_copy(
                hn_ref.at[_tile_rows(inv[base + r])], xbuf[s].at[_tile_rows(r)], gsem.at[s]).start(priority=r % 2)

    def wait_gather(s):
        pltpu.make_async_copy(hn_ref.at[_tile_rows(0, ROW_BLOCK)], xbuf[s], gsem.at[s]).wait()

    @pl.when(i == 0)
    def _():
        gather(0, 0)
        gather(jnp.minimum(1, nv - 1), 1)

    e = blk_e[i]
    prev = blk_e[jnp.maximum(i - 1, 0)]

    @pl.when(((i == 0) | (e != prev)) & (i < nv))
    def _():
        wg_bf[...] = wg_ref[0].astype(BF16)
        wu_bf[...] = wu_ref[0].astype(BF16)
        wd_bf[...] = wd_ref[0].astype(BF16)

    def step(slot):
        gather(jnp.minimum(i + 2, nv - 1), (slot + 2) % RING)
        wait_gather(slot)
        x = jnp.concatenate(
            [xbuf[slot][pl.ds(c, ROW_BLOCK, stride=ROW_TILES), :] for c in range(ROW_TILES)], axis=1).astype(BF16)
        gl = jnp.minimum(_dot(x, wg_bf[...]) + bg_ref[0], SWIGLU_LIMIT)
        ul = jnp.clip(_dot(x, wu_bf[...]) + bu_ref[0], -SWIGLU_LIMIT, SWIGLU_LIMIT)
        act = (ul + 1.0) * (gl * _sigmoid(SWIGLU_ALPHA * gl))
        y = _dot(act.astype(BF16), wd_bf[...]) + bd_ref[0]
        for c in range(ROW_TILES):
            ys_ref[pl.ds(c, ROW_BLOCK, stride=ROW_TILES), :] = y[:, c * LANES:(c + 1) * LANES]

    for slot in range(RING):
        mine = (phase == slot) & (i < nv)

        @pl.when(mine)
        def _():
            step(slot)

        @pl.when(mine & (i == nv - 1))
        def _():
            wait_gather((slot + 1) % RING)
            wait_gather((slot + 2) % RING)

    @pl.when(i >= nv)
    def _():
        ys_ref[...] = jnp.zeros_like(ys_ref)


def _experts(blk_e, n_valid, inv, hn_all, w_gate, b_gate, w_up, b_up, w_down, b_down):
    n_blocks = blk_e.shape[0]
    wsel = lambda i, be, nv, iv: (be[i], 0, 0)
    d_ff = w_gate.shape[-1]
    grid_spec = pltpu.PrefetchScalarGridSpec(
        num_scalar_prefetch=3,
        grid=(n_blocks,),
        in_specs=[
            pl.BlockSpec(memory_space=pl.ANY),
            pl.BlockSpec((1, D_MODEL, d_ff), wsel),
            pl.BlockSpec((1, 1, d_ff), wsel),
            pl.BlockSpec((1, D_MODEL, d_ff), wsel),
            pl.BlockSpec((1, 1, d_ff), wsel),
            pl.BlockSpec((1, d_ff, D_MODEL), wsel),
            pl.BlockSpec((1, 1, D_MODEL), wsel),
        ],
        out_specs=pl.BlockSpec((ROW_BLOCK * ROW_TILES, LANES), lambda i, be, nv, iv: (i, 0)),
        scratch_shapes=[
            pltpu.VMEM((D_MODEL, d_ff), BF16),
            pltpu.VMEM((D_MODEL, d_ff), BF16),
            pltpu.VMEM((d_ff, D_MODEL), BF16),
        ] + [pltpu.VMEM((ROW_BLOCK * ROW_TILES, LANES), F32)] * RING + [
            pltpu.SemaphoreType.DMA((RING,)),
        ],
    )
    return pl.pallas_call(
        _experts_kernel,
        grid_spec=grid_spec,
        out_shape=jax.ShapeDtypeStruct((n_blocks * ROW_BLOCK * ROW_TILES, LANES), F32),
        compiler_params=pltpu.CompilerParams(
            dimension_semantics=("arbitrary",), vmem_limit_bytes=VMEM_LIMIT),
        name="experts",
    )(blk_e, n_valid, inv, hn_all, w_gate, b_gate, w_up, b_up, w_down, b_down)


def _combine_kernel(dest_ref, dest_next_ref, gate_ref, h_ref, fw_ref, ys_ref, y_ref, buf0, buf1, sem, *, n_tiles):
    i = pl.program_id(0)
    parity = lax.rem(i, 2)
    bufs = (buf0, buf1)

    def fetch(slots_ref, s):
        def body(j, c):
            for dt in range(FETCH_UNROLL):
                t = j * FETCH_UNROLL + dt
                for k in range(TOP_K):
                    pltpu.make_async_copy(
                        ys_ref.at[_tile_rows(slots_ref[t * TOP_K + k])], bufs[s].at[k, _tile_rows(t)],
                        sem.at[s]).start(priority=k % 2)
            return c

        lax.fori_loop(0, TOK_TILE // FETCH_UNROLL, body, 0)

    @pl.when(i == 0)
    def _():
        fetch(dest_ref, 0)

    for s in range(2):
        @pl.when(parity == s)
        def _():
            @pl.when(i + 1 < n_tiles)
            def _():
                fetch(dest_next_ref, 1 - s)

            for k in range(TOP_K):
                pltpu.make_async_copy(ys_ref.at[_tile_rows(0, TOK_TILE)], bufs[s].at[k], sem.at[s]).wait()
            gate = gate_ref[...]
            acc = h_ref[...]
            for k in range(TOP_K):
                y_k = jnp.concatenate(
                    [bufs[s][k, pl.ds(c, TOK_TILE, stride=ROW_TILES), :] for c in range(ROW_TILES)], axis=1)
                acc = acc + gate[:, k:k + 1] * y_k
            y_ref[...] = acc * lax.rsqrt(jnp.mean(acc * acc, axis=-1, keepdims=True) + EPS) * fw_ref[...]


def _combine(dest_flat, gates, h_group, final_w, ys, *, tile0):
    n_tiles = h_group.shape[0] // TOK_TILE
    n_pairs = TOK_TILE * TOP_K
    last = tile0 + n_tiles - 1
    return pl.pallas_call(
        functools.partial(_combine_kernel, n_tiles=n_tiles),
        grid=(n_tiles,),
        in_specs=[
            pl.BlockSpec((n_pairs,), lambda i: (tile0 + i,), memory_space=pltpu.SMEM),
            pl.BlockSpec((n_pairs,), lambda i: (jnp.minimum(tile0 + i + 1, last),), memory_space=pltpu.SMEM),
            pl.BlockSpec((TOK_TILE, LANES), lambda i: (tile0 + i, 0)),
            pl.BlockSpec((TOK_TILE, D_MODEL), lambda i: (i, 0)),
            pl.BlockSpec((1, D_MODEL), lambda i: (0, 0)),
            pl.BlockSpec(memory_space=pl.ANY),
        ],
        out_specs=pl.BlockSpec((TOK_TILE, D_MODEL), lambda i: (i, 0)),
        out_shape=jax.ShapeDtypeStruct((n_tiles * TOK_TILE, D_MODEL), F32),
        scratch_shapes=[
            pltpu.VMEM((TOP_K, TOK_TILE * ROW_TILES, LANES), F32),
            pltpu.VMEM((TOP_K, TOK_TILE * ROW_TILES, LANES), F32),
            pltpu.SemaphoreType.DMA((2,)),
        ],
        compiler_params=pltpu.CompilerParams(dimension_semantics=("arbitrary",)),
        name="combine",
    )(dest_flat, dest_flat, gates, h_group, final_w, ys)


def _permute_w_in(w):
    o_a = QKV_WIDTH
    o_g = o_a + 2 * DN_HEADS
    o_rest = o_g + DN_WIDTH
    pad = jnp.zeros((w.shape[0], LANES - 2 * DN_HEADS), BF16)
    pieces = [w[:, :o_a], w[:, o_g:o_rest], w[:, o_rest:], w[:, o_a:o_g]]
    return jnp.concatenate([p.astype(BF16) for p in pieces] + [pad], axis=1)


def _lane_row(v, fill=0.0):
    return jnp.concatenate([v.astype(F32), jnp.full((LANES - v.shape[0],), fill, F32)]).reshape(1, LANES)


def kernel(x_prompt, x_sample, mem_prompt, state_dn, state_dn_conv, state_sc_conv, cache_mem_k, cache_mem_v, w_in, dn_conv_w, dn_A_log, dn_dt_bias, dn_norm_w, sc_conv_w, mem_norm_w, w_mem_kv, w_br, w_o, norm1_w, norm2_w, w_router, b_router, w_gate, b_gate, w_up, b_up, w_down, b_down, final_norm_w):
    assert w_in.shape[0] == 1, "one layer"
    bp, tp, _ = x_prompt.shape
    bs, ts, _ = x_sample.shape
    n_p, n_s = bp * tp, bs * ts
    n_tok = n_p + n_s
    assert n_p % TOK_TILE == 0 and n_s % TOK_TILE == 0 and tp % MIX_TILE == 0

    weights = (
        _permute_w_in(w_in[0]),
        w_br[0].astype(BF16),
        w_o[0].astype(BF16),
        norm1_w[0].reshape(1, D_MODEL),
        norm2_w[0].reshape(1, D_MODEL),
        dn_conv_w[0],
        sc_conv_w[0],
        _lane_row(dn_A_log[0]),
        _lane_row(dn_dt_bias[0]),
        dn_norm_w[0].reshape(1, DN_HEAD_DIM),
        jnp.concatenate([w_router[0], jnp.zeros((D_MODEL, LANES - N_EXPERTS), F32)], axis=1),
        _lane_row(b_router[0], NEG_BIG),
    )

    mk2d, mv2d = _memkv(mem_prompt.reshape(bp * MEM_LEN, D_MODEL), mem_norm_w[0].reshape(1, D_MODEL),
                        w_mem_kv[0].astype(BF16))

    assert (n_tok + TRASH_TOK) % MIX_TILE == 0
    hn_all = jnp.zeros(((n_tok + TRASH_TOK) * ROW_TILES, LANES), F32)
    h_s, hn_all, lg_s, s_dnc, s_dns, s_scc = _mixer(
        x_sample.reshape(n_s, D_MODEL), state_dn_conv[0], state_dn[0], state_sc_conv[0],
        cache_mem_k[0].reshape(bs, MEM_LEN, MEM_WIDTH), cache_mem_v[0].reshape(bs, MEM_LEN, MEM_WIDTH),
        weights, hn_all, n_seq=bs, seq_len=ts, nb=bs, tt=ts, row0=n_p)
    h_p, hn_all, lg_p, p_dnc, p_dns, p_scc = _mixer(
        x_prompt.reshape(n_p, D_MODEL),
        jnp.zeros((bp, DN_CONV - 1, QKV_WIDTH), F32),
        jnp.zeros((bp, DN_HEADS, DN_HEAD_DIM, DN_HEAD_DIM), F32),
        jnp.zeros((bp, SC_CONV - 1, SC_WIDTH), F32),
        mk2d.reshape(bp, MEM_LEN, MEM_WIDTH), mv2d.reshape(bp, MEM_LEN, MEM_WIDTH),
        weights, hn_all, n_seq=bp, seq_len=tp, nb=1, tt=MIX_TILE, row0=0)

    idx, gates, rank, counts = _router(jnp.concatenate([lg_p, lg_s], axis=0))
    counts = counts[0, :N_EXPERTS]
    n_blk_e = (counts + ROW_BLOCK - 1) // ROW_BLOCK
    blk_end = jnp.cumsum(n_blk_e)
    blk_start = blk_end - n_blk_e
    row_start = blk_start * ROW_BLOCK
    experts = jnp.arange(N_EXPERTS, dtype=jnp.int32)
    start_of = jnp.sum(jnp.where(idx[:, :TOP_K, None] == experts, row_start, 0), axis=-1)
    dest = (start_of + rank[:, :TOP_K]).reshape(n_tok * TOP_K)

    n_blocks = (n_tok * TOP_K) // ROW_BLOCK + N_EXPERTS
    bi = jnp.arange(n_blocks, dtype=jnp.int32)
    n_valid = blk_end[-1].astype(jnp.int32)
    bclip = jnp.minimum(bi, n_valid - 1)
    blk_e = jnp.sum((bclip[:, None] >= blk_end[None, :]).astype(jnp.int32), axis=1)
    n_valid = n_valid.reshape(1)
    dest = dest.astype(jnp.int32)

    slots = jnp.arange(n_blocks * ROW_BLOCK, dtype=jnp.int32)
    default_inv = n_tok + (slots & (TRASH_TOK - 1))
    pair_tok = jnp.arange(n_tok * TOP_K, dtype=jnp.int32) // TOP_K
    inv = _invert(dest, pair_tok, default_inv)

    ys = _experts(blk_e, n_valid, inv, hn_all,
                  w_gate[0], b_gate[0].reshape(N_EXPERTS, 1, -1), w_up[0], b_up[0].reshape(N_EXPERTS, 1, -1),
                  w_down[0], b_down[0].reshape(N_EXPERTS, 1, -1))
    fw = final_norm_w.reshape(1, D_MODEL)
    y_p = _combine(dest, gates, h_p, fw, ys, tile0=0)
    y_s = _combine(dest, gates, h_s, fw, ys, tile0=n_p // TOK_TILE)

    return (y_p.reshape(bp, tp, D_MODEL), y_s.reshape(bs, ts, D_MODEL),
            p_dns[None], p_dnc[None], p_scc[None],
            mk2d.reshape(1, bp, MEM_LEN, MEM_HEADS, MEM_HEAD_DIM), mv2d.reshape(1, bp, MEM_LEN, MEM_HEADS, MEM_HEAD_DIM),
            s_dns[None], s_dnc[None], s_scc[None])
```

```python
import functools

import jax
import jax.numpy as jnp
from jax import lax
from jax.experimental import pallas as pl
from jax.experimental.pallas import tpu as pltpu

F32 = jnp.float32
BF16 = jnp.bfloat16

D_MODEL = 1024
CHUNK = 64
EPS = 1e-6
DN_HEADS = 4
DN_HEAD_DIM = 128
DN_WIDTH = DN_HEADS * DN_HEAD_DIM
QKV_WIDTH = 3 * DN_WIDTH
DN_CONV = 4
SC_WIDTH = 256
SC_CONV = 3
MEM_LEN = 256
MEM_HEADS = 4
MEM_HEAD_DIM = 64
MEM_WIDTH = MEM_HEADS * MEM_HEAD_DIM
N_EXPERTS = 32
TOP_K = 4
SWIGLU_ALPHA = 1.702
SWIGLU_LIMIT = 7.0

LANES = 128
SUBLANES = 8
ROW_TILES = D_MODEL // LANES
assert ROW_TILES == SUBLANES
CONV_PAD = 8

OFF_QKV = 0
OFF_DNG = OFF_QKV + QKV_WIDTH
OFF_SC = OFF_DNG + DN_WIDTH
OFF_MQ = OFF_SC + 3 * SC_WIDTH
OFF_GATE = OFF_MQ + MEM_WIDTH
OFF_AB = OFF_GATE + 3 * D_MODEL
IN_PERM_WIDTH = OFF_AB + LANES

MIX_TILE = 512
TOK_TILE = 256
ROW_BLOCK = 256
INVERT_UNROLL = 32
TRASH_TOK = 256
RING = 3
PAIR_TOK_BITS = 16
PAIR_TOK_MASK = (1 << PAIR_TOK_BITS) - 1
VMEM_LIMIT = 56 * 1024 * 1024
NEG_BIG = -1e30


def _dot(a, b):
    return jnp.dot(a, b, preferred_element_type=F32)


def _dot_nt(a, b):
    return lax.dot_general(a, b, (((1,), (1,)), ((), ())), preferred_element_type=F32)


def _dot_tn(a, b):
    return lax.dot_general(a, b, (((0,), (0,)), ((), ())), preferred_element_type=F32)


def _sigmoid(x):
    return 1.0 / (1.0 + jnp.exp(-x))


def _softplus(x):
    return jnp.maximum(x, 0.0) + jnp.log1p(jnp.exp(-jnp.abs(x)))


def _for_each(n, body):
    if n == 1:
        body(0)
    else:
        def step(i, carry):
            body(i)
            return carry
        lax.fori_loop(0, n, step, 0)


def _memkv_kernel(mem_ref, nw_ref, w_ref, k_ref, v_ref):
    x = mem_ref[...]
    xn = x * lax.rsqrt(jnp.mean(x * x, axis=-1, keepdims=True) + EPS) * nw_ref[...]
    kv = _dot(xn.astype(BF16), w_ref[...])
    k_ref[...] = kv[:, :MEM_WIDTH]
    v_ref[...] = kv[:, MEM_WIDTH:]


def _memkv(mem2d, norm_w, w_kv_bf16):
    rows = mem2d.shape[0]
    grid = rows // MEM_LEN
    return pl.pallas_call(
        _memkv_kernel,
        grid=(grid,),
        in_specs=[
            pl.BlockSpec((MEM_LEN, D_MODEL), lambda i: (i, 0)),
            pl.BlockSpec((1, D_MODEL), lambda i: (0, 0)),
            pl.BlockSpec((D_MODEL, 2 * MEM_WIDTH), lambda i: (0, 0)),
        ],
        out_specs=[
            pl.BlockSpec((MEM_LEN, MEM_WIDTH), lambda i: (i, 0)),
            pl.BlockSpec((MEM_LEN, MEM_WIDTH), lambda i: (i, 0)),
        ],
        out_shape=[jax.ShapeDtypeStruct((rows, MEM_WIDTH), F32)] * 2,
        name="memkv",
    )(mem2d, norm_w, w_kv_bf16)


def _unit_lower_inverse(a, eye, size):
    inv = eye - a
    power = a
    span = 2
    while span < size:
        power = _dot(power, power)
        inv = _dot(inv, eye + power)
        span *= 2
    return inv


def _mixer_kernel(x_ref, dnc_in, dns_in, scc_in, mk_ref, mv_ref, w_in, w_br, w_o, n1_ref, n2_ref, dcw_ref, scw_ref,
                  alog_ref, dtb_ref, dnw_ref, wr_ref, br_ref, hn_all_ref,
                  h_ref, hn_ref, lg_ref, dnc_out, dns_out, scc_out,
                  xp, scp, q_s, k_s, v_s, gb_s, o_s, mq_s, ysc_s, ymem_s, wv_s, wk_s, qk_s, qd_s, kd_s, cd_s,
                  *, nb, tt, chunk):
    t_idx = pl.program_id(1)
    rows = nb * tt
    n_chunk = tt // chunk

    @pl.when(t_idx == 0)
    def _():
        xp[:, CONV_PAD - (DN_CONV - 1):CONV_PAD, :] = dnc_in[...]
        scp[:, CONV_PAD - (SC_CONV - 1):CONV_PAD, :] = scc_in[...]
        dns_out[...] = dns_in[...]

    x = x_ref[...]
    xn = (x * lax.rsqrt(jnp.mean(x * x, axis=-1, keepdims=True) + EPS) * n1_ref[...]).astype(BF16)

    def proj(off, width):
        return _dot(xn, w_in[:, off:off + width])

    qkv_pre = proj(OFF_QKV, QKV_WIDTH)
    for s in range(nb):
        xp[s, CONV_PAD:CONV_PAD + tt, :] = qkv_pre[s * tt:(s + 1) * tt, :]
    sc = proj(OFF_SC, 3 * SC_WIDTH)
    sc_b = sc[:, :SC_WIDTH]
    sc_ch = sc[:, SC_WIDTH:2 * SC_WIDTH] * sc[:, 2 * SC_WIDTH:]
    for s in range(nb):
        scp[s, CONV_PAD:CONV_PAD + tt, :] = sc_ch[s * tt:(s + 1) * tt, :]
    mq_s[...] = proj(OFF_MQ, MEM_WIDTH)

    ab = proj(OFF_AB, LANES)
    lane = lax.broadcasted_iota(jnp.int32, (rows, LANES), 1)
    g_log = -jnp.exp(alog_ref[...]) * _softplus(ab + dtb_ref[...])
    gb_s[...] = jnp.where(lane < DN_HEADS, g_log, _sigmoid(ab))

    def conv_seq(s):
        base = CONV_PAD - (DN_CONV - 1)
        acc = dcw_ref[0:1, :] * xp[s, pl.ds(base, tt), :]
        for j in range(1, DN_CONV):
            acc = acc + dcw_ref[j:j + 1, :] * xp[s, pl.ds(base + j, tt), :]
        act = acc * _sigmoid(acc)
        r0 = pl.multiple_of(s * tt, tt)
        for hd in range(DN_HEADS):
            lo = hd * DN_HEAD_DIM
            qh = act[:, lo:lo + DN_HEAD_DIM]
            kh = act[:, DN_WIDTH + lo:DN_WIDTH + lo + DN_HEAD_DIM]
            q_s[pl.ds(r0, tt), lo:lo + DN_HEAD_DIM] = (
                qh * lax.rsqrt(jnp.sum(qh * qh, axis=-1, keepdims=True) + EPS) * (DN_HEAD_DIM ** -0.5))
            k_s[pl.ds(r0, tt), lo:lo + DN_HEAD_DIM] = (
                kh * lax.rsqrt(jnp.sum(kh * kh, axis=-1, keepdims=True) + EPS))
        v_s[pl.ds(r0, tt), :] = act[:, 2 * DN_WIDTH:]
        tail = xp[s, pl.ds(tt + base, DN_CONV - 1), :]
        dnc_out[s] = tail
        xp[s, pl.ds(base, DN_CONV - 1), :] = tail

        base2 = CONV_PAD - (SC_CONV - 1)
        acc2 = scw_ref[0:1, :] * scp[s, pl.ds(base2, tt), :]
        for j in range(1, SC_CONV):
            acc2 = acc2 + scw_ref[j:j + 1, :] * scp[s, pl.ds(base2 + j, tt), :]
        ysc_s[pl.ds(r0, tt), :] = acc2
        tail2 = scp[s, pl.ds(tt + base2, SC_CONV - 1), :]
        scc_out[s] = tail2
        scp[s, pl.ds(base2, SC_CONV - 1), :] = tail2

    _for_each(nb, conv_seq)

    hl = DN_HEADS * chunk
    shift = chunk.bit_length() - 1
    ri = lax.broadcasted_iota(jnp.int32, (hl, hl), 0)
    ci = lax.broadcasted_iota(jnp.int32, (hl, hl), 1)
    same_head = lax.shift_right_logical(ri, shift) == lax.shift_right_logical(ci, shift)
    causal = same_head & (ri >= ci)
    strict = same_head & (ri > ci)
    eye = (ri == ci).astype(F32)
    ri1 = lax.broadcasted_iota(jnp.int32, (chunk, chunk), 0)
    ci1 = lax.broadcasted_iota(jnp.int32, (chunk, chunk), 1)
    tril = (ri1 >= ci1).astype(F32)
    triu = (ri1 <= ci1).astype(F32)

    def stack_heads(ref, r0):
        return jnp.concatenate(
            [ref[r0:r0 + chunk, hd * DN_HEAD_DIM:(hd + 1) * DN_HEAD_DIM] for hd in range(DN_HEADS)], axis=0)

    for c in range(nb * n_chunk):
        r0 = c * chunk
        gb = gb_s[r0:r0 + chunk, :]
        cum = _dot(tril, gb)
        cum_t = _dot_tn(gb, triu)
        cc = jnp.concatenate([cum[:, hd:hd + 1] for hd in range(DN_HEADS)], axis=0)
        cr = jnp.concatenate([cum_t[hd:hd + 1, :] for hd in range(DN_HEADS)], axis=1)
        beta = jnp.concatenate([gb[:, DN_HEADS + hd:DN_HEADS + hd + 1] for hd in range(DN_HEADS)], axis=0)
        c_last = jnp.concatenate(
            [jnp.broadcast_to(cum[chunk - 1:chunk, hd:hd + 1], (chunk, 1)) for hd in range(DN_HEADS)], axis=0)
        qst, kst, vst = stack_heads(q_s, r0), stack_heads(k_s, r0), stack_heads(v_s, r0)
        decay = jnp.where(causal, jnp.exp(jnp.where(causal, cc - cr, 0.0)), 0.0)
        a_mat = jnp.where(strict, decay * _dot_nt(kst, kst), 0.0) * beta
        t_inv = _unit_lower_inverse(a_mat, eye, chunk)
        e_cum = jnp.exp(cc)
        w = _dot(t_inv, jnp.concatenate([beta * vst, (beta * e_cum) * kst], axis=1))
        wv_s[c] = w[:, :DN_HEAD_DIM]
        wk_s[c] = w[:, DN_HEAD_DIM:]
        qk_s[c] = _dot_nt(qst, kst) * decay
        qd_s[c] = e_cum * qst
        kd_s[c] = jnp.exp(c_last - cc) * kst
        cd_s[c] = jnp.broadcast_to(jnp.exp(c_last), (hl, DN_HEAD_DIM))

    for c in range(nb * n_chunk):
        s = c // n_chunk
        r0 = c * chunk
        states, us = [], []
        for hd in range(DN_HEADS):
            hr = slice(hd * chunk, (hd + 1) * chunk)
            state = dns_out[s, hd]
            states.append(state)
            us.append(wv_s[c, hr, :] - _dot(wk_s[c, hr, :], state))
        o_intra = _dot(qk_s[c], jnp.concatenate(us, axis=0))
        for hd in range(DN_HEADS):
            hr = slice(hd * chunk, (hd + 1) * chunk)
            lo = hd * DN_HEAD_DIM
            o_s[r0:r0 + chunk, lo:lo + DN_HEAD_DIM] = _dot(qd_s[c, hr, :], states[hd]) + o_intra[hr, :]
            dns_out[s, hd] = (cd_s[c, hd * chunk:hd * chunk + 1, :] * states[hd]
                              + _dot_tn(kd_s[c, hr, :], us[hd]))

    def attn_seq(s):
        r0 = pl.multiple_of(s * tt, tt)
        mq = mq_s[pl.ds(r0, tt), :]
        for hd in range(MEM_HEADS):
            lo = hd * MEM_HEAD_DIM
            qh = mq[:, lo:lo + MEM_HEAD_DIM].astype(BF16)
            kh = mk_ref[s, :, lo:lo + MEM_HEAD_DIM].astype(BF16)
            vh = mv_ref[s, :, lo:lo + MEM_HEAD_DIM].astype(BF16)
            sc_h = _dot_nt(qh, kh) * (MEM_HEAD_DIM ** -0.5)
            p = jnp.exp(sc_h - jnp.max(sc_h, axis=-1, keepdims=True))
            denom = jnp.sum(p, axis=-1, keepdims=True)
            ymem_s[pl.ds(r0, tt), lo:lo + MEM_HEAD_DIM] = _dot(p.astype(BF16), vh) / denom

    _for_each(nb, attn_seq)

    dn_gate = proj(OFF_DNG, DN_WIDTH)
    o_all = o_s[...]
    y_heads = []
    for hd in range(DN_HEADS):
        lo = hd * DN_HEAD_DIM
        oh = o_all[:, lo:lo + DN_HEAD_DIM]
        oh = oh * lax.rsqrt(jnp.mean(oh * oh, axis=-1, keepdims=True) + EPS) * dnw_ref[...]
        gh = dn_gate[:, lo:lo + DN_HEAD_DIM]
        y_heads.append(oh * (gh * _sigmoid(gh)))
    y_dn = jnp.concatenate(y_heads, axis=-1).astype(BF16)
    y_sc = (sc_b * ysc_s[...]).astype(BF16)
    y_mem = ymem_s[...].astype(BF16)

    merged = _sigmoid(proj(OFF_GATE, D_MODEL)) * _dot(y_dn, w_br[0:DN_WIDTH, :])
    merged = merged + _sigmoid(proj(OFF_GATE + D_MODEL, D_MODEL)) * _dot(y_sc, w_br[DN_WIDTH:DN_WIDTH + SC_WIDTH, :])
    merged = merged + _sigmoid(proj(OFF_GATE + 2 * D_MODEL, D_MODEL)) * _dot(y_mem, w_br[DN_WIDTH + SC_WIDTH:, :])
    h = x + _dot(merged.astype(BF16), w_o[...])
    h_ref[...] = h
    hn = h * lax.rsqrt(jnp.mean(h * h, axis=-1, keepdims=True) + EPS) * n2_ref[...]
    for c in range(ROW_TILES):
        hn_ref[pl.ds(c, rows, stride=ROW_TILES), :] = hn[:, c * LANES:(c + 1) * LANES]
    lg_ref[...] = _dot(hn, wr_ref[...]) + br_ref[...]


def _mixer(x2d, dnc_in, dns_in, scc_in, mk, mv, weights, hn_all, *, n_seq, seq_len, nb, tt, row0):
    chunk = CHUNK if seq_len % CHUNK == 0 else seq_len
    rows = nb * tt
    n_t = seq_len // tt
    total_rows = n_seq * seq_len
    n_chunks = rows // chunk
    hl = DN_HEADS * chunk
    grid = (n_seq // nb, n_t)
    const = lambda b, t: (0, 0)
    seq3 = lambda b, t: (b, 0, 0)
    assert row0 % rows == 0

    def tok(b, t):
        return (b * n_t + t, 0)

    def tok_all(b, t):
        return (row0 // rows + b * n_t + t, 0)

    (w_in, w_br, w_o, n1, n2, dcw, scw, alog, dtb, dnw, wr, br) = weights
    in_specs = [
        pl.BlockSpec((rows, D_MODEL), tok),
        pl.BlockSpec((nb, DN_CONV - 1, QKV_WIDTH), seq3),
        pl.BlockSpec((nb, DN_HEADS, DN_HEAD_DIM, DN_HEAD_DIM), lambda b, t: (b, 0, 0, 0)),
        pl.BlockSpec((nb, SC_CONV - 1, SC_WIDTH), seq3),
        pl.BlockSpec((nb, MEM_LEN, MEM_WIDTH), seq3),
        pl.BlockSpec((nb, MEM_LEN, MEM_WIDTH), seq3),
        pl.BlockSpec(w_in.shape, const, pipeline_mode=pl.Buffered(1)),
        pl.BlockSpec(w_br.shape, const, pipeline_mode=pl.Buffered(1)),
        pl.BlockSpec(w_o.shape, const, pipeline_mode=pl.Buffered(1)),
        pl.BlockSpec(n1.shape, const),
        pl.BlockSpec(n2.shape, const),
        pl.BlockSpec(dcw.shape, const),
        pl.BlockSpec(scw.shape, const),
        pl.BlockSpec(alog.shape, const),
        pl.BlockSpec(dtb.shape, const),
        pl.BlockSpec(dnw.shape, const),
        pl.BlockSpec(wr.shape, const),
        pl.BlockSpec(br.shape, const),
        pl.BlockSpec(memory_space=pl.ANY),
    ]
    out_shape = [
        jax.ShapeDtypeStruct((total_rows, D_MODEL), F32),
        jax.ShapeDtypeStruct(hn_all.shape, F32),
        jax.ShapeDtypeStruct((total_rows, LANES), F32),
        jax.ShapeDtypeStruct((n_seq, DN_CONV - 1, QKV_WIDTH), F32),
        jax.ShapeDtypeStruct((n_seq, DN_HEADS, DN_HEAD_DIM, DN_HEAD_DIM), F32),
        jax.ShapeDtypeStruct((n_seq, SC_CONV - 1, SC_WIDTH), F32),
    ]
    out_specs = [
        pl.BlockSpec((rows, D_MODEL), tok),
        pl.BlockSpec((rows * ROW_TILES, LANES), tok_all),
        pl.BlockSpec((rows, LANES), tok),
        pl.BlockSpec((nb, DN_CONV - 1, QKV_WIDTH), seq3),
        pl.BlockSpec((nb, DN_HEADS, DN_HEAD_DIM, DN_HEAD_DIM), lambda b, t: (b, 0, 0, 0)),
        pl.BlockSpec((nb, SC_CONV - 1, SC_WIDTH), seq3),
    ]
    args = [x2d, dnc_in, dns_in, scc_in, mk, mv, w_in, w_br, w_o, n1, n2, dcw, scw, alog, dtb, dnw, wr, br, hn_all]
    scratch = [
        pltpu.VMEM((nb, CONV_PAD + tt, QKV_WIDTH), F32),
        pltpu.VMEM((nb, CONV_PAD + tt, SC_WIDTH), F32),
        pltpu.VMEM((rows, DN_WIDTH), F32),
        pltpu.VMEM((rows, DN_WIDTH), F32),
        pltpu.VMEM((rows, DN_WIDTH), F32),
        pltpu.VMEM((rows, LANES), F32),
        pltpu.VMEM((rows, DN_WIDTH), F32),
        pltpu.VMEM((rows, MEM_WIDTH), F32),
        pltpu.VMEM((rows, SC_WIDTH), F32),
        pltpu.VMEM((rows, MEM_WIDTH), F32),
        pltpu.VMEM((n_chunks, hl, DN_HEAD_DIM), F32),
        pltpu.VMEM((n_chunks, hl, DN_HEAD_DIM), F32),
        pltpu.VMEM((n_chunks, hl, hl), F32),
        pltpu.VMEM((n_chunks, hl, DN_HEAD_DIM), F32),
        pltpu.VMEM((n_chunks, hl, DN_HEAD_DIM), F32),
        pltpu.VMEM((n_chunks, hl, DN_HEAD_DIM), F32),
    ]
    return pl.pallas_call(
        functools.partial(_mixer_kernel, nb=nb, tt=tt, chunk=chunk),
        grid=grid,
        in_specs=in_specs,
        out_specs=out_specs,
        out_shape=out_shape,
        scratch_shapes=scratch,
        input_output_aliases={len(args) - 1: 1},
        compiler_params=pltpu.CompilerParams(
            dimension_semantics=("arbitrary", "arbitrary"), vmem_limit_bytes=VMEM_LIMIT),
        name="mixer",
    )(*args)


def _router_kernel(lg_ref, idx_ref, gate_ref, rank_ref, cnt_ref, carry):
    i = pl.program_id(0)

    @pl.when(i == 0)
    def _():
        carry[...] = jnp.zeros_like(carry)

    work = lg_ref[...]
    tm = work.shape[0]
    lane = lax.broadcasted_iota(jnp.int32, (tm, LANES), 1).astype(F32)
    idxs, vals = [], []
    for _ in range(TOP_K):
        m = jnp.max(work, axis=-1, keepdims=True)
        ik = jnp.min(jnp.where(work == m, lane, float(LANES)), axis=-1, keepdims=True)
        idxs.append(ik)
        vals.append(m)
        work = jnp.where(lane == ik, -jnp.inf, work)
    exps = [jnp.exp(v - vals[0]) for v in vals]
    denom = exps[0] + exps[1] + exps[2] + exps[3]
    hot = jnp.zeros((tm, LANES), F32)
    for ik in idxs:
        hot = hot + (lane == ik).astype(F32)
    ri = lax.broadcasted_iota(jnp.int32, (tm, tm), 0)
    ci = lax.broadcasted_iota(jnp.int32, (tm, tm), 1)
    before = (ri > ci).astype(BF16)
    prefix = _dot(before, hot.astype(BF16)) + carry[...]
    idx_out = jnp.zeros((tm, LANES), F32)
    gate_out = jnp.zeros((tm, LANES), F32)
    rank_out = jnp.zeros((tm, LANES), F32)
    for k in range(TOP_K):
        rk = jnp.sum(jnp.where(lane == idxs[k], prefix, 0.0), axis=-1, keepdims=True)
        idx_out = jnp.where(lane == k, idxs[k], idx_out)
        gate_out = jnp.where(lane == k, exps[k] / denom, gate_out)
        rank_out = jnp.where(lane == k, rk, rank_out)
    idx_ref[...] = idx_out.astype(jnp.int32)
    gate_ref[...] = gate_out
    rank_ref[...] = rank_out.astype(jnp.int32)
    carry[...] = carry[...] + jnp.sum(hot, axis=0, keepdims=True)
    cnt_ref[...] = carry[...].astype(jnp.int32)


def _router(logits):
    n_tok = logits.shape[0]
    tile = lambda i: (i, 0)
    return pl.pallas_call(
        _router_kernel,
        grid=(n_tok // TOK_TILE,),
        in_specs=[pl.BlockSpec((TOK_TILE, LANES), tile)],
        out_specs=[pl.BlockSpec((TOK_TILE, LANES), tile)] * 3 + [pl.BlockSpec((1, LANES), lambda i: (0, 0))],
        out_shape=[
            jax.ShapeDtypeStruct((n_tok, LANES), jnp.int32),
            jax.ShapeDtypeStruct((n_tok, LANES), F32),
            jax.ShapeDtypeStruct((n_tok, LANES), jnp.int32),
            jax.ShapeDtypeStruct((1, LANES), jnp.int32),
        ],
        scratch_shapes=[pltpu.VMEM((1, LANES), F32)],
        compiler_params=pltpu.CompilerParams(dimension_semantics=("arbitrary",)),
        name="router",
    )(logits)


def _invert_kernel(dest_ref, code_ref, default_ref, inv_ref, sem):
    i = pl.program_id(0)

    @pl.when(i == 0)
    def _():
        cp = pltpu.make_async_copy(default_ref, inv_ref, sem)
        cp.start()
        cp.wait()

    def body(j, c):
        for d in range(INVERT_UNROLL):
            q = j * INVERT_UNROLL + d
            inv_ref[dest_ref[q]] = code_ref[q]
        return c

    lax.fori_loop(0, TOK_TILE * TOP_K // INVERT_UNROLL, body, 0)


def _invert(dest_flat, pair_code, default_inv):
    n_pairs = TOK_TILE * TOP_K
    return pl.pallas_call(
        _invert_kernel,
        grid=(dest_flat.shape[0] // n_pairs,),
        in_specs=[
            pl.BlockSpec((n_pairs,), lambda i: (i,), memory_space=pltpu.SMEM),
            pl.BlockSpec((n_pairs,), lambda i: (i,), memory_space=pltpu.SMEM),
            pl.BlockSpec(memory_space=pl.ANY),
        ],
        out_specs=pl.BlockSpec(memory_space=pltpu.SMEM),
        out_shape=jax.ShapeDtypeStruct(default_inv.shape, jnp.int32),
        scratch_shapes=[pltpu.SemaphoreType.DMA(())],
        compiler_params=pltpu.CompilerParams(dimension_semantics=("arbitrary",)),
        name="invert",
    )(dest_flat, pair_code, default_inv)


def _experts_kernel(blk_e, n_valid, inv, hn_ref, wg_ref, bg_ref, wu_ref, bu_ref, wd_ref, bd_ref, yp_ref,
                    wg_bf, wu_bf, wd_bf, xbuf0, xbuf1, xbuf2, ybuf0, ybuf1, ybuf2, zeros, gsem, ssem, zsem,
                    *, n_tok):
    i = pl.program_id(0)
    nv = n_valid[0]
    phase = lax.rem(i, RING)
    xbuf = (xbuf0, xbuf1, xbuf2)
    ybuf = (ybuf0, ybuf1, ybuf2)
    plane_rows = n_tok + TRASH_TOK

    def tile_rows(row, n=1):
        return pl.ds(pl.multiple_of(row * ROW_TILES, ROW_TILES), n * ROW_TILES)

    def gather(block, s):
        base = block * ROW_BLOCK
        for r in range(ROW_BLOCK):
            tok = inv[base + r] & PAIR_TOK_MASK
            pltpu.make_async_copy(
                hn_ref.at[tile_rows(tok)], xbuf[s].at[tile_rows(r)], gsem.at[s]).start(priority=r % 2)

    def scatter(block, s):
        base = block * ROW_BLOCK
        for r in range(ROW_BLOCK):
            pair = inv[base + r]
            row = lax.shift_right_logical(pair, PAIR_TOK_BITS) * plane_rows + (pair & PAIR_TOK_MASK)
            pltpu.make_async_copy(
                ybuf[s].at[tile_rows(r)], yp_ref.at[tile_rows(row)], ssem.at[s]).start(priority=(r + 1) % 2)

    def wait_gather(s):
        pltpu.make_async_copy(hn_ref.at[tile_rows(0, ROW_BLOCK)], xbuf[s], gsem.at[s]).wait()

    def wait_scatter(s):
        pltpu.make_async_copy(ybuf[s], yp_ref.at[tile_rows(0, ROW_BLOCK)], ssem.at[s]).wait()

    @pl.when(i == 0)
    def _():
        zeros[...] = jnp.zeros_like(zeros)
        for k in range(TOP_K):
            cp = pltpu.make_async_copy(zeros, yp_ref.at[tile_rows(k * plane_rows + n_tok, TRASH_TOK)], zsem)
            cp.start()
            cp.wait()
        gather(0, 0)
        gather(jnp.minimum(1, nv - 1), 1)

    e = blk_e[i]
    prev = blk_e[jnp.maximum(i - 1, 0)]

    @pl.when(((i == 0) | (e != prev)) & (i < nv))
    def _():
        wg_bf[...] = wg_ref[0].astype(BF16)
        wu_bf[...] = wu_ref[0].astype(BF16)
        wd_bf[...] = wd_ref[0].astype(BF16)

    def step(slot, scatter_previous):
        ahead, behind = (slot + 2) % RING, (slot - 1) % RING
        wait_gather(slot)
        gather(jnp.minimum(i + 2, nv - 1), ahead)
        if scatter_previous:
            scatter(i - 1, behind)
        x = jnp.concatenate(
            [xbuf[slot][pl.ds(c, ROW_BLOCK, stride=ROW_TILES), :] for c in range(ROW_TILES)], axis=1).astype(BF16)
        gl = jnp.minimum(_dot(x, wg_bf[...]) + bg_ref[0], SWIGLU_LIMIT)
        ul = jnp.clip(_dot(x, wu_bf[...]) + bu_ref[0], -SWIGLU_LIMIT, SWIGLU_LIMIT)
        act = (ul + 1.0) * (gl * _sigmoid(SWIGLU_ALPHA * gl))
        y = _dot(act.astype(BF16), wd_bf[...]) + bd_ref[0]
        for c in range(ROW_TILES):
            ybuf[slot][pl.ds(c, ROW_BLOCK, stride=ROW_TILES), :] = y[:, c * LANES:(c + 1) * LANES]

    def finish(slot):
        ahead, behind = (slot + 2) % RING, (slot - 1) % RING
        scatter(i, slot)
        wait_gather((slot + 1) % RING)
        wait_gather(ahead)

        @pl.when(i >= 2)
        def _():
            wait_scatter((slot - 2) % RING)

        @pl.when(i >= 1)
        def _():
            wait_scatter(behind)

        wait_scatter(slot)

    @pl.when((i == 0) & (i < nv))
    def _():
        step(0, False)

    for slot in range(RING):
        mine = (phase == slot) & (i < nv)

        @pl.when(mine & (i >= RING))
        def _():
            wait_scatter(slot)

        @pl.when(mine & (i > 0))
        def _():
            step(slot, True)

        @pl.when(mine & (i == nv - 1))
        def _():
            finish(slot)


def _experts(blk_e, n_valid, inv, hn_all, w_gate, b_gate, w_up, b_up, w_down, b_down, *, n_tok):
    n_blocks = blk_e.shape[0]
    wsel = lambda i, be, nv, iv: (be[i], 0, 0)
    d_ff = w_gate.shape[-1]
    grid_spec = pltpu.PrefetchScalarGridSpec(
        num_scalar_prefetch=3,
        grid=(n_blocks,),
        in_specs=[
            pl.BlockSpec(memory_space=pl.ANY),
            pl.BlockSpec((1, D_MODEL, d_ff), wsel),
            pl.BlockSpec((1, 1, d_ff), wsel),
            pl.BlockSpec((1, D_MODEL, d_ff), wsel),
            pl.BlockSpec((1, 1, d_ff), wsel),
            pl.BlockSpec((1, d_ff, D_MODEL), wsel),
            pl.BlockSpec((1, 1, D_MODEL), wsel),
        ],
        out_specs=pl.BlockSpec(memory_space=pl.ANY),
        scratch_shapes=[
            pltpu.VMEM((D_MODEL, d_ff), BF16),
            pltpu.VMEM((D_MODEL, d_ff), BF16),
            pltpu.VMEM((d_ff, D_MODEL), BF16),
        ] + [pltpu.VMEM((ROW_BLOCK * ROW_TILES, LANES), F32)] * (2 * RING) + [
            pltpu.VMEM((TRASH_TOK * ROW_TILES, LANES), F32),
            pltpu.SemaphoreType.DMA((RING,)),
            pltpu.SemaphoreType.DMA((RING,)),
            pltpu.SemaphoreType.DMA(()),
        ],
    )
    return pl.pallas_call(
        functools.partial(_experts_kernel, n_tok=n_tok),
        grid_spec=grid_spec,
        out_shape=jax.ShapeDtypeStruct((TOP_K * (n_tok + TRASH_TOK) * ROW_TILES, LANES), F32),
        compiler_params=pltpu.CompilerParams(
            dimension_semantics=("arbitrary",), vmem_limit_bytes=VMEM_LIMIT),
        name="experts",
    )(blk_e, n_valid, inv, hn_all, w_gate, b_gate, w_up, b_up, w_down, b_down)


def _combine_kernel(gate_ref, h_ref, fw_ref, *refs):
    y_ref = refs[TOP_K]
    gate = gate_ref[...]
    acc = h_ref[...]
    for k in range(TOP_K):
        y_k = jnp.concatenate(
            [refs[k][pl.ds(c, TOK_TILE, stride=ROW_TILES), :] for c in range(ROW_TILES)], axis=1)
        acc = acc + gate[:, k:k + 1] * y_k
    y_ref[...] = acc * lax.rsqrt(jnp.mean(acc * acc, axis=-1, keepdims=True) + EPS) * fw_ref[...]


def _combine(gates, h_group, final_w, y_pairs, *, tile0, plane_rows):
    n_tiles = h_group.shape[0] // TOK_TILE
    plane_tiles = plane_rows // TOK_TILE

    def plane(k):
        return pl.BlockSpec((TOK_TILE * ROW_TILES, LANES), lambda i: (k * plane_tiles + tile0 + i, 0))

    return pl.pallas_call(
        _combine_kernel,
        grid=(n_tiles,),
        in_specs=[
            pl.BlockSpec((TOK_TILE, LANES), lambda i: (tile0 + i, 0)),
            pl.BlockSpec((TOK_TILE, D_MODEL), lambda i: (i, 0)),
            pl.BlockSpec((1, D_MODEL), lambda i: (0, 0)),
        ] + [plane(k) for k in range(TOP_K)],
        out_specs=pl.BlockSpec((TOK_TILE, D_MODEL), lambda i: (i, 0)),
        out_shape=jax.ShapeDtypeStruct((n_tiles * TOK_TILE, D_MODEL), F32),
        compiler_params=pltpu.CompilerParams(dimension_semantics=("arbitrary",)),
        name="combine",
    )(gates, h_group, final_w, *([y_pairs] * TOP_K))


def _permute_w_in(w):
    o_a = QKV_WIDTH
    o_g = o_a + 2 * DN_HEADS
    o_rest = o_g + DN_WIDTH
    pad = jnp.zeros((w.shape[0], LANES - 2 * DN_HEADS), BF16)
    pieces = [w[:, :o_a], w[:, o_g:o_rest], w[:, o_rest:], w[:, o_a:o_g]]
    return jnp.concatenate([p.astype(BF16) for p in pieces] + [pad], axis=1)


def _lane_row(v, fill=0.0):
    return jnp.concatenate([v.astype(F32), jnp.full((LANES - v.shape[0],), fill, F32)]).reshape(1, LANES)


def kernel(x_prompt, x_sample, mem_prompt, state_dn, state_dn_conv, state_sc_conv, cache_mem_k, cache_mem_v, w_in, dn_conv_w, dn_A_log, dn_dt_bias, dn_norm_w, sc_conv_w, mem_norm_w, w_mem_kv, w_br, w_o, norm1_w, norm2_w, w_router, b_router, w_gate, b_gate, w_up, b_up, w_down, b_down, final_norm_w):
    assert w_in.shape[0] == 1, "one layer"
    bp, tp, _ = x_prompt.shape
    bs, ts, _ = x_sample.shape
    n_p, n_s = bp * tp, bs * ts
    n_tok = n_p + n_s
    assert n_p % TOK_TILE == 0 and n_s % TOK_TILE == 0 and tp % MIX_TILE == 0

    weights = (
        _permute_w_in(w_in[0]),
        w_br[0].astype(BF16),
        w_o[0].astype(BF16),
        norm1_w[0].reshape(1, D_MODEL),
        norm2_w[0].reshape(1, D_MODEL),
        dn_conv_w[0],
        sc_conv_w[0],
        _lane_row(dn_A_log[0]),
        _lane_row(dn_dt_bias[0]),
        dn_norm_w[0].reshape(1, DN_HEAD_DIM),
        jnp.concatenate([w_router[0], jnp.zeros((D_MODEL, LANES - N_EXPERTS), F32)], axis=1),
        _lane_row(b_router[0], NEG_BIG),
    )

    mk2d, mv2d = _memkv(mem_prompt.reshape(bp * MEM_LEN, D_MODEL), mem_norm_w[0].reshape(1, D_MODEL),
                        w_mem_kv[0].astype(BF16))

    assert n_tok + TRASH_TOK <= PAIR_TOK_MASK + 1 and (n_tok + TRASH_TOK) % MIX_TILE == 0
    hn_all = jnp.zeros(((n_tok + TRASH_TOK) * ROW_TILES, LANES), F32)
    h_s, hn_all, lg_s, s_dnc, s_dns, s_scc = _mixer(
        x_sample.reshape(n_s, D_MODEL), state_dn_conv[0], state_dn[0], state_sc_conv[0],
        cache_mem_k[0].reshape(bs, MEM_LEN, MEM_WIDTH), cache_mem_v[0].reshape(bs, MEM_LEN, MEM_WIDTH),
        weights, hn_all, n_seq=bs, seq_len=ts, nb=bs, tt=ts, row0=n_p)
    h_p, hn_all, lg_p, p_dnc, p_dns, p_scc = _mixer(
        x_prompt.reshape(n_p, D_MODEL),
        jnp.zeros((bp, DN_CONV - 1, QKV_WIDTH), F32),
        jnp.zeros((bp, DN_HEADS, DN_HEAD_DIM, DN_HEAD_DIM), F32),
        jnp.zeros((bp, SC_CONV - 1, SC_WIDTH), F32),
        mk2d.reshape(bp, MEM_LEN, MEM_WIDTH), mv2d.reshape(bp, MEM_LEN, MEM_WIDTH),
        weights, hn_all, n_seq=bp, seq_len=tp, nb=1, tt=MIX_TILE, row0=0)

    idx, gates, rank, counts = _router(jnp.concatenate([lg_p, lg_s], axis=0))
    counts = counts[0, :N_EXPERTS]
    n_blk_e = (counts + ROW_BLOCK - 1) // ROW_BLOCK
    blk_end = jnp.cumsum(n_blk_e)
    blk_start = blk_end - n_blk_e
    row_start = blk_start * ROW_BLOCK
    experts = jnp.arange(N_EXPERTS, dtype=jnp.int32)
    start_of = jnp.sum(jnp.where(idx[:, :TOP_K, None] == experts, row_start, 0), axis=-1)
    dest = (start_of + rank[:, :TOP_K]).reshape(n_tok * TOP_K)

    n_blocks = (n_tok * TOP_K) // ROW_BLOCK + N_EXPERTS
    bi = jnp.arange(n_blocks, dtype=jnp.int32)
    n_valid = blk_end[-1].astype(jnp.int32)
    bclip = jnp.minimum(bi, n_valid - 1)
    blk_e = jnp.sum((bclip[:, None] >= blk_end[None, :]).astype(jnp.int32), axis=1)
    n_valid = n_valid.reshape(1)

    slots = jnp.arange(n_blocks * ROW_BLOCK, dtype=jnp.int32)
    default_inv = n_tok + (slots & (TRASH_TOK - 1))
    pairs = jnp.arange(n_tok * TOP_K, dtype=jnp.int32)
    pair_code = ((pairs % TOP_K) << PAIR_TOK_BITS) | (pairs // TOP_K)
    inv = _invert(dest.astype(jnp.int32), pair_code, default_inv)

    y_pairs = _experts(blk_e, n_valid, inv, hn_all,
                       w_gate[0], b_gate[0].reshape(N_EXPERTS, 1, -1), w_up[0], b_up[0].reshape(N_EXPERTS, 1, -1),
                       w_down[0], b_down[0].reshape(N_EXPERTS, 1, -1), n_tok=n_tok)
    fw = final_norm_w.reshape(1, D_MODEL)
    y_p = _combine(gates, h_p, fw, y_pairs, tile0=0, plane_rows=n_tok + TRASH_TOK)
    y_s = _combine(gates, h_s, fw, y_pairs, tile0=n_p // TOK_TILE, plane_rows=n_tok + TRASH_TOK)

    return (y_p.reshape(bp, tp, D_MODEL), y_s.reshape(bs, ts, D_MODEL),
            p_dns[None], p_dnc[None], p_scc[None],
            mk2d.reshape(1, bp, MEM_LEN, MEM_HEADS, MEM_HEAD_DIM), mv2d.reshape(1, bp, MEM_LEN, MEM_HEADS, MEM_HEAD_DIM),
            s_dns[None], s_dnc[None], s_scc[None])
```

```python
import functools

import jax
import jax.numpy as jnp
from jax import lax
from jax.experimental import pallas as pl
from jax.experimental.pallas import tpu as pltpu

F32 = jnp.float32
BF16 = jnp.bfloat16

D_MODEL = 1024
CHUNK = 64
EPS = 1e-6
DN_HEADS = 4
DN_HEAD_DIM = 128
DN_WIDTH = DN_HEADS * DN_HEAD_DIM
QKV_WIDTH = 3 * DN_WIDTH
DN_CONV = 4
SC_WIDTH = 256
SC_CONV = 3
MEM_LEN = 256
MEM_HEADS = 4
MEM_HEAD_DIM = 64
MEM_WIDTH = MEM_HEADS * MEM_HEAD_DIM
N_EXPERTS = 32
TOP_K = 4
SWIGLU_ALPHA = 1.702
SWIGLU_LIMIT = 7.0

LANES = 128
SUBLANES = 8
ROW_TILES = D_MODEL // LANES
assert ROW_TILES == SUBLANES
CONV_PAD = 8

OFF_QKV = 0
OFF_DNG = OFF_QKV + QKV_WIDTH
OFF_SC = OFF_DNG + DN_WIDTH
OFF_MQ = OFF_SC + 3 * SC_WIDTH
OFF_GATE = OFF_MQ + MEM_WIDTH
OFF_AB = OFF_GATE + 3 * D_MODEL
IN_PERM_WIDTH = OFF_AB + LANES

MIX_TILE = 512
TOK_TILE = 256
ROW_BLOCK = 256
INVERT_UNROLL = 32
TRASH_TOK = 256
RING = 3
PAIR_TOK_BITS = 16
PAIR_TOK_MASK = (1 << PAIR_TOK_BITS) - 1
VMEM_LIMIT = 56 * 1024 * 1024
NEG_BIG = -1e30


def _dot(a, b):
    return jnp.dot(a, b, preferred_element_type=F32)


def _dot_nt(a, b):
    return lax.dot_general(a, b, (((1,), (1,)), ((), ())), preferred_element_type=F32)


def _dot_tn(a, b):
    return lax.dot_general(a, b, (((0,), (0,)), ((), ())), preferred_element_type=F32)


def _sigmoid(x):
    return 1.0 / (1.0 + jnp.exp(-x))


def _softplus(x):
    return jnp.maximum(x, 0.0) + jnp.log1p(jnp.exp(-jnp.abs(x)))


def _for_each(n, body):
    if n == 1:
        body(0)
    else:
        def step(i, carry):
            body(i)
            return carry
        lax.fori_loop(0, n, step, 0)


def _memkv_kernel(mem_ref, nw_ref, w_ref, k_ref, v_ref):
    x = mem_ref[...]
    xn = x * lax.rsqrt(jnp.mean(x * x, axis=-1, keepdims=True) + EPS) * nw_ref[...]
    kv = _dot(xn.astype(BF16), w_ref[...])
    k_ref[...] = kv[:, :MEM_WIDTH]
    v_ref[...] = kv[:, MEM_WIDTH:]


def _memkv(mem2d, norm_w, w_kv_bf16):
    rows = mem2d.shape[0]
    grid = rows // MEM_LEN
    return pl.pallas_call(
        _memkv_kernel,
        grid=(grid,),
        in_specs=[
            pl.BlockSpec((MEM_LEN, D_MODEL), lambda i: (i, 0)),
            pl.BlockSpec((1, D_MODEL), lambda i: (0, 0)),
            pl.BlockSpec((D_MODEL, 2 * MEM_WIDTH), lambda i: (0, 0)),
        ],
        out_specs=[
            pl.BlockSpec((MEM_LEN, MEM_WIDTH), lambda i: (i, 0)),
            pl.BlockSpec((MEM_LEN, MEM_WIDTH), lambda i: (i, 0)),
        ],
        out_shape=[jax.ShapeDtypeStruct((rows, MEM_WIDTH), F32)] * 2,
        name="memkv",
    )(mem2d, norm_w, w_kv_bf16)


def _unit_lower_inverse(a, eye, size):
    inv = eye - a
    power = a
    span = 2
    while span < size:
        power = _dot(power, power)
        inv = _dot(inv, eye + power)
        span *= 2
    return inv


def _mixer_kernel(x_ref, dnc_in, dns_in, scc_in, mk_ref, mv_ref, w_in, w_br, w_o, n1_ref, n2_ref, dcw_ref, scw_ref,
                  alog_ref, dtb_ref, dnw_ref, wr_ref, br_ref, hn_all_ref,
                  h_ref, hn_ref, lg_ref, dnc_out, dns_out, scc_out,
                  xp, scp, q_s, k_s, v_s, gb_s, o_s, mq_s, ysc_s, ymem_s, wv_s, wk_s, qk_s, qd_s, kd_s, cd_s,
                  *, nb, tt, chunk):
    t_idx = pl.program_id(1)
    rows = nb * tt
    n_chunk = tt // chunk

    @pl.when(t_idx == 0)
    def _():
        xp[:, CONV_PAD - (DN_CONV - 1):CONV_PAD, :] = dnc_in[...]
        scp[:, CONV_PAD - (SC_CONV - 1):CONV_PAD, :] = scc_in[...]
        dns_out[...] = dns_in[...]

    x = x_ref[...]
    xn = (x * lax.rsqrt(jnp.mean(x * x, axis=-1, keepdims=True) + EPS) * n1_ref[...]).astype(BF16)

    def proj(off, width):
        return _dot(xn, w_in[:, off:off + width])

    qkv_pre = proj(OFF_QKV, QKV_WIDTH)
    for s in range(nb):
        xp[s, CONV_PAD:CONV_PAD + tt, :] = qkv_pre[s * tt:(s + 1) * tt, :]
    sc = proj(OFF_SC, 3 * SC_WIDTH)
    sc_b = sc[:, :SC_WIDTH]
    sc_ch = sc[:, SC_WIDTH:2 * SC_WIDTH] * sc[:, 2 * SC_WIDTH:]
    for s in range(nb):
        scp[s, CONV_PAD:CONV_PAD + tt, :] = sc_ch[s * tt:(s + 1) * tt, :]
    mq_s[...] = proj(OFF_MQ, MEM_WIDTH)

    ab = proj(OFF_AB, LANES)
    lane = lax.broadcasted_iota(jnp.int32, (rows, LANES), 1)
    g_log = -jnp.exp(alog_ref[...]) * _softplus(ab + dtb_ref[...])
    gb_s[...] = jnp.where(lane < DN_HEADS, g_log, _sigmoid(ab))

    def conv_seq(s):
        base = CONV_PAD - (DN_CONV - 1)
        acc = dcw_ref[0:1, :] * xp[s, pl.ds(base, tt), :]
        for j in range(1, DN_CONV):
            acc = acc + dcw_ref[j:j + 1, :] * xp[s, pl.ds(base + j, tt), :]
        act = acc * _sigmoid(acc)
        r0 = pl.multiple_of(s * tt, tt)
        for hd in range(DN_HEADS):
            lo = hd * DN_HEAD_DIM
            qh = act[:, lo:lo + DN_HEAD_DIM]
            kh = act[:, DN_WIDTH + lo:DN_WIDTH + lo + DN_HEAD_DIM]
            q_s[pl.ds(r0, tt), lo:lo + DN_HEAD_DIM] = (
                qh * lax.rsqrt(jnp.sum(qh * qh, axis=-1, keepdims=True) + EPS) * (DN_HEAD_DIM ** -0.5))
            k_s[pl.ds(r0, tt), lo:lo + DN_HEAD_DIM] = (
                kh * lax.rsqrt(jnp.sum(kh * kh, axis=-1, keepdims=True) + EPS))
        v_s[pl.ds(r0, tt), :] = act[:, 2 * DN_WIDTH:]
        tail = xp[s, pl.ds(tt + base, DN_CONV - 1), :]
        dnc_out[s] = tail
        xp[s, pl.ds(base, DN_CONV - 1), :] = tail

        base2 = CONV_PAD - (SC_CONV - 1)
        acc2 = scw_ref[0:1, :] * scp[s, pl.ds(base2, tt), :]
        for j in range(1, SC_CONV):
            acc2 = acc2 + scw_ref[j:j + 1, :] * scp[s, pl.ds(base2 + j, tt), :]
        ysc_s[pl.ds(r0, tt), :] = acc2
        tail2 = scp[s, pl.ds(tt + base2, SC_CONV - 1), :]
        scc_out[s] = tail2
        scp[s, pl.ds(base2, SC_CONV - 1), :] = tail2

    _for_each(nb, conv_seq)

    hl = DN_HEADS * chunk
    shift = chunk.bit_length() - 1
    ri = lax.broadcasted_iota(jnp.int32, (hl, hl), 0)
    ci = lax.broadcasted_iota(jnp.int32, (hl, hl), 1)
    same_head = lax.shift_right_logical(ri, shift) == lax.shift_right_logical(ci, shift)
    causal = same_head & (ri >= ci)
    strict = same_head & (ri > ci)
    eye = (ri == ci).astype(F32)
    ri1 = lax.broadcasted_iota(jnp.int32, (chunk, chunk), 0)
    ci1 = lax.broadcasted_iota(jnp.int32, (chunk, chunk), 1)
    tril = (ri1 >= ci1).astype(F32)
    triu = (ri1 <= ci1).astype(F32)

    def stack_heads(ref, r0):
        return jnp.concatenate(
            [ref[r0:r0 + chunk, hd * DN_HEAD_DIM:(hd + 1) * DN_HEAD_DIM] for hd in range(DN_HEADS)], axis=0)

    for c in range(nb * n_chunk):
        r0 = c * chunk
        gb = gb_s[r0:r0 + chunk, :]
        cum = _dot(tril, gb)
        cum_t = _dot_tn(gb, triu)
        cc = jnp.concatenate([cum[:, hd:hd + 1] for hd in range(DN_HEADS)], axis=0)
        cr = jnp.concatenate([cum_t[hd:hd + 1, :] for hd in range(DN_HEADS)], axis=1)
        beta = jnp.concatenate([gb[:, DN_HEADS + hd:DN_HEADS + hd + 1] for hd in range(DN_HEADS)], axis=0)
        c_last = jnp.concatenate(
            [jnp.broadcast_to(cum[chunk - 1:chunk, hd:hd + 1], (chunk, 1)) for hd in range(DN_HEADS)], axis=0)
        qst, kst, vst = stack_heads(q_s, r0), stack_heads(k_s, r0), stack_heads(v_s, r0)
        decay = jnp.where(causal, jnp.exp(jnp.where(causal, cc - cr, 0.0)), 0.0)
        a_mat = jnp.where(strict, decay * _dot_nt(kst, kst), 0.0) * beta
        t_inv = _unit_lower_inverse(a_mat, eye, chunk)
        e_cum = jnp.exp(cc)
        w = _dot(t_inv, jnp.concatenate([beta * vst, (beta * e_cum) * kst], axis=1))
        wv_s[c] = w[:, :DN_HEAD_DIM]
        wk_s[c] = w[:, DN_HEAD_DIM:]
        qk_s[c] = _dot_nt(qst, kst) * decay
        qd_s[c] = e_cum * qst
        kd_s[c] = jnp.exp(c_last - cc) * kst
        cd_s[c] = jnp.broadcast_to(jnp.exp(c_last), (hl, DN_HEAD_DIM))

    for c in range(nb * n_chunk):
        s = c // n_chunk
        r0 = c * chunk
        states, us = [], []
        for hd in range(DN_HEADS):
            hr = slice(hd * chunk, (hd + 1) * chunk)
            state = dns_out[s, hd]
            states.append(state)
            us.append(wv_s[c, hr, :] - _dot(wk_s[c, hr, :], state))
        o_intra = _dot(qk_s[c], jnp.concatenate(us, axis=0))
        for hd in range(DN_HEADS):
            hr = slice(hd * chunk, (hd + 1) * chunk)
            lo = hd * DN_HEAD_DIM
            o_s[r0:r0 + chunk, lo:lo + DN_HEAD_DIM] = _dot(qd_s[c, hr, :], states[hd]) + o_intra[hr, :]
            dns_out[s, hd] = (cd_s[c, hd * chunk:hd * chunk + 1, :] * states[hd]
                              + _dot_tn(kd_s[c, hr, :], us[hd]))

    def attn_seq(s):
        r0 = pl.multiple_of(s * tt, tt)
        mq = mq_s[pl.ds(r0, tt), :]
        for hd in range(MEM_HEADS):
            lo = hd * MEM_HEAD_DIM
            qh = mq[:, lo:lo + MEM_HEAD_DIM].astype(BF16)
            kh = mk_ref[s, :, lo:lo + MEM_HEAD_DIM].astype(BF16)
            vh = mv_ref[s, :, lo:lo + MEM_HEAD_DIM].astype(BF16)
            sc_h = _dot_nt(qh, kh) * (MEM_HEAD_DIM ** -0.5)
            p = jnp.exp(sc_h - jnp.max(sc_h, axis=-1, keepdims=True))
            denom = jnp.sum(p, axis=-1, keepdims=True)
            ymem_s[pl.ds(r0, tt), lo:lo + MEM_HEAD_DIM] = _dot(p.astype(BF16), vh) / denom

    _for_each(nb, attn_seq)

    dn_gate = proj(OFF_DNG, DN_WIDTH)
    o_all = o_s[...]
    y_heads = []
    for hd in range(DN_HEADS):
        lo = hd * DN_HEAD_DIM
        oh = o_all[:, lo:lo + DN_HEAD_DIM]
        oh = oh * lax.rsqrt(jnp.mean(oh * oh, axis=-1, keepdims=True) + EPS) * dnw_ref[...]
        gh = dn_gate[:, lo:lo + DN_HEAD_DIM]
        y_heads.append(oh * (gh * _sigmoid(gh)))
    y_dn = jnp.concatenate(y_heads, axis=-1).astype(BF16)
    y_sc = (sc_b * ysc_s[...]).astype(BF16)
    y_mem = ymem_s[...].astype(BF16)

    merged = _sigmoid(proj(OFF_GATE, D_MODEL)) * _dot(y_dn, w_br[0:DN_WIDTH, :])
    merged = merged + _sigmoid(proj(OFF_GATE + D_MODEL, D_MODEL)) * _dot(y_sc, w_br[DN_WIDTH:DN_WIDTH + SC_WIDTH, :])
    merged = merged + _sigmoid(proj(OFF_GATE + 2 * D_MODEL, D_MODEL)) * _dot(y_mem, w_br[DN_WIDTH + SC_WIDTH:, :])
    h = x + _dot(merged.astype(BF16), w_o[...])
    h_ref[...] = h
    hn = h * lax.rsqrt(jnp.mean(h * h, axis=-1, keepdims=True) + EPS) * n2_ref[...]
    for c in range(ROW_TILES):
        hn_ref[pl.ds(c, rows, stride=ROW_TILES), :] = hn[:, c * LANES:(c + 1) * LANES]
    lg_ref[...] = _dot(hn, wr_ref[...]) + br_ref[...]


def _mixer(x2d, dnc_in, dns_in, scc_in, mk, mv, weights, hn_all, *, n_seq, seq_len, nb, tt, row0):
    chunk = CHUNK if seq_len % CHUNK == 0 else seq_len
    rows = nb * tt
    n_t = seq_len // tt
    total_rows = n_seq * seq_len
    n_chunks = rows // chunk
    hl = DN_HEADS * chunk
    grid = (n_seq // nb, n_t)
    const = lambda b, t: (0, 0)
    seq3 = lambda b, t: (b, 0, 0)
    assert row0 % rows == 0

    def tok(b, t):
        return (b * n_t + t, 0)

    def tok_all(b, t):
        return (row0 // rows + b * n_t + t, 0)

    (w_in, w_br, w_o, n1, n2, dcw, scw, alog, dtb, dnw, wr, br) = weights
    in_specs = [
        pl.BlockSpec((rows, D_MODEL), tok),
        pl.BlockSpec((nb, DN_CONV - 1, QKV_WIDTH), seq3),
        pl.BlockSpec((nb, DN_HEADS, DN_HEAD_DIM, DN_HEAD_DIM), lambda b, t: (b, 0, 0, 0)),
        pl.BlockSpec((nb, SC_CONV - 1, SC_WIDTH), seq3),
        pl.BlockSpec((nb, MEM_LEN, MEM_WIDTH), seq3),
        pl.BlockSpec((nb, MEM_LEN, MEM_WIDTH), seq3),
        pl.BlockSpec(w_in.shape, const, pipeline_mode=pl.Buffered(1)),
        pl.BlockSpec(w_br.shape, const, pipeline_mode=pl.Buffered(1)),
        pl.BlockSpec(w_o.shape, const, pipeline_mode=pl.Buffered(1)),
        pl.BlockSpec(n1.shape, const),
        pl.BlockSpec(n2.shape, const),
        pl.BlockSpec(dcw.shape, const),
        pl.BlockSpec(scw.shape, const),
        pl.BlockSpec(alog.shape, const),
        pl.BlockSpec(dtb.shape, const),
        pl.BlockSpec(dnw.shape, const),
        pl.BlockSpec(wr.shape, const),
        pl.BlockSpec(br.shape, const),
        pl.BlockSpec(memory_space=pl.ANY),
    ]
    out_shape = [
        jax.ShapeDtypeStruct((total_rows, D_MODEL), F32),
        jax.ShapeDtypeStruct(hn_all.shape, F32),
        jax.ShapeDtypeStruct((total_rows, LANES), F32),
        jax.ShapeDtypeStruct((n_seq, DN_CONV - 1, QKV_WIDTH), F32),
        jax.ShapeDtypeStruct((n_seq, DN_HEADS, DN_HEAD_DIM, DN_HEAD_DIM), F32),
        jax.ShapeDtypeStruct((n_seq, SC_CONV - 1, SC_WIDTH), F32),
    ]
    out_specs = [
        pl.BlockSpec((rows, D_MODEL), tok),
        pl.BlockSpec((rows * ROW_TILES, LANES), tok_all),
        pl.BlockSpec((rows, LANES), tok),
        pl.BlockSpec((nb, DN_CONV - 1, QKV_WIDTH), seq3),
        pl.BlockSpec((nb, DN_HEADS, DN_HEAD_DIM, DN_HEAD_DIM), lambda b, t: (b, 0, 0, 0)),
        pl.BlockSpec((nb, SC_CONV - 1, SC_WIDTH), seq3),
    ]
    args = [x2d, dnc_in, dns_in, scc_in, mk, mv, w_in, w_br, w_o, n1, n2, dcw, scw, alog, dtb, dnw, wr, br, hn_all]
    scratch = [
        pltpu.VMEM((nb, CONV_PAD + tt, QKV_WIDTH), F32),
        pltpu.VMEM((nb, CONV_PAD + tt, SC_WIDTH), F32),
        pltpu.VMEM((rows, DN_WIDTH), F32),
        pltpu.VMEM((rows, DN_WIDTH), F32),
        pltpu.VMEM((rows, DN_WIDTH), F32),
        pltpu.VMEM((rows, LANES), F32),
        pltpu.VMEM((rows, DN_WIDTH), F32),
        pltpu.VMEM((rows, MEM_WIDTH), F32),
        pltpu.VMEM((rows, SC_WIDTH), F32),
        pltpu.VMEM((rows, MEM_WIDTH), F32),
        pltpu.VMEM((n_chunks, hl, DN_HEAD_DIM), F32),
        pltpu.VMEM((n_chunks, hl, DN_HEAD_DIM), F32),
        pltpu.VMEM((n_chunks, hl, hl), F32),
        pltpu.VMEM((n_chunks, hl, DN_HEAD_DIM), F32),
        pltpu.VMEM((n_chunks, hl, DN_HEAD_DIM), F32),
        pltpu.VMEM((n_chunks, hl, DN_HEAD_DIM), F32),
    ]
    return pl.pallas_call(
        functools.partial(_mixer_kernel, nb=nb, tt=tt, chunk=chunk),
        grid=grid,
        in_specs=in_specs,
        out_specs=out_specs,
        out_shape=out_shape,
        scratch_shapes=scratch,
        input_output_aliases={len(args) - 1: 1},
        compiler_params=pltpu.CompilerParams(
            dimension_semantics=("arbitrary", "arbitrary"), vmem_limit_bytes=VMEM_LIMIT),
        name="mixer",
    )(*args)


def _router_kernel(lg_ref, idx_ref, gate_ref, rank_ref, cnt_ref, carry):
    i = pl.program_id(0)

    @pl.when(i == 0)
    def _():
        carry[...] = jnp.zeros_like(carry)

    work = lg_ref[...]
    tm = work.shape[0]
    lane = lax.broadcasted_iota(jnp.int32, (tm, LANES), 1).astype(F32)
    idxs, vals = [], []
    for _ in range(TOP_K):
        m = jnp.max(work, axis=-1, keepdims=True)
        ik = jnp.min(jnp.where(work == m, lane, float(LANES)), axis=-1, keepdims=True)
        idxs.append(ik)
        vals.append(m)
        work = jnp.where(lane == ik, -jnp.inf, work)
    exps = [jnp.exp(v - vals[0]) for v in vals]
    denom = exps[0] + exps[1] + exps[2] + exps[3]
    hot = jnp.zeros((tm, LANES), F32)
    for ik in idxs:
        hot = hot + (lane == ik).astype(F32)
    ri = lax.broadcasted_iota(jnp.int32, (tm, tm), 0)
    ci = lax.broadcasted_iota(jnp.int32, (tm, tm), 1)
    before = (ri > ci).astype(BF16)
    prefix = _dot(before, hot.astype(BF16)) + carry[...]
    idx_out = jnp.zeros((tm, LANES), F32)
    gate_out = jnp.zeros((tm, LANES), F32)
    rank_out = jnp.zeros((tm, LANES), F32)
    for k in range(TOP_K):
        rk = jnp.sum(jnp.where(lane == idxs[k], prefix, 0.0), axis=-1, keepdims=True)
        idx_out = jnp.where(lane == k, idxs[k], idx_out)
        gate_out = jnp.where(lane == k, exps[k] / denom, gate_out)
        rank_out = jnp.where(lane == k, rk, rank_out)
    idx_ref[...] = idx_out.astype(jnp.int32)
    gate_ref[...] = gate_out
    rank_ref[...] = rank_out.astype(jnp.int32)
    carry[...] = carry[...] + jnp.sum(hot, axis=0, keepdims=True)
    cnt_ref[...] = carry[...].astype(jnp.int32)


def _router(logits):
    n_tok = logits.shape[0]
    tile = lambda i: (i, 0)
    return pl.pallas_call(
        _router_kernel,
        grid=(n_tok // TOK_TILE,),
        in_specs=[pl.BlockSpec((TOK_TILE, LANES), tile)],
        out_specs=[pl.BlockSpec((TOK_TILE, LANES), tile)] * 3 + [pl.BlockSpec((1, LANES), lambda i: (0, 0))],
        out_shape=[
            jax.ShapeDtypeStruct((n_tok, LANES), jnp.int32),
            jax.ShapeDtypeStruct((n_tok, LANES), F32),
            jax.ShapeDtypeStruct((n_tok, LANES), jnp.int32),
            jax.ShapeDtypeStruct((1, LANES), jnp.int32),
        ],
        scratch_shapes=[pltpu.VMEM((1, LANES), F32)],
        compiler_params=pltpu.CompilerParams(dimension_semantics=("arbitrary",)),
        name="router",
    )(logits)


def _invert_kernel(dest_ref, code_ref, default_ref, inv_ref, sem):
    i = pl.program_id(0)

    @pl.when(i == 0)
    def _():
        cp = pltpu.make_async_copy(default_ref, inv_ref, sem)
        cp.start()
        cp.wait()

    def body(j, c):
        for d in range(INVERT_UNROLL):
            q = j * INVERT_UNROLL + d
            inv_ref[dest_ref[q]] = code_ref[q]
        return c

    lax.fori_loop(0, TOK_TILE * TOP_K // INVERT_UNROLL, body, 0)


def _invert(dest_flat, pair_code, default_inv):
    n_pairs = TOK_TILE * TOP_K
    return pl.pallas_call(
        _invert_kernel,
        grid=(dest_flat.shape[0] // n_pairs,),
        in_specs=[
            pl.BlockSpec((n_pairs,), lambda i: (i,), memory_space=pltpu.SMEM),
            pl.BlockSpec((n_pairs,), lambda i: (i,), memory_space=pltpu.SMEM),
            pl.BlockSpec(memory_space=pl.ANY),
        ],
        out_specs=pl.BlockSpec(memory_space=pltpu.SMEM),
        out_shape=jax.ShapeDtypeStruct(default_inv.shape, jnp.int32),
        scratch_shapes=[pltpu.SemaphoreType.DMA(())],
        compiler_params=pltpu.CompilerParams(dimension_semantics=("arbitrary",)),
        name="invert",
    )(dest_flat, pair_code, default_inv)


def _experts_kernel(blk_e, n_valid, inv, hn_ref, wg_ref, bg_ref, wu_ref, bu_ref, wd_ref, bd_ref, yp_ref,
                    wg_bf, wu_bf, wd_bf, xbuf0, xbuf1, xbuf2, ybuf0, ybuf1, ybuf2, zeros, gsem, ssem, zsem,
                    *, n_tok):
    i = pl.program_id(0)
    nv = n_valid[0]
    phase = lax.rem(i, RING)
    xbuf = (xbuf0, xbuf1, xbuf2)
    ybuf = (ybuf0, ybuf1, ybuf2)
    plane_rows = n_tok + TRASH_TOK

    def tile_rows(row, n=1):
        return pl.ds(pl.multiple_of(row * ROW_TILES, ROW_TILES), n * ROW_TILES)

    def gather(block, s):
        base = block * ROW_BLOCK
        for r in range(ROW_BLOCK):
            tok = inv[base + r] & PAIR_TOK_MASK
            pltpu.make_async_copy(
                hn_ref.at[tile_rows(tok)], xbuf[s].at[tile_rows(r)], gsem.at[s]).start(priority=0)

    def scatter(block, s):
        base = block * ROW_BLOCK
        for r in range(ROW_BLOCK):
            pair = inv[base + r]
            row = lax.shift_right_logical(pair, PAIR_TOK_BITS) * plane_rows + (pair & PAIR_TOK_MASK)
            pltpu.make_async_copy(
                ybuf[s].at[tile_rows(r)], yp_ref.at[tile_rows(row)], ssem.at[s]).start(priority=1)

    def wait_gather(s):
        pltpu.make_async_copy(hn_ref.at[tile_rows(0, ROW_BLOCK)], xbuf[s], gsem.at[s]).wait()

    def wait_scatter(s):
        pltpu.make_async_copy(ybuf[s], yp_ref.at[tile_rows(0, ROW_BLOCK)], ssem.at[s]).wait()

    @pl.when(i == 0)
    def _():
        zeros[...] = jnp.zeros_like(zeros)
        for k in range(TOP_K):
            cp = pltpu.make_async_copy(zeros, yp_ref.at[tile_rows(k * plane_rows + n_tok, TRASH_TOK)], zsem)
            cp.start()
            cp.wait()
        gather(0, 0)
        gather(jnp.minimum(1, nv - 1), 1)

    e = blk_e[i]
    prev = blk_e[jnp.maximum(i - 1, 0)]

    @pl.when(((i == 0) | (e != prev)) & (i < nv))
    def _():
        wg_bf[...] = wg_ref[0].astype(BF16)
        wu_bf[...] = wu_ref[0].astype(BF16)
        wd_bf[...] = wd_ref[0].astype(BF16)

    def step(slot, scatter_previous):
        ahead, behind = (slot + 2) % RING, (slot - 1) % RING
        wait_gather(slot)
        gather(jnp.minimum(i + 2, nv - 1), ahead)
        if scatter_previous:
            scatter(i - 1, behind)
        x = jnp.concatenate(
            [xbuf[slot][pl.ds(c, ROW_BLOCK, stride=ROW_TILES), :] for c in range(ROW_TILES)], axis=1).astype(BF16)
        gl = jnp.minimum(_dot(x, wg_bf[...]) + bg_ref[0], SWIGLU_LIMIT)
        ul = jnp.clip(_dot(x, wu_bf[...]) + bu_ref[0], -SWIGLU_LIMIT, SWIGLU_LIMIT)
        act = (ul + 1.0) * (gl * _sigmoid(SWIGLU_ALPHA * gl))
        y = _dot(act.astype(BF16), wd_bf[...]) + bd_ref[0]
        for c in range(ROW_TILES):
            ybuf[slot][pl.ds(c, ROW_BLOCK, stride=ROW_TILES), :] = y[:, c * LANES:(c + 1) * LANES]

    def finish(slot):
        ahead, behind = (slot + 2) % RING, (slot - 1) % RING
        scatter(i, slot)
        wait_gather((slot + 1) % RING)
        wait_gather(ahead)

        @pl.when(i >= 2)
        def _():
            wait_scatter((slot - 2) % RING)

        @pl.when(i >= 1)
        def _():
            wait_scatter(behind)

        wait_scatter(slot)

    @pl.when((i == 0) & (i < nv))
    def _():
        step(0, False)

    for slot in range(RING):
        mine = (phase == slot) & (i < nv)

        @pl.when(mine & (i >= RING))
        def _():
            wait_scatter(slot)

        @pl.when(mine & (i > 0))
        def _():
            step(slot, True)

        @pl.when(mine & (i == nv - 1))
        def _():
            finish(slot)


def _experts(blk_e, n_valid, inv, hn_all, w_gate, b_gate, w_up, b_up, w_down, b_down, *, n_tok):
    n_blocks = blk_e.shape[0]
    wsel = lambda i, be, nv, iv: (be[i], 0, 0)
    d_ff = w_gate.shape[-1]
    grid_spec = pltpu.PrefetchScalarGridSpec(
        num_scalar_prefetch=3,
        grid=(n_blocks,),
        in_specs=[
            pl.BlockSpec(memory_space=pl.ANY),
            pl.BlockSpec((1, D_MODEL, d_ff), wsel),
            pl.BlockSpec((1, 1, d_ff), wsel),
            pl.BlockSpec((1, D_MODEL, d_ff), wsel),
            pl.BlockSpec((1, 1, d_ff), wsel),
            pl.BlockSpec((1, d_ff, D_MODEL), wsel),
            pl.BlockSpec((1, 1, D_MODEL), wsel),
        ],
        out_specs=pl.BlockSpec(memory_space=pl.ANY),
        scratch_shapes=[
            pltpu.VMEM((D_MODEL, d_ff), BF16),
            pltpu.VMEM((D_MODEL, d_ff), BF16),
            pltpu.VMEM((d_ff, D_MODEL), BF16),
        ] + [pltpu.VMEM((ROW_BLOCK * ROW_TILES, LANES), F32)] * (2 * RING) + [
            pltpu.VMEM((TRASH_TOK * ROW_TILES, LANES), F32),
            pltpu.SemaphoreType.DMA((RING,)),
            pltpu.SemaphoreType.DMA((RING,)),
            pltpu.SemaphoreType.DMA(()),
        ],
    )
    return pl.pallas_call(
        functools.partial(_experts_kernel, n_tok=n_tok),
        grid_spec=grid_spec,
        out_shape=jax.ShapeDtypeStruct((TOP_K * (n_tok + TRASH_TOK) * ROW_TILES, LANES), F32),
        compiler_params=pltpu.CompilerParams(
            dimension_semantics=("arbitrary",), vmem_limit_bytes=VMEM_LIMIT),
        name="experts",
    )(blk_e, n_valid, inv, hn_all, w_gate, b_gate, w_up, b_up, w_down, b_down)


def _combine_kernel(gate_ref, h_ref, fw_ref, *refs):
    y_ref = refs[TOP_K]
    gate = gate_ref[...]
    acc = h_ref[...]
    for k in range(TOP_K):
        y_k = jnp.concatenate(
            [refs[k][pl.ds(c, TOK_TILE, stride=ROW_TILES), :] for c in range(ROW_TILES)], axis=1)
        acc = acc + gate[:, k:k + 1] * y_k
    y_ref[...] = acc * lax.rsqrt(jnp.mean(acc * acc, axis=-1, keepdims=True) + EPS) * fw_ref[...]


def _combine(gates, h_group, final_w, y_pairs, *, tile0, plane_rows):
    n_tiles = h_group.shape[0] // TOK_TILE
    plane_tiles = plane_rows // TOK_TILE

    def plane(k):
        return pl.BlockSpec((TOK_TILE * ROW_TILES, LANES), lambda i: (k * plane_tiles + tile0 + i, 0))

    return pl.pallas_call(
        _combine_kernel,
        grid=(n_tiles,),
        in_specs=[
            pl.BlockSpec((TOK_TILE, LANES), lambda i: (tile0 + i, 0)),
            pl.BlockSpec((TOK_TILE, D_MODEL), lambda i: (i, 0)),
            pl.BlockSpec((1, D_MODEL), lambda i: (0, 0)),
        ] + [plane(k) for k in range(TOP_K)],
        out_specs=pl.BlockSpec((TOK_TILE, D_MODEL), lambda i: (i, 0)),
        out_shape=jax.ShapeDtypeStruct((n_tiles * TOK_TILE, D_MODEL), F32),
        compiler_params=pltpu.CompilerParams(dimension_semantics=("arbitrary",)),
        name="combine",
    )(gates, h_group, final_w, *([y_pairs] * TOP_K))


def _permute_w_in(w):
    o_a = QKV_WIDTH
    o_g = o_a + 2 * DN_HEADS
    o_rest = o_g + DN_WIDTH
    pad = jnp.zeros((w.shape[0], LANES - 2 * DN_HEADS), BF16)
    pieces = [w[:, :o_a], w[:, o_g:o_rest], w[:, o_rest:], w[:, o_a:o_g]]
    return jnp.concatenate([p.astype(BF16) for p in pieces] + [pad], axis=1)


def _lane_row(v, fill=0.0):
    return jnp.concatenate([v.astype(F32), jnp.full((LANES - v.shape[0],), fill, F32)]).reshape(1, LANES)


def kernel(x_prompt, x_sample, mem_prompt, state_dn, state_dn_conv, state_sc_conv, cache_mem_k, cache_mem_v, w_in, dn_conv_w, dn_A_log, dn_dt_bias, dn_norm_w, sc_conv_w, mem_norm_w, w_mem_kv, w_br, w_o, norm1_w, norm2_w, w_router, b_router, w_gate, b_gate, w_up, b_up, w_down, b_down, final_norm_w):
    assert w_in.shape[0] == 1, "one layer"
    bp, tp, _ = x_prompt.shape
    bs, ts, _ = x_sample.shape
    n_p, n_s = bp * tp, bs * ts
    n_tok = n_p + n_s
    assert n_p % TOK_TILE == 0 and n_s % TOK_TILE == 0 and tp % MIX_TILE == 0

    weights = (
        _permute_w_in(w_in[0]),
        w_br[0].astype(BF16),
        w_o[0].astype(BF16),
        norm1_w[0].reshape(1, D_MODEL),
        norm2_w[0].reshape(1, D_MODEL),
        dn_conv_w[0],
        sc_conv_w[0],
        _lane_row(dn_A_log[0]),
        _lane_row(dn_dt_bias[0]),
        dn_norm_w[0].reshape(1, DN_HEAD_DIM),
        jnp.concatenate([w_router[0], jnp.zeros((D_MODEL, LANES - N_EXPERTS), F32)], axis=1),
        _lane_row(b_router[0], NEG_BIG),
    )

    mk2d, mv2d = _memkv(mem_prompt.reshape(bp * MEM_LEN, D_MODEL), mem_norm_w[0].reshape(1, D_MODEL),
                        w_mem_kv[0].astype(BF16))

    assert n_tok + TRASH_TOK <= PAIR_TOK_MASK + 1 and (n_tok + TRASH_TOK) % MIX_TILE == 0
    hn_all = jnp.zeros(((n_tok + TRASH_TOK) * ROW_TILES, LANES), F32)
    h_s, hn_all, lg_s, s_dnc, s_dns, s_scc = _mixer(
        x_sample.reshape(n_s, D_MODEL), state_dn_conv[0], state_dn[0], state_sc_conv[0],
        cache_mem_k[0].reshape(bs, MEM_LEN, MEM_WIDTH), cache_mem_v[0].reshape(bs, MEM_LEN, MEM_WIDTH),
        weights, hn_all, n_seq=bs, seq_len=ts, nb=bs, tt=ts, row0=n_p)
    h_p, hn_all, lg_p, p_dnc, p_dns, p_scc = _mixer(
        x_prompt.reshape(n_p, D_MODEL),
        jnp.zeros((bp, DN_CONV - 1, QKV_WIDTH), F32),
        jnp.zeros((bp, DN_HEADS, DN_HEAD_DIM, DN_HEAD_DIM), F32),
        jnp.zeros((bp, SC_CONV - 1, SC_WIDTH), F32),
        mk2d.reshape(bp, MEM_LEN, MEM_WIDTH), mv2d.reshape(bp, MEM_LEN, MEM_WIDTH),
        weights, hn_all, n_seq=bp, seq_len=tp, nb=1, tt=MIX_TILE, row0=0)

    idx, gates, rank, counts = _router(jnp.concatenate([lg_p, lg_s], axis=0))
    counts = counts[0, :N_EXPERTS]
    n_blk_e = (counts + ROW_BLOCK - 1) // ROW_BLOCK
    blk_end = jnp.cumsum(n_blk_e)
    blk_start = blk_end - n_blk_e
    row_start = blk_start * ROW_BLOCK
    experts = jnp.arange(N_EXPERTS, dtype=jnp.int32)
    start_of = jnp.sum(jnp.where(idx[:, :TOP_K, None] == experts, row_start, 0), axis=-1)
    dest = (start_of + rank[:, :TOP_K]).reshape(n_tok * TOP_K)

    n_blocks = (n_tok * TOP_K) // ROW_BLOCK + N_EXPERTS
    bi = jnp.arange(n_blocks, dtype=jnp.int32)
    n_valid = blk_end[-1].astype(jnp.int32)
    bclip = jnp.minimum(bi, n_valid - 1)
    blk_e = jnp.sum((bclip[:, None] >= blk_end[None, :]).astype(jnp.int32), axis=1)
    n_valid = n_valid.reshape(1)

    slots = jnp.arange(n_blocks * ROW_BLOCK, dtype=jnp.int32)
    default_inv = n_tok + (slots & (TRASH_TOK - 1))
    pairs = jnp.arange(n_tok * TOP_K, dtype=jnp.int32)
    pair_code = ((pairs % TOP_K) << PAIR_TOK_BITS) | (pairs // TOP_K)
    inv = _invert(dest.astype(jnp.int32), pair_code, default_inv)

    y_pairs = _experts(blk_e, n_valid, inv, hn_all,
                       w_gate[0], b_gate[0].reshape(N_EXPERTS, 1, -1), w_up[0], b_up[0].reshape(N_EXPERTS, 1, -1),
                       w_down[0], b_down[0].reshape(N_EXPERTS, 1, -1), n_tok=n_tok)
    fw = final_norm_w.reshape(1, D_MODEL)
    y_p = _combine(gates, h_p, fw, y_pairs, tile0=0, plane_rows=n_tok + TRASH_TOK)
    y_s = _combine(gates, h_s, fw, y_pairs, tile0=n_p // TOK_TILE, plane_rows=n_tok + TRASH_TOK)

    return (y_p.reshape(bp, tp, D_MODEL), y_s.reshape(bs, ts, D_MODEL),
            p_dns[None], p_dnc[None], p_scc[None],
            mk2d.reshape(1, bp, MEM_LEN, MEM_HEADS, MEM_HEAD_DIM), mv2d.reshape(1, bp, MEM_LEN, MEM_HEADS, MEM_HEAD_DIM),
            s_dns[None], s_dnc[None], s_scc[None])
```

```python
import functools

import jax
import jax.numpy as jnp
from jax import lax
from jax.experimental import pallas as pl
from jax.experimental.pallas import tpu as pltpu

F32 = jnp.float32
BF16 = jnp.bfloat16

D_MODEL = 1024
CHUNK = 64
EPS = 1e-6
DN_HEADS = 4
DN_HEAD_DIM = 128
DN_WIDTH = DN_HEADS * DN_HEAD_DIM
QKV_WIDTH = 3 * DN_WIDTH
DN_CONV = 4
SC_WIDTH = 256
SC_CONV = 3
MEM_LEN = 256
MEM_HEADS = 4
MEM_HEAD_DIM = 64
MEM_WIDTH = MEM_HEADS * MEM_HEAD_DIM
N_EXPERTS = 32
TOP_K = 4
SWIGLU_ALPHA = 1.702
SWIGLU_LIMIT = 7.0

LANES = 128
SUBLANES = 8
ROW_TILES = D_MODEL // LANES
assert ROW_TILES == SUBLANES
CONV_PAD = 8

OFF_QKV = 0
OFF_DNG = OFF_QKV + QKV_WIDTH
OFF_SC = OFF_DNG + DN_WIDTH
OFF_MQ = OFF_SC + 3 * SC_WIDTH
OFF_GATE = OFF_MQ + MEM_WIDTH
OFF_AB = OFF_GATE + 3 * D_MODEL
IN_PERM_WIDTH = OFF_AB + LANES

MIX_TILE = 512
TOK_TILE = 256
ROW_BLOCK = 256
INVERT_UNROLL = 32
TRASH_TOK = 256
RING = 3
PAIR_TOK_BITS = 16
PAIR_TOK_MASK = (1 << PAIR_TOK_BITS) - 1
VMEM_LIMIT = 56 * 1024 * 1024
NEG_BIG = -1e30


def _dot(a, b):
    return jnp.dot(a, b, preferred_element_type=F32)


def _dot_nt(a, b):
    return lax.dot_general(a, b, (((1,), (1,)), ((), ())), preferred_element_type=F32)


def _dot_tn(a, b):
    return lax.dot_general(a, b, (((0,), (0,)), ((), ())), preferred_element_type=F32)


def _sigmoid(x):
    return 1.0 / (1.0 + jnp.exp(-x))


def _softplus(x):
    return jnp.maximum(x, 0.0) + jnp.log1p(jnp.exp(-jnp.abs(x)))


def _for_each(n, body):
    if n == 1:
        body(0)
    else:
        def step(i, carry):
            body(i)
            return carry
        lax.fori_loop(0, n, step, 0)


def _memkv_kernel(mem_ref, nw_ref, w_ref, k_ref, v_ref):
    x = mem_ref[...]
    xn = x * lax.rsqrt(jnp.mean(x * x, axis=-1, keepdims=True) + EPS) * nw_ref[...]
    kv = _dot(xn.astype(BF16), w_ref[...])
    k_ref[...] = kv[:, :MEM_WIDTH]
    v_ref[...] = kv[:, MEM_WIDTH:]


def _memkv(mem2d, norm_w, w_kv_bf16):
    rows = mem2d.shape[0]
    grid = rows // MEM_LEN
    return pl.pallas_call(
        _memkv_kernel,
        grid=(grid,),
        in_specs=[
            pl.BlockSpec((MEM_LEN, D_MODEL), lambda i: (i, 0)),
            pl.BlockSpec((1, D_MODEL), lambda i: (0, 0)),
            pl.BlockSpec((D_MODEL, 2 * MEM_WIDTH), lambda i: (0, 0)),
        ],
        out_specs=[
            pl.BlockSpec((MEM_LEN, MEM_WIDTH), lambda i: (i, 0)),
            pl.BlockSpec((MEM_LEN, MEM_WIDTH), lambda i: (i, 0)),
        ],
        out_shape=[jax.ShapeDtypeStruct((rows, MEM_WIDTH), F32)] * 2,
        name="memkv",
    )(mem2d, norm_w, w_kv_bf16)


def _unit_lower_inverse(a, eye, size):
    inv = eye - a
    power = a
    span = 2
    while span < size:
        power = _dot(power, power)
        inv = _dot(inv, eye + power)
        span *= 2
    return inv


def _mixer_kernel(x_ref, dnc_in, dns_in, scc_in, mk_ref, mv_ref, w_in, w_br, w_o, n1_ref, n2_ref, dcw_ref, scw_ref,
                  alog_ref, dtb_ref, dnw_ref, wr_ref, br_ref, hn_all_ref,
                  h_ref, hn_ref, lg_ref, dnc_out, dns_out, scc_out,
                  xp, scp, q_s, k_s, v_s, gb_s, o_s, mq_s, ysc_s, ymem_s, wv_s, wk_s, qk_s, qd_s, kd_s, cd_s,
                  *, nb, tt, chunk):
    t_idx = pl.program_id(1)
    rows = nb * tt
    n_chunk = tt // chunk

    @pl.when(t_idx == 0)
    def _():
        xp[:, CONV_PAD - (DN_CONV - 1):CONV_PAD, :] = dnc_in[...]
        scp[:, CONV_PAD - (SC_CONV - 1):CONV_PAD, :] = scc_in[...]
        dns_out[...] = dns_in[...]

    x = x_ref[...]
    xn = (x * lax.rsqrt(jnp.mean(x * x, axis=-1, keepdims=True) + EPS) * n1_ref[...]).astype(BF16)

    def proj(off, width):
        return _dot(xn, w_in[:, off:off + width])

    qkv_pre = proj(OFF_QKV, QKV_WIDTH)
    for s in range(nb):
        xp[s, CONV_PAD:CONV_PAD + tt, :] = qkv_pre[s * tt:(s + 1) * tt, :]
    sc = proj(OFF_SC, 3 * SC_WIDTH)
    sc_b = sc[:, :SC_WIDTH]
    sc_ch = sc[:, SC_WIDTH:2 * SC_WIDTH] * sc[:, 2 * SC_WIDTH:]
    for s in range(nb):
        scp[s, CONV_PAD:CONV_PAD + tt, :] = sc_ch[s * tt:(s + 1) * tt, :]
    mq_s[...] = proj(OFF_MQ, MEM_WIDTH)

    ab = proj(OFF_AB, LANES)
    lane = lax.broadcasted_iota(jnp.int32, (rows, LANES), 1)
    g_log = -jnp.exp(alog_ref[...]) * _softplus(ab + dtb_ref[...])
    gb_s[...] = jnp.where(lane < DN_HEADS, g_log, _sigmoid(ab))

    def conv_seq(s):
        base = CONV_PAD - (DN_CONV - 1)
        acc = dcw_ref[0:1, :] * xp[s, pl.ds(base, tt), :]
        for j in range(1, DN_CONV):
            acc = acc + dcw_ref[j:j + 1, :] * xp[s, pl.ds(base + j, tt), :]
        act = acc * _sigmoid(acc)
        r0 = pl.multiple_of(s * tt, tt)
        for hd in range(DN_HEADS):
            lo = hd * DN_HEAD_DIM
            qh = act[:, lo:lo + DN_HEAD_DIM]
            kh = act[:, DN_WIDTH + lo:DN_WIDTH + lo + DN_HEAD_DIM]
            q_s[pl.ds(r0, tt), lo:lo + DN_HEAD_DIM] = (
                qh * lax.rsqrt(jnp.sum(qh * qh, axis=-1, keepdims=True) + EPS) * (DN_HEAD_DIM ** -0.5))
            k_s[pl.ds(r0, tt), lo:lo + DN_HEAD_DIM] = (
                kh * lax.rsqrt(jnp.sum(kh * kh, axis=-1, keepdims=True) + EPS))
        v_s[pl.ds(r0, tt), :] = act[:, 2 * DN_WIDTH:]
        tail = xp[s, pl.ds(tt + base, DN_CONV - 1), :]
        dnc_out[s] = tail
        xp[s, pl.ds(base, DN_CONV - 1), :] = tail

        base2 = CONV_PAD - (SC_CONV - 1)
        acc2 = scw_ref[0:1, :] * scp[s, pl.ds(base2, tt), :]
        for j in range(1, SC_CONV):
            acc2 = acc2 + scw_ref[j:j + 1, :] * scp[s, pl.ds(base2 + j, tt), :]
        ysc_s[pl.ds(r0, tt), :] = acc2
        tail2 = scp[s, pl.ds(tt + base2, SC_CONV - 1), :]
        scc_out[s] = tail2
        scp[s, pl.ds(base2, SC_CONV - 1), :] = tail2

    _for_each(nb, conv_seq)

    hl = DN_HEADS * chunk
    shift = chunk.bit_length() - 1
    ri = lax.broadcasted_iota(jnp.int32, (hl, hl), 0)
    ci = lax.broadcasted_iota(jnp.int32, (hl, hl), 1)
    same_head = lax.shift_right_logical(ri, shift) == lax.shift_right_logical(ci, shift)
    causal = same_head & (ri >= ci)
    strict = same_head & (ri > ci)
    eye = (ri == ci).astype(F32)
    ri1 = lax.broadcasted_iota(jnp.int32, (chunk, chunk), 0)
    ci1 = lax.broadcasted_iota(jnp.int32, (chunk, chunk), 1)
    tril = (ri1 >= ci1).astype(F32)
    triu = (ri1 <= ci1).astype(F32)

    def stack_heads(ref, r0):
        return jnp.concatenate(
            [ref[r0:r0 + chunk, hd * DN_HEAD_DIM:(hd + 1) * DN_HEAD_DIM] for hd in range(DN_HEADS)], axis=0)

    for c in range(nb * n_chunk):
        r0 = c * chunk
        gb = gb_s[r0:r0 + chunk, :]
        cum = _dot(tril, gb)
        cum_t = _dot_tn(gb, triu)
        cc = jnp.concatenate([cum[:, hd:hd + 1] for hd in range(DN_HEADS)], axis=0)
        cr = jnp.concatenate([cum_t[hd:hd + 1, :] for hd in range(DN_HEADS)], axis=1)
        beta = jnp.concatenate([gb[:, DN_HEADS + hd:DN_HEADS + hd + 1] for hd in range(DN_HEADS)], axis=0)
        c_last = jnp.concatenate(
            [jnp.broadcast_to(cum[chunk - 1:chunk, hd:hd + 1], (chunk, 1)) for hd in range(DN_HEADS)], axis=0)
        qst, kst, vst = stack_heads(q_s, r0), stack_heads(k_s, r0), stack_heads(v_s, r0)
        decay = jnp.where(causal, jnp.exp(jnp.where(causal, cc - cr, 0.0)), 0.0)
        a_mat = jnp.where(strict, decay * _dot_nt(kst, kst), 0.0) * beta
        t_inv = _unit_lower_inverse(a_mat, eye, chunk)
        e_cum = jnp.exp(cc)
        w = _dot(t_inv, jnp.concatenate([beta * vst, (beta * e_cum) * kst], axis=1))
        wv_s[c] = w[:, :DN_HEAD_DIM]
        wk_s[c] = w[:, DN_HEAD_DIM:]
        qk_s[c] = _dot_nt(qst, kst) * decay
        qd_s[c] = e_cum * qst
        kd_s[c] = jnp.exp(c_last - cc) * kst
        cd_s[c] = jnp.broadcast_to(jnp.exp(c_last), (hl, DN_HEAD_DIM))

    for c in range(nb * n_chunk):
        s = c // n_chunk
        r0 = c * chunk
        states, us = [], []
        for hd in range(DN_HEADS):
            hr = slice(hd * chunk, (hd + 1) * chunk)
            state = dns_out[s, hd]
            states.append(state)
            us.append(wv_s[c, hr, :] - _dot(wk_s[c, hr, :], state))
        o_intra = _dot(qk_s[c], jnp.concatenate(us, axis=0))
        for hd in range(DN_HEADS):
            hr = slice(hd * chunk, (hd + 1) * chunk)
            lo = hd * DN_HEAD_DIM
            o_s[r0:r0 + chunk, lo:lo + DN_HEAD_DIM] = _dot(qd_s[c, hr, :], states[hd]) + o_intra[hr, :]
            dns_out[s, hd] = (cd_s[c, hd * chunk:hd * chunk + 1, :] * states[hd]
                              + _dot_tn(kd_s[c, hr, :], us[hd]))

    def attn_seq(s):
        r0 = pl.multiple_of(s * tt, tt)
        mq = mq_s[pl.ds(r0, tt), :]
        for hd in range(MEM_HEADS):
            lo = hd * MEM_HEAD_DIM
            qh = mq[:, lo:lo + MEM_HEAD_DIM].astype(BF16)
            kh = mk_ref[s, :, lo:lo + MEM_HEAD_DIM].astype(BF16)
            vh = mv_ref[s, :, lo:lo + MEM_HEAD_DIM].astype(BF16)
            sc_h = _dot_nt(qh, kh) * (MEM_HEAD_DIM ** -0.5)
            p = jnp.exp(sc_h - jnp.max(sc_h, axis=-1, keepdims=True))
            denom = jnp.sum(p, axis=-1, keepdims=True)
            ymem_s[pl.ds(r0, tt), lo:lo + MEM_HEAD_DIM] = _dot(p.astype(BF16), vh) / denom

    _for_each(nb, attn_seq)

    dn_gate = proj(OFF_DNG, DN_WIDTH)
    o_all = o_s[...]
    y_heads = []
    for hd in range(DN_HEADS):
        lo = hd * DN_HEAD_DIM
        oh = o_all[:, lo:lo + DN_HEAD_DIM]
        oh = oh * lax.rsqrt(jnp.mean(oh * oh, axis=-1, keepdims=True) + EPS) * dnw_ref[...]
        gh = dn_gate[:, lo:lo + DN_HEAD_DIM]
        y_heads.append(oh * (gh * _sigmoid(gh)))
    y_dn = jnp.concatenate(y_heads, axis=-1).astype(BF16)
    y_sc = (sc_b * ysc_s[...]).astype(BF16)
    y_mem = ymem_s[...].astype(BF16)

    merged = _sigmoid(proj(OFF_GATE, D_MODEL)) * _dot(y_dn, w_br[0:DN_WIDTH, :])
    merged = merged + _sigmoid(proj(OFF_GATE + D_MODEL, D_MODEL)) * _dot(y_sc, w_br[DN_WIDTH:DN_WIDTH + SC_WIDTH, :])
    merged = merged + _sigmoid(proj(OFF_GATE + 2 * D_MODEL, D_MODEL)) * _dot(y_mem, w_br[DN_WIDTH + SC_WIDTH:, :])
    h = x + _dot(merged.astype(BF16), w_o[...])
    h_ref[...] = h
    hn = h * lax.rsqrt(jnp.mean(h * h, axis=-1, keepdims=True) + EPS) * n2_ref[...]
    for c in range(ROW_TILES):
        hn_ref[pl.ds(c, rows, stride=ROW_TILES), :] = hn[:, c * LANES:(c + 1) * LANES]
    lg_ref[...] = _dot(hn, wr_ref[...]) + br_ref[...]


def _mixer(x2d, dnc_in, dns_in, scc_in, mk, mv, weights, hn_all, *, n_seq, seq_len, nb, tt, row0):
    chunk = CHUNK if seq_len % CHUNK == 0 else seq_len
    rows = nb * tt
    n_t = seq_len // tt
    total_rows = n_seq * seq_len
    n_chunks = rows // chunk
    hl = DN_HEADS * chunk
    grid = (n_seq // nb, n_t)
    const = lambda b, t: (0, 0)
    seq3 = lambda b, t: (b, 0, 0)
    assert row0 % rows == 0

    def tok(b, t):
        return (b * n_t + t, 0)

    def tok_all(b, t):
        return (row0 // rows + b * n_t + t, 0)

    (w_in, w_br, w_o, n1, n2, dcw, scw, alog, dtb, dnw, wr, br) = weights
    in_specs = [
        pl.BlockSpec((rows, D_MODEL), tok),
        pl.BlockSpec((nb, DN_CONV - 1, QKV_WIDTH), seq3),
        pl.BlockSpec((nb, DN_HEADS, DN_HEAD_DIM, DN_HEAD_DIM), lambda b, t: (b, 0, 0, 0)),
        pl.BlockSpec((nb, SC_CONV - 1, SC_WIDTH), seq3),
        pl.BlockSpec((nb, MEM_LEN, MEM_WIDTH), seq3),
        pl.BlockSpec((nb, MEM_LEN, MEM_WIDTH), seq3),
        pl.BlockSpec(w_in.shape, const, pipeline_mode=pl.Buffered(1)),
        pl.BlockSpec(w_br.shape, const, pipeline_mode=pl.Buffered(1)),
        pl.BlockSpec(w_o.shape, const, pipeline_mode=pl.Buffered(1)),
        pl.BlockSpec(n1.shape, const),
        pl.BlockSpec(n2.shape, const),
        pl.BlockSpec(dcw.shape, const),
        pl.BlockSpec(scw.shape, const),
        pl.BlockSpec(alog.shape, const),
        pl.BlockSpec(dtb.shape, const),
        pl.BlockSpec(dnw.shape, const),
        pl.BlockSpec(wr.shape, const),
        pl.BlockSpec(br.shape, const),
        pl.BlockSpec(memory_space=pl.ANY),
    ]
    out_shape = [
        jax.ShapeDtypeStruct((total_rows, D_MODEL), F32),
        jax.ShapeDtypeStruct(hn_all.shape, F32),
        jax.ShapeDtypeStruct((total_rows, LANES), F32),
        jax.ShapeDtypeStruct((n_seq, DN_CONV - 1, QKV_WIDTH), F32),
        jax.ShapeDtypeStruct((n_seq, DN_HEADS, DN_HEAD_DIM, DN_HEAD_DIM), F32),
        jax.ShapeDtypeStruct((n_seq, SC_CONV - 1, SC_WIDTH), F32),
    ]
    out_specs = [
        pl.BlockSpec((rows, D_MODEL), tok),
        pl.BlockSpec((rows * ROW_TILES, LANES), tok_all),
        pl.BlockSpec((rows, LANES), tok),
        pl.BlockSpec((nb, DN_CONV - 1, QKV_WIDTH), seq3),
        pl.BlockSpec((nb, DN_HEADS, DN_HEAD_DIM, DN_HEAD_DIM), lambda b, t: (b, 0, 0, 0)),
        pl.BlockSpec((nb, SC_CONV - 1, SC_WIDTH), seq3),
    ]
    args = [x2d, dnc_in, dns_in, scc_in, mk, mv, w_in, w_br, w_o, n1, n2, dcw, scw, alog, dtb, dnw, wr, br, hn_all]
    scratch = [
        pltpu.VMEM((nb, CONV_PAD + tt, QKV_WIDTH), F32),
        pltpu.VMEM((nb, CONV_PAD + tt, SC_WIDTH), F32),
        pltpu.VMEM((rows, DN_WIDTH), F32),
        pltpu.VMEM((rows, DN_WIDTH), F32),
        pltpu.VMEM((rows, DN_WIDTH), F32),
        pltpu.VMEM((rows, LANES), F32),
        pltpu.VMEM((rows, DN_WIDTH), F32),
        pltpu.VMEM((rows, MEM_WIDTH), F32),
        pltpu.VMEM((rows, SC_WIDTH), F32),
        pltpu.VMEM((rows, MEM_WIDTH), F32),
        pltpu.VMEM((n_chunks, hl, DN_HEAD_DIM), F32),
        pltpu.VMEM((n_chunks, hl, DN_HEAD_DIM), F32),
        pltpu.VMEM((n_chunks, hl, hl), F32),
        pltpu.VMEM((n_chunks, hl, DN_HEAD_DIM), F32),
        pltpu.VMEM((n_chunks, hl, DN_HEAD_DIM), F32),
        pltpu.VMEM((n_chunks, hl, DN_HEAD_DIM), F32),
    ]
    return pl.pallas_call(
        functools.partial(_mixer_kernel, nb=nb, tt=tt, chunk=chunk),
        grid=grid,
        in_specs=in_specs,
        out_specs=out_specs,
        out_shape=out_shape,
        scratch_shapes=scratch,
        input_output_aliases={len(args) - 1: 1},
        compiler_params=pltpu.CompilerParams(
            dimension_semantics=("arbitrary", "arbitrary"), vmem_limit_bytes=VMEM_LIMIT),
        name="mixer",
    )(*args)


def _router_kernel(lg_ref, idx_ref, gate_ref, rank_ref, cnt_ref, carry):
    i = pl.program_id(0)

    @pl.when(i == 0)
    def _():
        carry[...] = jnp.zeros_like(carry)

    work = lg_ref[...]
    tm = work.shape[0]
    lane = lax.broadcasted_iota(jnp.int32, (tm, LANES), 1).astype(F32)
    idxs, vals = [], []
    for _ in range(TOP_K):
        m = jnp.max(work, axis=-1, keepdims=True)
        ik = jnp.min(jnp.where(work == m, lane, float(LANES)), axis=-1, keepdims=True)
        idxs.append(ik)
        vals.append(m)
        work = jnp.where(lane == ik, -jnp.inf, work)
    exps = [jnp.exp(v - vals[0]) for v in vals]
    denom = exps[0] + exps[1] + exps[2] + exps[3]
    hot = jnp.zeros((tm, LANES), F32)
    for ik in idxs:
        hot = hot + (lane == ik).astype(F32)
    ri = lax.broadcasted_iota(jnp.int32, (tm, tm), 0)
    ci = lax.broadcasted_iota(jnp.int32, (tm, tm), 1)
    before = (ri > ci).astype(BF16)
    prefix = _dot(before, hot.astype(BF16)) + carry[...]
    idx_out = jnp.zeros((tm, LANES), F32)
    gate_out = jnp.zeros((tm, LANES), F32)
    rank_out = jnp.zeros((tm, LANES), F32)
    for k in range(TOP_K):
        rk = jnp.sum(jnp.where(lane == idxs[k], prefix, 0.0), axis=-1, keepdims=True)
        idx_out = jnp.where(lane == k, idxs[k], idx_out)
        gate_out = jnp.where(lane == k, exps[k] / denom, gate_out)
        rank_out = jnp.where(lane == k, rk, rank_out)
    idx_ref[...] = idx_out.astype(jnp.int32)
    gate_ref[...] = gate_out
    rank_ref[...] = rank_out.astype(jnp.int32)
    carry[...] = carry[...] + jnp.sum(hot, axis=0, keepdims=True)
    cnt_ref[...] = carry[...].astype(jnp.int32)


def _router(logits):
    n_tok = logits.shape[0]
    tile = lambda i: (i, 0)
    return pl.pallas_call(
        _router_kernel,
        grid=(n_tok // TOK_TILE,),
        in_specs=[pl.BlockSpec((TOK_TILE, LANES), tile)],
        out_specs=[pl.BlockSpec((TOK_TILE, LANES), tile)] * 3 + [pl.BlockSpec((1, LANES), lambda i: (0, 0))],
        out_shape=[
            jax.ShapeDtypeStruct((n_tok, LANES), jnp.int32),
            jax.ShapeDtypeStruct((n_tok, LANES), F32),
            jax.ShapeDtypeStruct((n_tok, LANES), jnp.int32),
            jax.ShapeDtypeStruct((1, LANES), jnp.int32),
        ],
        scratch_shapes=[pltpu.VMEM((1, LANES), F32)],
        compiler_params=pltpu.CompilerParams(dimension_semantics=("arbitrary",)),
        name="router",
    )(logits)


def _invert_kernel(dest_ref, default_ref, inv_ref, sem):
    i = pl.program_id(0)

    @pl.when(i == 0)
    def _():
        cp = pltpu.make_async_copy(default_ref, inv_ref, sem)
        cp.start()
        cp.wait()

    tok_per_trip = INVERT_UNROLL // TOP_K

    def body(j, c):
        tok0 = i * TOK_TILE + j * tok_per_trip
        for d in range(INVERT_UNROLL):
            code = tok0 + (((d % TOP_K) << PAIR_TOK_BITS) + d // TOP_K)
            inv_ref[dest_ref[j * INVERT_UNROLL + d]] = code
        return c

    lax.fori_loop(0, TOK_TILE * TOP_K // INVERT_UNROLL, body, 0)


def _invert(dest_flat, default_inv):
    n_pairs = TOK_TILE * TOP_K
    assert INVERT_UNROLL % TOP_K == 0 and n_pairs % INVERT_UNROLL == 0
    return pl.pallas_call(
        _invert_kernel,
        grid=(dest_flat.shape[0] // n_pairs,),
        in_specs=[
            pl.BlockSpec((n_pairs,), lambda i: (i,), memory_space=pltpu.SMEM),
            pl.BlockSpec(memory_space=pl.ANY),
        ],
        out_specs=pl.BlockSpec(memory_space=pltpu.SMEM),
        out_shape=jax.ShapeDtypeStruct(default_inv.shape, jnp.int32),
        scratch_shapes=[pltpu.SemaphoreType.DMA(())],
        compiler_params=pltpu.CompilerParams(dimension_semantics=("arbitrary",)),
        name="invert",
    )(dest_flat, default_inv)


def _experts_kernel(blk_e, n_valid, inv, hn_ref, wg_ref, bg_ref, wu_ref, bu_ref, wd_ref, bd_ref, yp_ref,
                    wg_bf, wu_bf, wd_bf, xbuf0, xbuf1, xbuf2, ybuf0, ybuf1, ybuf2, zeros, gsem, ssem, zsem,
                    *, n_tok):
    i = pl.program_id(0)
    nv = n_valid[0]
    phase = lax.rem(i, RING)
    xbuf = (xbuf0, xbuf1, xbuf2)
    ybuf = (ybuf0, ybuf1, ybuf2)
    plane_rows = n_tok + TRASH_TOK

    def tile_rows(row, n=1):
        return pl.ds(pl.multiple_of(row * ROW_TILES, ROW_TILES), n * ROW_TILES)

    def gather(block, s):
        base = block * ROW_BLOCK
        for r in range(ROW_BLOCK):
            tok = inv[base + r] & PAIR_TOK_MASK
            pltpu.make_async_copy(
                hn_ref.at[tile_rows(tok)], xbuf[s].at[tile_rows(r)], gsem.at[s]).start(priority=0)

    def scatter(block, s):
        base = block * ROW_BLOCK
        for r in range(ROW_BLOCK):
            pair = inv[base + r]
            row = lax.shift_right_logical(pair, PAIR_TOK_BITS) * plane_rows + (pair & PAIR_TOK_MASK)
            pltpu.make_async_copy(
                ybuf[s].at[tile_rows(r)], yp_ref.at[tile_rows(row)], ssem.at[s]).start(priority=1)

    def wait_gather(s):
        pltpu.make_async_copy(hn_ref.at[tile_rows(0, ROW_BLOCK)], xbuf[s], gsem.at[s]).wait()

    def wait_scatter(s):
        pltpu.make_async_copy(ybuf[s], yp_ref.at[tile_rows(0, ROW_BLOCK)], ssem.at[s]).wait()

    @pl.when(i == 0)
    def _():
        zeros[...] = jnp.zeros_like(zeros)
        for k in range(TOP_K):
            cp = pltpu.make_async_copy(zeros, yp_ref.at[tile_rows(k * plane_rows + n_tok, TRASH_TOK)], zsem)
            cp.start()
            cp.wait()
        gather(0, 0)
        gather(jnp.minimum(1, nv - 1), 1)

    e = blk_e[i]
    prev = blk_e[jnp.maximum(i - 1, 0)]

    @pl.when(((i == 0) | (e != prev)) & (i < nv))
    def _():
        wg_bf[...] = wg_ref[0].astype(BF16)
        wu_bf[...] = wu_ref[0].astype(BF16)
        wd_bf[...] = wd_ref[0].astype(BF16)

    def step(slot, scatter_previous):
        ahead, behind = (slot + 2) % RING, (slot - 1) % RING
        wait_gather(slot)
        gather(jnp.minimum(i + 2, nv - 1), ahead)
        if scatter_previous:
            scatter(i - 1, behind)
        x = jnp.concatenate(
            [xbuf[slot][pl.ds(c, ROW_BLOCK, stride=ROW_TILES), :] for c in range(ROW_TILES)], axis=1).astype(BF16)
        gl = jnp.minimum(_dot(x, wg_bf[...]) + bg_ref[0], SWIGLU_LIMIT)
        ul = jnp.clip(_dot(x, wu_bf[...]) + bu_ref[0], -SWIGLU_LIMIT, SWIGLU_LIMIT)
        act = (ul + 1.0) * (gl * _sigmoid(SWIGLU_ALPHA * gl))
        y = _dot(act.astype(BF16), wd_bf[...]) + bd_ref[0]
        for c in range(ROW_TILES):
            ybuf[slot][pl.ds(c, ROW_BLOCK, stride=ROW_TILES), :] = y[:, c * LANES:(c + 1) * LANES]

    def finish(slot):
        ahead, behind = (slot + 2) % RING, (slot - 1) % RING
        scatter(i, slot)
        wait_gather((slot + 1) % RING)
        wait_gather(ahead)

        @pl.when(i >= 2)
        def _():
            wait_scatter((slot - 2) % RING)

        @pl.when(i >= 1)
        def _():
            wait_scatter(behind)

        wait_scatter(slot)

    @pl.when((i == 0) & (i < nv))
    def _():
        step(0, False)

    for slot in range(RING):
        mine = (phase == slot) & (i < nv)

        @pl.when(mine & (i >= RING))
        def _():
            wait_scatter(slot)

        @pl.when(mine & (i > 0))
        def _():
            step(slot, True)

        @pl.when(mine & (i == nv - 1))
        def _():
            finish(slot)


def _experts(blk_e, n_valid, inv, hn_all, w_gate, b_gate, w_up, b_up, w_down, b_down, *, n_tok):
    n_blocks = blk_e.shape[0]
    wsel = lambda i, be, nv, iv: (be[i], 0, 0)
    d_ff = w_gate.shape[-1]
    grid_spec = pltpu.PrefetchScalarGridSpec(
        num_scalar_prefetch=3,
        grid=(n_blocks,),
        in_specs=[
            pl.BlockSpec(memory_space=pl.ANY),
            pl.BlockSpec((1, D_MODEL, d_ff), wsel),
            pl.BlockSpec((1, 1, d_ff), wsel),
            pl.BlockSpec((1, D_MODEL, d_ff), wsel),
            pl.BlockSpec((1, 1, d_ff), wsel),
            pl.BlockSpec((1, d_ff, D_MODEL), wsel),
            pl.BlockSpec((1, 1, D_MODEL), wsel),
        ],
        out_specs=pl.BlockSpec(memory_space=pl.ANY),
        scratch_shapes=[
            pltpu.VMEM((D_MODEL, d_ff), BF16),
            pltpu.VMEM((D_MODEL, d_ff), BF16),
            pltpu.VMEM((d_ff, D_MODEL), BF16),
        ] + [pltpu.VMEM((ROW_BLOCK * ROW_TILES, LANES), F32)] * (2 * RING) + [
            pltpu.VMEM((TRASH_TOK * ROW_TILES, LANES), F32),
            pltpu.SemaphoreType.DMA((RING,)),
            pltpu.SemaphoreType.DMA((RING,)),
            pltpu.SemaphoreType.DMA(()),
        ],
    )
    return pl.pallas_call(
        functools.partial(_experts_kernel, n_tok=n_tok),
        grid_spec=grid_spec,
        out_shape=jax.ShapeDtypeStruct((TOP_K * (n_tok + TRASH_TOK) * ROW_TILES, LANES), F32),
        compiler_params=pltpu.CompilerParams(
            dimension_semantics=("arbitrary",), vmem_limit_bytes=VMEM_LIMIT),
        name="experts",
    )(blk_e, n_valid, inv, hn_all, w_gate, b_gate, w_up, b_up, w_down, b_down)


def _combine_kernel(gate_ref, h_ref, fw_ref, *refs):
    y_ref = refs[TOP_K]
    gate = gate_ref[...]
    acc = h_ref[...]
    for k in range(TOP_K):
        y_k = jnp.concatenate(
            [refs[k][pl.ds(c, TOK_TILE, stride=ROW_TILES), :] for c in range(ROW_TILES)], axis=1)
        acc = acc + gate[:, k:k + 1] * y_k
    y_ref[...] = acc * lax.rsqrt(jnp.mean(acc * acc, axis=-1, keepdims=True) + EPS) * fw_ref[...]


def _combine(gates, h_group, final_w, y_pairs, *, tile0, plane_rows):
    n_tiles = h_group.shape[0] // TOK_TILE
    plane_tiles = plane_rows // TOK_TILE

    def plane(k):
        return pl.BlockSpec((TOK_TILE * ROW_TILES, LANES), lambda i: (k * plane_tiles + tile0 + i, 0))

    return pl.pallas_call(
        _combine_kernel,
        grid=(n_tiles,),
        in_specs=[
            pl.BlockSpec((TOK_TILE, LANES), lambda i: (tile0 + i, 0)),
            pl.BlockSpec((TOK_TILE, D_MODEL), lambda i: (i, 0)),
            pl.BlockSpec((1, D_MODEL), lambda i: (0, 0)),
        ] + [plane(k) for k in range(TOP_K)],
        out_specs=pl.BlockSpec((TOK_TILE, D_MODEL), lambda i: (i, 0)),
        out_shape=jax.ShapeDtypeStruct((n_tiles * TOK_TILE, D_MODEL), F32),
        compiler_params=pltpu.CompilerParams(dimension_semantics=("arbitrary",)),
        name="combine",
    )(gates, h_group, final_w, *([y_pairs] * TOP_K))


def _permute_w_in(w):
    o_a = QKV_WIDTH
    o_g = o_a + 2 * DN_HEADS
    o_rest = o_g + DN_WIDTH
    pad = jnp.zeros((w.shape[0], LANES - 2 * DN_HEADS), BF16)
    pieces = [w[:, :o_a], w[:, o_g:o_rest], w[:, o_rest:], w[:, o_a:o_g]]
    return jnp.concatenate([p.astype(BF16) for p in pieces] + [pad], axis=1)


def _lane_row(v, fill=0.0):
    return jnp.concatenate([v.astype(F32), jnp.full((LANES - v.shape[0],), fill, F32)]).reshape(1, LANES)


def kernel(x_prompt, x_sample, mem_prompt, state_dn, state_dn_conv, state_sc_conv, cache_mem_k, cache_mem_v, w_in, dn_conv_w, dn_A_log, dn_dt_bias, dn_norm_w, sc_conv_w, mem_norm_w, w_mem_kv, w_br, w_o, norm1_w, norm2_w, w_router, b_router, w_gate, b_gate, w_up, b_up, w_down, b_down, final_norm_w):
    assert w_in.shape[0] == 1, "one layer"
    bp, tp, _ = x_prompt.shape
    bs, ts, _ = x_sample.shape
    n_p, n_s = bp * tp, bs * ts
    n_tok = n_p + n_s
    assert n_p % TOK_TILE == 0 and n_s % TOK_TILE == 0 and tp % MIX_TILE == 0

    weights = (
        _permute_w_in(w_in[0]),
        w_br[0].astype(BF16),
        w_o[0].astype(BF16),
        norm1_w[0].reshape(1, D_MODEL),
        norm2_w[0].reshape(1, D_MODEL),
        dn_conv_w[0],
        sc_conv_w[0],
        _lane_row(dn_A_log[0]),
        _lane_row(dn_dt_bias[0]),
        dn_norm_w[0].reshape(1, DN_HEAD_DIM),
        jnp.concatenate([w_router[0], jnp.zeros((D_MODEL, LANES - N_EXPERTS), F32)], axis=1),
        _lane_row(b_router[0], NEG_BIG),
    )

    mk2d, mv2d = _memkv(mem_prompt.reshape(bp * MEM_LEN, D_MODEL), mem_norm_w[0].reshape(1, D_MODEL),
                        w_mem_kv[0].astype(BF16))

    assert n_tok + TRASH_TOK <= PAIR_TOK_MASK + 1 and (n_tok + TRASH_TOK) % MIX_TILE == 0
    hn_all = jnp.zeros(((n_tok + TRASH_TOK) * ROW_TILES, LANES), F32)
    h_s, hn_all, lg_s, s_dnc, s_dns, s_scc = _mixer(
        x_sample.reshape(n_s, D_MODEL), state_dn_conv[0], state_dn[0], state_sc_conv[0],
        cache_mem_k[0].reshape(bs, MEM_LEN, MEM_WIDTH), cache_mem_v[0].reshape(bs, MEM_LEN, MEM_WIDTH),
        weights, hn_all, n_seq=bs, seq_len=ts, nb=bs, tt=ts, row0=n_p)
    h_p, hn_all, lg_p, p_dnc, p_dns, p_scc = _mixer(
        x_prompt.reshape(n_p, D_MODEL),
        jnp.zeros((bp, DN_CONV - 1, QKV_WIDTH), F32),
        jnp.zeros((bp, DN_HEADS, DN_HEAD_DIM, DN_HEAD_DIM), F32),
        jnp.zeros((bp, SC_CONV - 1, SC_WIDTH), F32),
        mk2d.reshape(bp, MEM_LEN, MEM_WIDTH), mv2d.reshape(bp, MEM_LEN, MEM_WIDTH),
        weights, hn_all, n_seq=bp, seq_len=tp, nb=1, tt=MIX_TILE, row0=0)

    idx, gates, rank, counts = _router(jnp.concatenate([lg_p, lg_s], axis=0))
    counts = counts[0, :N_EXPERTS]
    n_blk_e = (counts + ROW_BLOCK - 1) // ROW_BLOCK
    blk_end = jnp.cumsum(n_blk_e)
    blk_start = blk_end - n_blk_e
    row_start = blk_start * ROW_BLOCK
    experts = jnp.arange(N_EXPERTS, dtype=jnp.int32)
    start_of = jnp.sum(jnp.where(idx[:, :TOP_K, None] == experts, row_start, 0), axis=-1)
    dest = (start_of + rank[:, :TOP_K]).reshape(n_tok * TOP_K)

    n_blocks = (n_tok * TOP_K) // ROW_BLOCK + N_EXPERTS
    bi = jnp.arange(n_blocks, dtype=jnp.int32)
    n_valid = blk_end[-1].astype(jnp.int32)
    bclip = jnp.minimum(bi, n_valid - 1)
    blk_e = jnp.sum((bclip[:, None] >= blk_end[None, :]).astype(jnp.int32), axis=1)
    n_valid = n_valid.reshape(1)

    slots = jnp.arange(n_blocks * ROW_BLOCK, dtype=jnp.int32)
    default_inv = n_tok + (slots & (TRASH_TOK - 1))
    inv = _invert(dest.astype(jnp.int32), default_inv)

    y_pairs = _experts(blk_e, n_valid, inv, hn_all,
                       w_gate[0], b_gate[0].reshape(N_EXPERTS, 1, -1), w_up[0], b_up[0].reshape(N_EXPERTS, 1, -1),
                       w_down[0], b_down[0].reshape(N_EXPERTS, 1, -1), n_tok=n_tok)
    fw = final_norm_w.reshape(1, D_MODEL)
    y_p = _combine(gates, h_p, fw, y_pairs, tile0=0, plane_rows=n_tok + TRASH_TOK)
    y_s = _combine(gates, h_s, fw, y_pairs, tile0=n_p // TOK_TILE, plane_rows=n_tok + TRASH_TOK)

    return (y_p.reshape(bp, tp, D_MODEL), y_s.reshape(bs, ts, D_MODEL),
            p_dns[None], p_dnc[None], p_scc[None],
            mk2d.reshape(1, bp, MEM_LEN, MEM_HEADS, MEM_HEAD_DIM), mv2d.reshape(1, bp, MEM_LEN, MEM_HEADS, MEM_HEAD_DIM),
            s_dns[None], s_dnc[None], s_scc[None])
```

```python
import functools

import jax
import jax.numpy as jnp
from jax import lax
from jax.experimental import pallas as pl
from jax.experimental.pallas import tpu as pltpu

F32 = jnp.float32
BF16 = jnp.bfloat16

D_MODEL = 1024
CHUNK = 64
EPS = 1e-6
DN_HEADS = 4
DN_HEAD_DIM = 128
DN_WIDTH = DN_HEADS * DN_HEAD_DIM
QKV_WIDTH = 3 * DN_WIDTH
DN_CONV = 4
SC_WIDTH = 256
SC_CONV = 3
MEM_LEN = 256
MEM_HEADS = 4
MEM_HEAD_DIM = 64
MEM_WIDTH = MEM_HEADS * MEM_HEAD_DIM
N_EXPERTS = 32
TOP_K = 4
SWIGLU_ALPHA = 1.702
SWIGLU_LIMIT = 7.0

LANES = 128
SUBLANES = 8
ROW_TILES = D_MODEL // LANES
assert ROW_TILES == SUBLANES
CONV_PAD = 8

OFF_QKV = 0
OFF_DNG = OFF_QKV + QKV_WIDTH
OFF_SC = OFF_DNG + DN_WIDTH
OFF_MQ = OFF_SC + 3 * SC_WIDTH
OFF_GATE = OFF_MQ + MEM_WIDTH
OFF_AB = OFF_GATE + 3 * D_MODEL
IN_PERM_WIDTH = OFF_AB + LANES

MIX_TILE = 512
TOK_TILE = 256
ROW_BLOCK = 256
INVERT_UNROLL = 32
TRASH_TOK = 256
RING = 3
PAIR_TOK_BITS = 16
PAIR_TOK_MASK = (1 << PAIR_TOK_BITS) - 1
VMEM_LIMIT = 56 * 1024 * 1024
NEG_BIG = -1e30


def _dot(a, b):
    return jnp.dot(a, b, preferred_element_type=F32)


def _dot_nt(a, b):
    return lax.dot_general(a, b, (((1,), (1,)), ((), ())), preferred_element_type=F32)


def _dot_tn(a, b):
    return lax.dot_general(a, b, (((0,), (0,)), ((), ())), preferred_element_type=F32)


def _sigmoid(x):
    return 1.0 / (1.0 + jnp.exp(-x))


def _softplus(x):
    return jnp.maximum(x, 0.0) + jnp.log1p(jnp.exp(-jnp.abs(x)))


def _for_each(n, body):
    if n == 1:
        body(0)
    else:
        def step(i, carry):
            body(i)
            return carry
        lax.fori_loop(0, n, step, 0)


def _memkv_kernel(mem_ref, nw_ref, w_ref, k_ref, v_ref):
    x = mem_ref[...]
    xn = x * lax.rsqrt(jnp.mean(x * x, axis=-1, keepdims=True) + EPS) * nw_ref[...]
    kv = _dot(xn.astype(BF16), w_ref[...])
    k_ref[...] = kv[:, :MEM_WIDTH]
    v_ref[...] = kv[:, MEM_WIDTH:]


def _memkv(mem2d, norm_w, w_kv_bf16):
    rows = mem2d.shape[0]
    grid = rows // MEM_LEN
    return pl.pallas_call(
        _memkv_kernel,
        grid=(grid,),
        in_specs=[
            pl.BlockSpec((MEM_LEN, D_MODEL), lambda i: (i, 0)),
            pl.BlockSpec((1, D_MODEL), lambda i: (0, 0)),
            pl.BlockSpec((D_MODEL, 2 * MEM_WIDTH), lambda i: (0, 0)),
        ],
        out_specs=[
            pl.BlockSpec((MEM_LEN, MEM_WIDTH), lambda i: (i, 0)),
            pl.BlockSpec((MEM_LEN, MEM_WIDTH), lambda i: (i, 0)),
        ],
        out_shape=[jax.ShapeDtypeStruct((rows, MEM_WIDTH), F32)] * 2,
        name="memkv",
    )(mem2d, norm_w, w_kv_bf16)


def _unit_lower_inverse(a, eye, size):
    inv = eye - a
    power = a
    span = 2
    while span < size:
        power = _dot(power, power)
        inv = _dot(inv, eye + power)
        span *= 2
    return inv


def _mixer_kernel(x_ref, dnc_in, dns_in, scc_in, mk_ref, mv_ref, w_qkv, w_rest, w_ab, w_br, w_o, n1_ref, n2_ref,
                  dcw_ref, scw_ref,
                  alog_ref, dtb_ref, dnw_ref, wr_ref, br_ref, hn_all_ref,
                  h_ref, hn_ref, lg_ref, dnc_out, dns_out, scc_out,
                  xp, scp, q_s, k_s, v_s, gb_s, o_s, mq_s, ysc_s, ymem_s, wv_s, wk_s, qk_s, qd_s, kd_s, cd_s,
                  *, nb, tt, chunk):
    t_idx = pl.program_id(1)
    rows = nb * tt
    n_chunk = tt // chunk

    @pl.when(t_idx == 0)
    def _():
        xp[:, CONV_PAD - (DN_CONV - 1):CONV_PAD, :] = dnc_in[...]
        scp[:, CONV_PAD - (SC_CONV - 1):CONV_PAD, :] = scc_in[...]
        dns_out[...] = dns_in[...]

    x = x_ref[...]
    xn = (x * lax.rsqrt(jnp.mean(x * x, axis=-1, keepdims=True) + EPS) * n1_ref[...]).astype(BF16)

    def proj(off, width):
        if off >= OFF_AB:
            w = w_ab[:, off - OFF_AB:off - OFF_AB + width]
        elif off >= OFF_DNG:
            w = w_rest[:, off - OFF_DNG:off - OFF_DNG + width]
        else:
            w = w_qkv[:, off:off + width]
        return _dot(xn, w)

    qkv_pre = proj(OFF_QKV, QKV_WIDTH)
    for s in range(nb):
        xp[s, CONV_PAD:CONV_PAD + tt, :] = qkv_pre[s * tt:(s + 1) * tt, :]
    sc = proj(OFF_SC, 3 * SC_WIDTH)
    sc_b = sc[:, :SC_WIDTH]
    sc_ch = sc[:, SC_WIDTH:2 * SC_WIDTH] * sc[:, 2 * SC_WIDTH:]
    for s in range(nb):
        scp[s, CONV_PAD:CONV_PAD + tt, :] = sc_ch[s * tt:(s + 1) * tt, :]
    mq_s[...] = proj(OFF_MQ, MEM_WIDTH)

    ab = proj(OFF_AB, LANES)
    lane = lax.broadcasted_iota(jnp.int32, (rows, LANES), 1)
    g_log = -jnp.exp(alog_ref[...]) * _softplus(ab + dtb_ref[...])
    gb_s[...] = jnp.where(lane < DN_HEADS, g_log, _sigmoid(ab))

    def conv_seq(s):
        base = CONV_PAD - (DN_CONV - 1)
        acc = dcw_ref[0:1, :] * xp[s, pl.ds(base, tt), :]
        for j in range(1, DN_CONV):
            acc = acc + dcw_ref[j:j + 1, :] * xp[s, pl.ds(base + j, tt), :]
        act = acc * _sigmoid(acc)
        r0 = pl.multiple_of(s * tt, tt)
        for hd in range(DN_HEADS):
            lo = hd * DN_HEAD_DIM
            qh = act[:, lo:lo + DN_HEAD_DIM]
            kh = act[:, DN_WIDTH + lo:DN_WIDTH + lo + DN_HEAD_DIM]
            q_s[pl.ds(r0, tt), lo:lo + DN_HEAD_DIM] = (
                qh * lax.rsqrt(jnp.sum(qh * qh, axis=-1, keepdims=True) + EPS) * (DN_HEAD_DIM ** -0.5))
            k_s[pl.ds(r0, tt), lo:lo + DN_HEAD_DIM] = (
                kh * lax.rsqrt(jnp.sum(kh * kh, axis=-1, keepdims=True) + EPS))
        v_s[pl.ds(r0, tt), :] = act[:, 2 * DN_WIDTH:]
        tail = xp[s, pl.ds(tt + base, DN_CONV - 1), :]
        dnc_out[s] = tail
        xp[s, pl.ds(base, DN_CONV - 1), :] = tail

        base2 = CONV_PAD - (SC_CONV - 1)
        acc2 = scw_ref[0:1, :] * scp[s, pl.ds(base2, tt), :]
        for j in range(1, SC_CONV):
            acc2 = acc2 + scw_ref[j:j + 1, :] * scp[s, pl.ds(base2 + j, tt), :]
        ysc_s[pl.ds(r0, tt), :] = acc2
        tail2 = scp[s, pl.ds(tt + base2, SC_CONV - 1), :]
        scc_out[s] = tail2
        scp[s, pl.ds(base2, SC_CONV - 1), :] = tail2

    _for_each(nb, conv_seq)

    hl = DN_HEADS * chunk
    shift = chunk.bit_length() - 1
    ri = lax.broadcasted_iota(jnp.int32, (hl, hl), 0)
    ci = lax.broadcasted_iota(jnp.int32, (hl, hl), 1)
    same_head = lax.shift_right_logical(ri, shift) == lax.shift_right_logical(ci, shift)
    causal = same_head & (ri >= ci)
    strict = same_head & (ri > ci)
    eye = (ri == ci).astype(F32)
    ri1 = lax.broadcasted_iota(jnp.int32, (chunk, chunk), 0)
    ci1 = lax.broadcasted_iota(jnp.int32, (chunk, chunk), 1)
    tril = (ri1 >= ci1).astype(F32)
    triu = (ri1 <= ci1).astype(F32)

    def stack_heads(ref, r0):
        return jnp.concatenate(
            [ref[r0:r0 + chunk, hd * DN_HEAD_DIM:(hd + 1) * DN_HEAD_DIM] for hd in range(DN_HEADS)], axis=0)

    for c in range(nb * n_chunk):
        r0 = c * chunk
        gb = gb_s[r0:r0 + chunk, :]
        cum = _dot(tril, gb)
        cum_t = _dot_tn(gb, triu)
        cc = jnp.concatenate([cum[:, hd:hd + 1] for hd in range(DN_HEADS)], axis=0)
        cr = jnp.concatenate([cum_t[hd:hd + 1, :] for hd in range(DN_HEADS)], axis=1)
        beta = jnp.concatenate([gb[:, DN_HEADS + hd:DN_HEADS + hd + 1] for hd in range(DN_HEADS)], axis=0)
        c_last = jnp.concatenate(
            [jnp.broadcast_to(cum[chunk - 1:chunk, hd:hd + 1], (chunk, 1)) for hd in range(DN_HEADS)], axis=0)
        qst, kst, vst = stack_heads(q_s, r0), stack_heads(k_s, r0), stack_heads(v_s, r0)
        decay = jnp.where(causal, jnp.exp(jnp.where(causal, cc - cr, 0.0)), 0.0)
        a_mat = jnp.where(strict, decay * _dot_nt(kst, kst), 0.0) * beta
        t_inv = _unit_lower_inverse(a_mat, eye, chunk)
        e_cum = jnp.exp(cc)
        w = _dot(t_inv, jnp.concatenate([beta * vst, (beta * e_cum) * kst], axis=1))
        wv_s[c] = w[:, :DN_HEAD_DIM]
        wk_s[c] = w[:, DN_HEAD_DIM:]
        qk_s[c] = _dot_nt(qst, kst) * decay
        qd_s[c] = e_cum * qst
        kd_s[c] = jnp.exp(c_last - cc) * kst
        cd_s[c] = jnp.broadcast_to(jnp.exp(c_last), (hl, DN_HEAD_DIM))

    for c in range(nb * n_chunk):
        s = c // n_chunk
        r0 = c * chunk
        states, us = [], []
        for hd in range(DN_HEADS):
            hr = slice(hd * chunk, (hd + 1) * chunk)
            state = dns_out[s, hd]
            states.append(state)
            us.append(wv_s[c, hr, :] - _dot(wk_s[c, hr, :], state))
        o_intra = _dot(qk_s[c], jnp.concatenate(us, axis=0))
        for hd in range(DN_HEADS):
            hr = slice(hd * chunk, (hd + 1) * chunk)
            lo = hd * DN_HEAD_DIM
            o_s[r0:r0 + chunk, lo:lo + DN_HEAD_DIM] = _dot(qd_s[c, hr, :], states[hd]) + o_intra[hr, :]
            dns_out[s, hd] = (cd_s[c, hd * chunk:hd * chunk + 1, :] * states[hd]
                              + _dot_tn(kd_s[c, hr, :], us[hd]))

    def attn_seq(s):
        r0 = pl.multiple_of(s * tt, tt)
        mq = mq_s[pl.ds(r0, tt), :]
        for hd in range(MEM_HEADS):
            lo = hd * MEM_HEAD_DIM
            qh = mq[:, lo:lo + MEM_HEAD_DIM].astype(BF16)
            kh = mk_ref[s, :, lo:lo + MEM_HEAD_DIM].astype(BF16)
            vh = mv_ref[s, :, lo:lo + MEM_HEAD_DIM].astype(BF16)
            sc_h = _dot_nt(qh, kh) * (MEM_HEAD_DIM ** -0.5)
            p = jnp.exp(sc_h - jnp.max(sc_h, axis=-1, keepdims=True))
            denom = jnp.sum(p, axis=-1, keepdims=True)
            ymem_s[pl.ds(r0, tt), lo:lo + MEM_HEAD_DIM] = _dot(p.astype(BF16), vh) / denom

    _for_each(nb, attn_seq)

    dn_gate = proj(OFF_DNG, DN_WIDTH)
    o_all = o_s[...]
    y_heads = []
    for hd in range(DN_HEADS):
        lo = hd * DN_HEAD_DIM
        oh = o_all[:, lo:lo + DN_HEAD_DIM]
        oh = oh * lax.rsqrt(jnp.mean(oh * oh, axis=-1, keepdims=True) + EPS) * dnw_ref[...]
        gh = dn_gate[:, lo:lo + DN_HEAD_DIM]
        y_heads.append(oh * (gh * _sigmoid(gh)))
    y_dn = jnp.concatenate(y_heads, axis=-1).astype(BF16)
    y_sc = (sc_b * ysc_s[...]).astype(BF16)
    y_mem = ymem_s[...].astype(BF16)

    merged = _sigmoid(proj(OFF_GATE, D_MODEL)) * _dot(y_dn, w_br[0:DN_WIDTH, :])
    merged = merged + _sigmoid(proj(OFF_GATE + D_MODEL, D_MODEL)) * _dot(y_sc, w_br[DN_WIDTH:DN_WIDTH + SC_WIDTH, :])
    merged = merged + _sigmoid(proj(OFF_GATE + 2 * D_MODEL, D_MODEL)) * _dot(y_mem, w_br[DN_WIDTH + SC_WIDTH:, :])
    h = x + _dot(merged.astype(BF16), w_o[...])
    h_ref[...] = h
    hn = h * lax.rsqrt(jnp.mean(h * h, axis=-1, keepdims=True) + EPS) * n2_ref[...]
    for c in range(ROW_TILES):
        hn_ref[pl.ds(c, rows, stride=ROW_TILES), :] = hn[:, c * LANES:(c + 1) * LANES]
    lg_ref[...] = _dot(hn, wr_ref[...]) + br_ref[...]


def _mixer(x2d, dnc_in, dns_in, scc_in, mk, mv, weights, hn_all, *, n_seq, seq_len, nb, tt, row0):
    chunk = CHUNK if seq_len % CHUNK == 0 else seq_len
    rows = nb * tt
    n_t = seq_len // tt
    total_rows = n_seq * seq_len
    n_chunks = rows // chunk
    hl = DN_HEADS * chunk
    grid = (n_seq // nb, n_t)
    const = lambda b, t: (0, 0)
    seq3 = lambda b, t: (b, 0, 0)
    assert row0 % rows == 0

    def tok(b, t):
        return (b * n_t + t, 0)

    def tok_all(b, t):
        return (row0 // rows + b * n_t + t, 0)

    (w_qkv, w_rest, w_ab, w_br, w_o, n1, n2, dcw, scw, alog, dtb, dnw, wr, br) = weights
    in_specs = [
        pl.BlockSpec((rows, D_MODEL), tok),
        pl.BlockSpec((nb, DN_CONV - 1, QKV_WIDTH), seq3),
        pl.BlockSpec((nb, DN_HEADS, DN_HEAD_DIM, DN_HEAD_DIM), lambda b, t: (b, 0, 0, 0)),
        pl.BlockSpec((nb, SC_CONV - 1, SC_WIDTH), seq3),
        pl.BlockSpec((nb, MEM_LEN, MEM_WIDTH), seq3),
        pl.BlockSpec((nb, MEM_LEN, MEM_WIDTH), seq3),
        pl.BlockSpec(w_qkv.shape, const, pipeline_mode=pl.Buffered(1)),
        pl.BlockSpec(w_rest.shape, const, pipeline_mode=pl.Buffered(1)),
        pl.BlockSpec(w_ab.shape, const, pipeline_mode=pl.Buffered(1)),
        pl.BlockSpec(w_br.shape, const, pipeline_mode=pl.Buffered(1)),
        pl.BlockSpec(w_o.shape, const, pipeline_mode=pl.Buffered(1)),
        pl.BlockSpec(n1.shape, const),
        pl.BlockSpec(n2.shape, const),
        pl.BlockSpec(dcw.shape, const),
        pl.BlockSpec(scw.shape, const),
        pl.BlockSpec(alog.shape, const),
        pl.BlockSpec(dtb.shape, const),
        pl.BlockSpec(dnw.shape, const),
        pl.BlockSpec(wr.shape, const),
        pl.BlockSpec(br.shape, const),
        pl.BlockSpec(memory_space=pl.ANY),
    ]
    out_shape = [
        jax.ShapeDtypeStruct((total_rows, D_MODEL), F32),
        jax.ShapeDtypeStruct(hn_all.shape, F32),
        jax.ShapeDtypeStruct((total_rows, LANES), F32),
        jax.ShapeDtypeStruct((n_seq, DN_CONV - 1, QKV_WIDTH), F32),
        jax.ShapeDtypeStruct((n_seq, DN_HEADS, DN_HEAD_DIM, DN_HEAD_DIM), F32),
        jax.ShapeDtypeStruct((n_seq, SC_CONV - 1, SC_WIDTH), F32),
    ]
    out_specs = [
        pl.BlockSpec((rows, D_MODEL), tok),
        pl.BlockSpec((rows * ROW_TILES, LANES), tok_all),
        pl.BlockSpec((rows, LANES), tok),
        pl.BlockSpec((nb, DN_CONV - 1, QKV_WIDTH), seq3),
        pl.BlockSpec((nb, DN_HEADS, DN_HEAD_DIM, DN_HEAD_DIM), lambda b, t: (b, 0, 0, 0)),
        pl.BlockSpec((nb, SC_CONV - 1, SC_WIDTH), seq3),
    ]
    args = [x2d, dnc_in, dns_in, scc_in, mk, mv, w_qkv, w_rest, w_ab, w_br, w_o, n1, n2, dcw, scw, alog, dtb, dnw, wr,
            br, hn_all]
    scratch = [
        pltpu.VMEM((nb, CONV_PAD + tt, QKV_WIDTH), F32),
        pltpu.VMEM((nb, CONV_PAD + tt, SC_WIDTH), F32),
        pltpu.VMEM((rows, DN_WIDTH), F32),
        pltpu.VMEM((rows, DN_WIDTH), F32),
        pltpu.VMEM((rows, DN_WIDTH), F32),
        pltpu.VMEM((rows, LANES), F32),
        pltpu.VMEM((rows, DN_WIDTH), F32),
        pltpu.VMEM((rows, MEM_WIDTH), F32),
        pltpu.VMEM((rows, SC_WIDTH), F32),
        pltpu.VMEM((rows, MEM_WIDTH), F32),
        pltpu.VMEM((n_chunks, hl, DN_HEAD_DIM), F32),
        pltpu.VMEM((n_chunks, hl, DN_HEAD_DIM), F32),
        pltpu.VMEM((n_chunks, hl, hl), F32),
        pltpu.VMEM((n_chunks, hl, DN_HEAD_DIM), F32),
        pltpu.VMEM((n_chunks, hl, DN_HEAD_DIM), F32),
        pltpu.VMEM((n_chunks, hl, DN_HEAD_DIM), F32),
    ]
    return pl.pallas_call(
        functools.partial(_mixer_kernel, nb=nb, tt=tt, chunk=chunk),
        grid=grid,
        in_specs=in_specs,
        out_specs=out_specs,
        out_shape=out_shape,
        scratch_shapes=scratch,
        input_output_aliases={len(args) - 1: 1},
        compiler_params=pltpu.CompilerParams(
            dimension_semantics=("arbitrary", "arbitrary"), vmem_limit_bytes=VMEM_LIMIT),
        name="mixer",
    )(*args)


def _router_kernel(lg_ref, idx_ref, gate_ref, rank_ref, cnt_ref, carry):
    i = pl.program_id(0)

    @pl.when(i == 0)
    def _():
        carry[...] = jnp.zeros_like(carry)

    work = lg_ref[...]
    tm = work.shape[0]
    lane = lax.broadcasted_iota(jnp.int32, (tm, LANES), 1).astype(F32)
    idxs, vals = [], []
    for _ in range(TOP_K):
        m = jnp.max(work, axis=-1, keepdims=True)
        ik = jnp.min(jnp.where(work == m, lane, float(LANES)), axis=-1, keepdims=True)
        idxs.append(ik)
        vals.append(m)
        work = jnp.where(lane == ik, -jnp.inf, work)
    exps = [jnp.exp(v - vals[0]) for v in vals]
    denom = exps[0] + exps[1] + exps[2] + exps[3]
    hot = jnp.zeros((tm, LANES), F32)
    for ik in idxs:
        hot = hot + (lane == ik).astype(F32)
    ri = lax.broadcasted_iota(jnp.int32, (tm, tm), 0)
    ci = lax.broadcasted_iota(jnp.int32, (tm, tm), 1)
    before = (ri > ci).astype(BF16)
    prefix = _dot(before, hot.astype(BF16)) + carry[...]
    idx_out = jnp.zeros((tm, LANES), F32)
    gate_out = jnp.zeros((tm, LANES), F32)
    rank_out = jnp.zeros((tm, LANES), F32)
    for k in range(TOP_K):
        rk = jnp.sum(jnp.where(lane == idxs[k], prefix, 0.0), axis=-1, keepdims=True)
        idx_out = jnp.where(lane == k, idxs[k], idx_out)
        gate_out = jnp.where(lane == k, exps[k] / denom, gate_out)
        rank_out = jnp.where(lane == k, rk, rank_out)
    idx_ref[...] = idx_out.astype(jnp.int32)
    gate_ref[...] = gate_out
    rank_ref[...] = rank_out.astype(jnp.int32)
    carry[...] = carry[...] + jnp.sum(hot, axis=0, keepdims=True)
    cnt_ref[...] = carry[...].astype(jnp.int32)


def _router(logits):
    n_tok = logits.shape[0]
    tile = lambda i: (i, 0)
    return pl.pallas_call(
        _router_kernel,
        grid=(n_tok // TOK_TILE,),
        in_specs=[pl.BlockSpec((TOK_TILE, LANES), tile)],
        out_specs=[pl.BlockSpec((TOK_TILE, LANES), tile)] * 3 + [pl.BlockSpec((1, LANES), lambda i: (0, 0))],
        out_shape=[
            jax.ShapeDtypeStruct((n_tok, LANES), jnp.int32),
            jax.ShapeDtypeStruct((n_tok, LANES), F32),
            jax.ShapeDtypeStruct((n_tok, LANES), jnp.int32),
            jax.ShapeDtypeStruct((1, LANES), jnp.int32),
        ],
        scratch_shapes=[pltpu.VMEM((1, LANES), F32)],
        compiler_params=pltpu.CompilerParams(dimension_semantics=("arbitrary",)),
        name="router",
    )(logits)


def _invert_kernel(dest_ref, default_ref, inv_ref, sem):
    i = pl.program_id(0)

    @pl.when(i == 0)
    def _():
        cp = pltpu.make_async_copy(default_ref, inv_ref, sem)
        cp.start()
        cp.wait()

    tok_per_trip = INVERT_UNROLL // TOP_K

    def body(j, c):
        tok0 = i * TOK_TILE + j * tok_per_trip
        for d in range(INVERT_UNROLL):
            code = tok0 + (((d % TOP_K) << PAIR_TOK_BITS) + d // TOP_K)
            inv_ref[dest_ref[j * INVERT_UNROLL + d]] = code
        return c

    lax.fori_loop(0, TOK_TILE * TOP_K // INVERT_UNROLL, body, 0)


def _invert(dest_flat, default_inv):
    n_pairs = TOK_TILE * TOP_K
    assert INVERT_UNROLL % TOP_K == 0 and n_pairs % INVERT_UNROLL == 0
    return pl.pallas_call(
        _invert_kernel,
        grid=(dest_flat.shape[0] // n_pairs,),
        in_specs=[
            pl.BlockSpec((n_pairs,), lambda i: (i,), memory_space=pltpu.SMEM),
            pl.BlockSpec(memory_space=pl.ANY),
        ],
        out_specs=pl.BlockSpec(memory_space=pltpu.SMEM),
        out_shape=jax.ShapeDtypeStruct(default_inv.shape, jnp.int32),
        scratch_shapes=[pltpu.SemaphoreType.DMA(())],
        compiler_params=pltpu.CompilerParams(dimension_semantics=("arbitrary",)),
        name="invert",
    )(dest_flat, default_inv)


def _experts_kernel(blk_e, n_valid, inv, hn_ref, wg_ref, bg_ref, wu_ref, bu_ref, wd_ref, bd_ref, yp_ref,
                    wg_bf, wu_bf, wd_bf, xbuf0, xbuf1, xbuf2, ybuf0, ybuf1, ybuf2, zeros, gsem, ssem, zsem,
                    *, n_tok):
    i = pl.program_id(0)
    nv = n_valid[0]
    phase = lax.rem(i, RING)
    xbuf = (xbuf0, xbuf1, xbuf2)
    ybuf = (ybuf0, ybuf1, ybuf2)
    plane_rows = n_tok + TRASH_TOK

    def tile_rows(row, n=1):
        return pl.ds(pl.multiple_of(row * ROW_TILES, ROW_TILES), n * ROW_TILES)

    def gather(block, s):
        base = block * ROW_BLOCK
        for r in range(ROW_BLOCK):
            tok = inv[base + r] & PAIR_TOK_MASK
            pltpu.make_async_copy(
                hn_ref.at[tile_rows(tok)], xbuf[s].at[tile_rows(r)], gsem.at[s]).start(priority=0)

    def scatter(block, s):
        base = block * ROW_BLOCK
        for r in range(ROW_BLOCK):
            pair = inv[base + r]
            row = lax.shift_right_logical(pair, PAIR_TOK_BITS) * plane_rows + (pair & PAIR_TOK_MASK)
            pltpu.make_async_copy(
                ybuf[s].at[tile_rows(r)], yp_ref.at[tile_rows(row)], ssem.at[s]).start(priority=1)

    def wait_gather(s):
        pltpu.make_async_copy(hn_ref.at[tile_rows(0, ROW_BLOCK)], xbuf[s], gsem.at[s]).wait()

    def wait_scatter(s):
        pltpu.make_async_copy(ybuf[s], yp_ref.at[tile_rows(0, ROW_BLOCK)], ssem.at[s]).wait()

    @pl.when(i == 0)
    def _():
        zeros[...] = jnp.zeros_like(zeros)
        for k in range(TOP_K):
            cp = pltpu.make_async_copy(zeros, yp_ref.at[tile_rows(k * plane_rows + n_tok, TRASH_TOK)], zsem)
            cp.start()
            cp.wait()
        gather(0, 0)
        gather(jnp.minimum(1, nv - 1), 1)

    e = blk_e[i]
    prev = blk_e[jnp.maximum(i - 1, 0)]

    @pl.when(((i == 0) | (e != prev)) & (i < nv))
    def _():
        wg_bf[...] = wg_ref[0].astype(BF16)
        wu_bf[...] = wu_ref[0].astype(BF16)
        wd_bf[...] = wd_ref[0].astype(BF16)

    def step(slot, scatter_previous):
        ahead, behind = (slot + 2) % RING, (slot - 1) % RING
        wait_gather(slot)
        gather(jnp.minimum(i + 2, nv - 1), ahead)
        if scatter_previous:
            scatter(i - 1, behind)
        x = jnp.concatenate(
            [xbuf[slot][pl.ds(c, ROW_BLOCK, stride=ROW_TILES), :] for c in range(ROW_TILES)], axis=1).astype(BF16)
        gl = jnp.minimum(_dot(x, wg_bf[...]) + bg_ref[0], SWIGLU_LIMIT)
        ul = jnp.clip(_dot(x, wu_bf[...]) + bu_ref[0], -SWIGLU_LIMIT, SWIGLU_LIMIT)
        act = (ul + 1.0) * (gl * _sigmoid(SWIGLU_ALPHA * gl))
        y = _dot(act.astype(BF16), wd_bf[...]) + bd_ref[0]
        for c in range(ROW_TILES):
            ybuf[slot][pl.ds(c, ROW_BLOCK, stride=ROW_TILES), :] = y[:, c * LANES:(c + 1) * LANES]

    def finish(slot):
        ahead, behind = (slot + 2) % RING, (slot - 1) % RING
        scatter(i, slot)
        wait_gather((slot + 1) % RING)
        wait_gather(ahead)

        @pl.when(i >= 2)
        def _():
            wait_scatter((slot - 2) % RING)

        @pl.when(i >= 1)
        def _():
            wait_scatter(behind)

        wait_scatter(slot)

    @pl.when((i == 0) & (i < nv))
    def _():
        step(0, False)

    for slot in range(RING):
        mine = (phase == slot) & (i < nv)

        @pl.when(mine & (i >= RING))
        def _():
            wait_scatter(slot)

        @pl.when(mine & (i > 0))
        def _():
            step(slot, True)

        @pl.when(mine & (i == nv - 1))
        def _():
            finish(slot)


def _experts(blk_e, n_valid, inv, hn_all, w_gate, b_gate, w_up, b_up, w_down, b_down, *, n_tok):
    n_blocks = blk_e.shape[0]
    wsel = lambda i, be, nv, iv: (be[i], 0, 0)
    d_ff = w_gate.shape[-1]
    grid_spec = pltpu.PrefetchScalarGridSpec(
        num_scalar_prefetch=3,
        grid=(n_blocks,),
        in_specs=[
            pl.BlockSpec(memory_space=pl.ANY),
            pl.BlockSpec((1, D_MODEL, d_ff), wsel),
            pl.BlockSpec((1, 1, d_ff), wsel),
            pl.BlockSpec((1, D_MODEL, d_ff), wsel),
            pl.BlockSpec((1, 1, d_ff), wsel),
            pl.BlockSpec((1, d_ff, D_MODEL), wsel),
            pl.BlockSpec((1, 1, D_MODEL), wsel),
        ],
        out_specs=pl.BlockSpec(memory_space=pl.ANY),
        scratch_shapes=[
            pltpu.VMEM((D_MODEL, d_ff), BF16),
            pltpu.VMEM((D_MODEL, d_ff), BF16),
            pltpu.VMEM((d_ff, D_MODEL), BF16),
        ] + [pltpu.VMEM((ROW_BLOCK * ROW_TILES, LANES), F32)] * (2 * RING) + [
            pltpu.VMEM((TRASH_TOK * ROW_TILES, LANES), F32),
            pltpu.SemaphoreType.DMA((RING,)),
            pltpu.SemaphoreType.DMA((RING,)),
            pltpu.SemaphoreType.DMA(()),
        ],
    )
    return pl.pallas_call(
        functools.partial(_experts_kernel, n_tok=n_tok),
        grid_spec=grid_spec,
        out_shape=jax.ShapeDtypeStruct((TOP_K * (n_tok + TRASH_TOK) * ROW_TILES, LANES), F32),
        compiler_params=pltpu.CompilerParams(
            dimension_semantics=("arbitrary",), vmem_limit_bytes=VMEM_LIMIT),
        name="experts",
    )(blk_e, n_valid, inv, hn_all, w_gate, b_gate, w_up, b_up, w_down, b_down)


def _combine_kernel(gate_ref, h_ref, fw_ref, *refs):
    y_ref = refs[TOP_K]
    gate = gate_ref[...]
    acc = h_ref[...]
    for k in range(TOP_K):
        y_k = jnp.concatenate(
            [refs[k][pl.ds(c, TOK_TILE, stride=ROW_TILES), :] for c in range(ROW_TILES)], axis=1)
        acc = acc + gate[:, k:k + 1] * y_k
    y_ref[...] = acc * lax.rsqrt(jnp.mean(acc * acc, axis=-1, keepdims=True) + EPS) * fw_ref[...]


def _combine(gates, h_group, final_w, y_pairs, *, tile0, plane_rows):
    n_tiles = h_group.shape[0] // TOK_TILE
    plane_tiles = plane_rows // TOK_TILE

    def plane(k):
        return pl.BlockSpec((TOK_TILE * ROW_TILES, LANES), lambda i: (k * plane_tiles + tile0 + i, 0))

    return pl.pallas_call(
        _combine_kernel,
        grid=(n_tiles,),
        in_specs=[
            pl.BlockSpec((TOK_TILE, LANES), lambda i: (tile0 + i, 0)),
            pl.BlockSpec((TOK_TILE, D_MODEL), lambda i: (i, 0)),
            pl.BlockSpec((1, D_MODEL), lambda i: (0, 0)),
        ] + [plane(k) for k in range(TOP_K)],
        out_specs=pl.BlockSpec((TOK_TILE, D_MODEL), lambda i: (i, 0)),
        out_shape=jax.ShapeDtypeStruct((n_tiles * TOK_TILE, D_MODEL), F32),
        compiler_params=pltpu.CompilerParams(dimension_semantics=("arbitrary",)),
        name="combine",
    )(gates, h_group, final_w, *([y_pairs] * TOP_K))


def _split_w_in(w):
    o_a = QKV_WIDTH
    o_g = o_a + 2 * DN_HEADS
    pad = jnp.zeros((w.shape[0], LANES - 2 * DN_HEADS), BF16)
    w_ab = jnp.concatenate([w[:, o_a:o_g].astype(BF16), pad], axis=1)
    return w[:, :o_a].astype(BF16), w[:, o_g:].astype(BF16), w_ab


def _lane_row(v, fill=0.0):
    return jnp.concatenate([v.astype(F32), jnp.full((LANES - v.shape[0],), fill, F32)]).reshape(1, LANES)


def kernel(x_prompt, x_sample, mem_prompt, state_dn, state_dn_conv, state_sc_conv, cache_mem_k, cache_mem_v, w_in, dn_conv_w, dn_A_log, dn_dt_bias, dn_norm_w, sc_conv_w, mem_norm_w, w_mem_kv, w_br, w_o, norm1_w, norm2_w, w_router, b_router, w_gate, b_gate, w_up, b_up, w_down, b_down, final_norm_w):
    assert w_in.shape[0] == 1, "one layer"
    bp, tp, _ = x_prompt.shape
    bs, ts, _ = x_sample.shape
    n_p, n_s = bp * tp, bs * ts
    n_tok = n_p + n_s
    assert n_p % TOK_TILE == 0 and n_s % TOK_TILE == 0 and tp % MIX_TILE == 0

    weights = (
        *_split_w_in(w_in[0]),
        w_br[0].astype(BF16),
        w_o[0].astype(BF16),
        norm1_w[0].reshape(1, D_MODEL),
        norm2_w[0].reshape(1, D_MODEL),
        dn_conv_w[0],
        sc_conv_w[0],
        _lane_row(dn_A_log[0]),
        _lane_row(dn_dt_bias[0]),
        dn_norm_w[0].reshape(1, DN_HEAD_DIM),
        jnp.concatenate([w_router[0], jnp.zeros((D_MODEL, LANES - N_EXPERTS), F32)], axis=1),
        _lane_row(b_router[0], NEG_BIG),
    )

    mk2d, mv2d = _memkv(mem_prompt.reshape(bp * MEM_LEN, D_MODEL), mem_norm_w[0].reshape(1, D_MODEL),
                        w_mem_kv[0].astype(BF16))

    assert n_tok + TRASH_TOK <= PAIR_TOK_MASK + 1 and (n_tok + TRASH_TOK) % MIX_TILE == 0
    hn_all = jnp.zeros(((n_tok + TRASH_TOK) * ROW_TILES, LANES), F32)
    h_s, hn_all, lg_s, s_dnc, s_dns, s_scc = _mixer(
        x_sample.reshape(n_s, D_MODEL), state_dn_conv[0], state_dn[0], state_sc_conv[0],
        cache_mem_k[0].reshape(bs, MEM_LEN, MEM_WIDTH), cache_mem_v[0].reshape(bs, MEM_LEN, MEM_WIDTH),
        weights, hn_all, n_seq=bs, seq_len=ts, nb=bs, tt=ts, row0=n_p)
    h_p, hn_all, lg_p, p_dnc, p_dns, p_scc = _mixer(
        x_prompt.reshape(n_p, D_MODEL),
        jnp.zeros((bp, DN_CONV - 1, QKV_WIDTH), F32),
        jnp.zeros((bp, DN_HEADS, DN_HEAD_DIM, DN_HEAD_DIM), F32),
        jnp.zeros((bp, SC_CONV - 1, SC_WIDTH), F32),
        mk2d.reshape(bp, MEM_LEN, MEM_WIDTH), mv2d.reshape(bp, MEM_LEN, MEM_WIDTH),
        weights, hn_all, n_seq=bp, seq_len=tp, nb=1, tt=MIX_TILE, row0=0)

    idx, gates, rank, counts = _router(jnp.concatenate([lg_p, lg_s], axis=0))
    counts = counts[0, :N_EXPERTS]
    n_blk_e = (counts + ROW_BLOCK - 1) // ROW_BLOCK
    blk_end = jnp.cumsum(n_blk_e)
    blk_start = blk_end - n_blk_e
    row_start = blk_start * ROW_BLOCK
    experts = jnp.arange(N_EXPERTS, dtype=jnp.int32)
    start_of = jnp.sum(jnp.where(idx[:, :TOP_K, None] == experts, row_start, 0), axis=-1)
    dest = (start_of + rank[:, :TOP_K]).reshape(n_tok * TOP_K)

    n_blocks = (n_tok * TOP_K) // ROW_BLOCK + N_EXPERTS
    bi = jnp.arange(n_blocks, dtype=jnp.int32)
    n_valid = blk_end[-1].astype(jnp.int32)
    bclip = jnp.minimum(bi, n_valid - 1)
    blk_e = jnp.sum((bclip[:, None] >= blk_end[None, :]).astype(jnp.int32), axis=1)
    n_valid = n_valid.reshape(1)

    slots = jnp.arange(n_blocks * ROW_BLOCK, dtype=jnp.int32)
    default_inv = n_tok + (slots & (TRASH_TOK - 1))
    inv = _invert(dest.astype(jnp.int32), default_inv)

    y_pairs = _experts(blk_e, n_valid, inv, hn_all,
                       w_gate[0], b_gate[0].reshape(N_EXPERTS, 1, -1), w_up[0], b_up[0].reshape(N_EXPERTS, 1, -1),
                       w_down[0], b_down[0].reshape(N_EXPERTS, 1, -1), n_tok=n_tok)
    fw = final_norm_w.reshape(1, D_MODEL)
    y_p = _combine(gates, h_p, fw, y_pairs, tile0=0, plane_rows=n_tok + TRASH_TOK)
    y_s = _combine(gates, h_s, fw, y_pairs, tile0=n_p // TOK_TILE, plane_rows=n_tok + TRASH_TOK)

    return (y_p.reshape(bp, tp, D_MODEL), y_s.reshape(bs, ts, D_MODEL),
            p_dns[None], p_dnc[None], p_scc[None],
            mk2d.reshape(1, bp, MEM_LEN, MEM_HEADS, MEM_HEAD_DIM), mv2d.reshape(1, bp, MEM_LEN, MEM_HEADS, MEM_HEAD_DIM),
            s_dns[None], s_dnc[None], s_scc[None])
```

```python
import functools

import jax
import jax.numpy as jnp
from jax import lax
from jax.experimental import pallas as pl
from jax.experimental.pallas import tpu as pltpu

F32 = jnp.float32
BF16 = jnp.bfloat16

D_MODEL = 1024
CHUNK = 64
EPS = 1e-6
DN_HEADS = 4
DN_HEAD_DIM = 128
DN_WIDTH = DN_HEADS * DN_HEAD_DIM
QKV_WIDTH = 3 * DN_WIDTH
DN_CONV = 4
SC_WIDTH = 256
SC_CONV = 3
MEM_LEN = 256
MEM_HEADS = 4
MEM_HEAD_DIM = 64
MEM_WIDTH = MEM_HEADS * MEM_HEAD_DIM
N_EXPERTS = 32
TOP_K = 4
SWIGLU_ALPHA = 1.702
SWIGLU_LIMIT = 7.0

LANES = 128
SUBLANES = 8
ROW_TILES = D_MODEL // LANES
assert ROW_TILES == SUBLANES
CONV_PAD = 8

OFF_QKV = 0
OFF_DNG = OFF_QKV + QKV_WIDTH
OFF_SC = OFF_DNG + DN_WIDTH
OFF_MQ = OFF_SC + 3 * SC_WIDTH
OFF_GATE = OFF_MQ + MEM_WIDTH
OFF_AB = OFF_GATE + 3 * D_MODEL
IN_PERM_WIDTH = OFF_AB + LANES

MIX_TILE = 512
TOK_TILE = 256
ROW_BLOCK = 256
INVERT_UNROLL = 32
TRASH_TOK = 2 * ROW_BLOCK
RING = 3
PAIR_TOK_BITS = 16
PAIR_TOK_MASK = (1 << PAIR_TOK_BITS) - 1
VMEM_LIMIT = 56 * 1024 * 1024
NEG_BIG = -1e30


def _dot(a, b):
    return jnp.dot(a, b, preferred_element_type=F32)


def _dot_nt(a, b):
    return lax.dot_general(a, b, (((1,), (1,)), ((), ())), preferred_element_type=F32)


def _dot_tn(a, b):
    return lax.dot_general(a, b, (((0,), (0,)), ((), ())), preferred_element_type=F32)


def _sigmoid(x):
    return 1.0 / (1.0 + jnp.exp(-x))


def _softplus(x):
    return jnp.maximum(x, 0.0) + jnp.log1p(jnp.exp(-jnp.abs(x)))


def _for_each(n, body):
    if n == 1:
        body(0)
    else:
        def step(i, carry):
            body(i)
            return carry
        lax.fori_loop(0, n, step, 0)


def _memkv_kernel(mem_ref, nw_ref, w_ref, k_ref, v_ref):
    x = mem_ref[...]
    xn = x * lax.rsqrt(jnp.mean(x * x, axis=-1, keepdims=True) + EPS) * nw_ref[...]
    kv = _dot(xn.astype(BF16), w_ref[...])
    k_ref[...] = kv[:, :MEM_WIDTH]
    v_ref[...] = kv[:, MEM_WIDTH:]


def _memkv(mem2d, norm_w, w_kv_bf16):
    rows = mem2d.shape[0]
    grid = rows // MEM_LEN
    return pl.pallas_call(
        _memkv_kernel,
        grid=(grid,),
        in_specs=[
            pl.BlockSpec((MEM_LEN, D_MODEL), lambda i: (i, 0)),
            pl.BlockSpec((1, D_MODEL), lambda i: (0, 0)),
            pl.BlockSpec((D_MODEL, 2 * MEM_WIDTH), lambda i: (0, 0)),
        ],
        out_specs=[
            pl.BlockSpec((MEM_LEN, MEM_WIDTH), lambda i: (i, 0)),
            pl.BlockSpec((MEM_LEN, MEM_WIDTH), lambda i: (i, 0)),
        ],
        out_shape=[jax.ShapeDtypeStruct((rows, MEM_WIDTH), F32)] * 2,
        name="memkv",
    )(mem2d, norm_w, w_kv_bf16)


def _unit_lower_inverse(a, eye, size):
    inv = eye - a
    power = a
    span = 2
    while span < size:
        power = _dot(power, power)
        inv = _dot(inv, eye + power)
        span *= 2
    return inv


def _mixer_kernel(x_ref, dnc_in, dns_in, scc_in, mk_ref, mv_ref, w_qkv, w_rest, w_ab, w_br, w_o, n1_ref, n2_ref,
                  dcw_ref, scw_ref,
                  alog_ref, dtb_ref, dnw_ref, wr_ref, br_ref, hn_all_ref,
                  h_ref, hn_ref, lg_ref, dnc_out, dns_out, scc_out,
                  xp, scp, q_s, k_s, v_s, gb_s, o_s, mq_s, ysc_s, ymem_s, wv_s, wk_s, qk_s, qd_s, kd_s, cd_s,
                  *, nb, tt, chunk):
    t_idx = pl.program_id(1)
    rows = nb * tt
    n_chunk = tt // chunk

    @pl.when(t_idx == 0)
    def _():
        xp[:, CONV_PAD - (DN_CONV - 1):CONV_PAD, :] = dnc_in[...]
        scp[:, CONV_PAD - (SC_CONV - 1):CONV_PAD, :] = scc_in[...]
        dns_out[...] = dns_in[...]

    x = x_ref[...]
    xn = (x * lax.rsqrt(jnp.mean(x * x, axis=-1, keepdims=True) + EPS) * n1_ref[...]).astype(BF16)

    def proj(off, width):
        if off >= OFF_AB:
            w = w_ab[:, off - OFF_AB:off - OFF_AB + width]
        elif off >= OFF_DNG:
            w = w_rest[:, off - OFF_DNG:off - OFF_DNG + width]
        else:
            w = w_qkv[:, off:off + width]
        return _dot(xn, w)

    qkv_pre = proj(OFF_QKV, QKV_WIDTH)
    for s in range(nb):
        xp[s, CONV_PAD:CONV_PAD + tt, :] = qkv_pre[s * tt:(s + 1) * tt, :]
    sc = proj(OFF_SC, 3 * SC_WIDTH)
    sc_b = sc[:, :SC_WIDTH]
    sc_ch = sc[:, SC_WIDTH:2 * SC_WIDTH] * sc[:, 2 * SC_WIDTH:]
    for s in range(nb):
        scp[s, CONV_PAD:CONV_PAD + tt, :] = sc_ch[s * tt:(s + 1) * tt, :]
    mq_s[...] = proj(OFF_MQ, MEM_WIDTH)

    ab = proj(OFF_AB, LANES)
    lane = lax.broadcasted_iota(jnp.int32, (rows, LANES), 1)
    g_log = -jnp.exp(alog_ref[...]) * _softplus(ab + dtb_ref[...])
    gb_s[...] = jnp.where(lane < DN_HEADS, g_log, _sigmoid(ab))

    def conv_seq(s):
        base = CONV_PAD - (DN_CONV - 1)
        acc = dcw_ref[0:1, :] * xp[s, pl.ds(base, tt), :]
        for j in range(1, DN_CONV):
            acc = acc + dcw_ref[j:j + 1, :] * xp[s, pl.ds(base + j, tt), :]
        act = acc * _sigmoid(acc)
        r0 = pl.multiple_of(s * tt, tt)
        for hd in range(DN_HEADS):
            lo = hd * DN_HEAD_DIM
            qh = act[:, lo:lo + DN_HEAD_DIM]
            kh = act[:, DN_WIDTH + lo:DN_WIDTH + lo + DN_HEAD_DIM]
            q_s[pl.ds(r0, tt), lo:lo + DN_HEAD_DIM] = (
                qh * lax.rsqrt(jnp.sum(qh * qh, axis=-1, keepdims=True) + EPS) * (DN_HEAD_DIM ** -0.5))
            k_s[pl.ds(r0, tt), lo:lo + DN_HEAD_DIM] = (
                kh * lax.rsqrt(jnp.sum(kh * kh, axis=-1, keepdims=True) + EPS))
        v_s[pl.ds(r0, tt), :] = act[:, 2 * DN_WIDTH:]
        tail = xp[s, pl.ds(tt + base, DN_CONV - 1), :]
        dnc_out[s] = tail
        xp[s, pl.ds(base, DN_CONV - 1), :] = tail

        base2 = CONV_PAD - (SC_CONV - 1)
        acc2 = scw_ref[0:1, :] * scp[s, pl.ds(base2, tt), :]
        for j in range(1, SC_CONV):
            acc2 = acc2 + scw_ref[j:j + 1, :] * scp[s, pl.ds(base2 + j, tt), :]
        ysc_s[pl.ds(r0, tt), :] = acc2
        tail2 = scp[s, pl.ds(tt + base2, SC_CONV - 1), :]
        scc_out[s] = tail2
        scp[s, pl.ds(base2, SC_CONV - 1), :] = tail2

    _for_each(nb, conv_seq)

    hl = DN_HEADS * chunk
    shift = chunk.bit_length() - 1
    ri = lax.broadcasted_iota(jnp.int32, (hl, hl), 0)
    ci = lax.broadcasted_iota(jnp.int32, (hl, hl), 1)
    same_head = lax.shift_right_logical(ri, shift) == lax.shift_right_logical(ci, shift)
    causal = same_head & (ri >= ci)
    strict = same_head & (ri > ci)
    eye = (ri == ci).astype(F32)
    ri1 = lax.broadcasted_iota(jnp.int32, (chunk, chunk), 0)
    ci1 = lax.broadcasted_iota(jnp.int32, (chunk, chunk), 1)
    tril = (ri1 >= ci1).astype(F32)
    triu = (ri1 <= ci1).astype(F32)

    def stack_heads(ref, r0):
        return jnp.concatenate(
            [ref[r0:r0 + chunk, hd * DN_HEAD_DIM:(hd + 1) * DN_HEAD_DIM] for hd in range(DN_HEADS)], axis=0)

    for c in range(nb * n_chunk):
        r0 = c * chunk
        gb = gb_s[r0:r0 + chunk, :]
        cum = _dot(tril, gb)
        cum_t = _dot_tn(gb, triu)
        cc = jnp.concatenate([cum[:, hd:hd + 1] for hd in range(DN_HEADS)], axis=0)
        cr = jnp.concatenate([cum_t[hd:hd + 1, :] for hd in range(DN_HEADS)], axis=1)
        beta = jnp.concatenate([gb[:, DN_HEADS + hd:DN_HEADS + hd + 1] for hd in range(DN_HEADS)], axis=0)
        c_last = jnp.concatenate(
            [jnp.broadcast_to(cum[chunk - 1:chunk, hd:hd + 1], (chunk, 1)) for hd in range(DN_HEADS)], axis=0)
        qst, kst, vst = stack_heads(q_s, r0), stack_heads(k_s, r0), stack_heads(v_s, r0)
        decay = jnp.where(causal, jnp.exp(jnp.where(causal, cc - cr, 0.0)), 0.0)
        a_mat = jnp.where(strict, decay * _dot_nt(kst, kst), 0.0) * beta
        t_inv = _unit_lower_inverse(a_mat, eye, chunk)
        e_cum = jnp.exp(cc)
        w = _dot(t_inv, jnp.concatenate([beta * vst, (beta * e_cum) * kst], axis=1))
        wv_s[c] = w[:, :DN_HEAD_DIM]
        wk_s[c] = w[:, DN_HEAD_DIM:]
        qk_s[c] = _dot_nt(qst, kst) * decay
        qd_s[c] = e_cum * qst
        kd_s[c] = jnp.exp(c_last - cc) * kst
        cd_s[c] = jnp.broadcast_to(jnp.exp(c_last), (hl, DN_HEAD_DIM))

    for c in range(nb * n_chunk):
        s = c // n_chunk
        r0 = c * chunk
        states, us = [], []
        for hd in range(DN_HEADS):
            hr = slice(hd * chunk, (hd + 1) * chunk)
            state = dns_out[s, hd]
            states.append(state)
            us.append(wv_s[c, hr, :] - _dot(wk_s[c, hr, :], state))
        o_intra = _dot(qk_s[c], jnp.concatenate(us, axis=0))
        for hd in range(DN_HEADS):
            hr = slice(hd * chunk, (hd + 1) * chunk)
            lo = hd * DN_HEAD_DIM
            o_s[r0:r0 + chunk, lo:lo + DN_HEAD_DIM] = _dot(qd_s[c, hr, :], states[hd]) + o_intra[hr, :]
            dns_out[s, hd] = (cd_s[c, hd * chunk:hd * chunk + 1, :] * states[hd]
                              + _dot_tn(kd_s[c, hr, :], us[hd]))

    def attn_seq(s):
        r0 = pl.multiple_of(s * tt, tt)
        mq = mq_s[pl.ds(r0, tt), :]
        for hd in range(MEM_HEADS):
            lo = hd * MEM_HEAD_DIM
            qh = mq[:, lo:lo + MEM_HEAD_DIM].astype(BF16)
            kh = mk_ref[s, :, lo:lo + MEM_HEAD_DIM].astype(BF16)
            vh = mv_ref[s, :, lo:lo + MEM_HEAD_DIM].astype(BF16)
            sc_h = _dot_nt(qh, kh) * (MEM_HEAD_DIM ** -0.5)
            p = jnp.exp(sc_h - jnp.max(sc_h, axis=-1, keepdims=True))
            denom = jnp.sum(p, axis=-1, keepdims=True)
            ymem_s[pl.ds(r0, tt), lo:lo + MEM_HEAD_DIM] = _dot(p.astype(BF16), vh) / denom

    _for_each(nb, attn_seq)

    dn_gate = proj(OFF_DNG, DN_WIDTH)
    o_all = o_s[...]
    y_heads = []
    for hd in range(DN_HEADS):
        lo = hd * DN_HEAD_DIM
        oh = o_all[:, lo:lo + DN_HEAD_DIM]
        oh = oh * lax.rsqrt(jnp.mean(oh * oh, axis=-1, keepdims=True) + EPS) * dnw_ref[...]
        gh = dn_gate[:, lo:lo + DN_HEAD_DIM]
        y_heads.append(oh * (gh * _sigmoid(gh)))
    y_dn = jnp.concatenate(y_heads, axis=-1).astype(BF16)
    y_sc = (sc_b * ysc_s[...]).astype(BF16)
    y_mem = ymem_s[...].astype(BF16)

    merged = _sigmoid(proj(OFF_GATE, D_MODEL)) * _dot(y_dn, w_br[0:DN_WIDTH, :])
    merged = merged + _sigmoid(proj(OFF_GATE + D_MODEL, D_MODEL)) * _dot(y_sc, w_br[DN_WIDTH:DN_WIDTH + SC_WIDTH, :])
    merged = merged + _sigmoid(proj(OFF_GATE + 2 * D_MODEL, D_MODEL)) * _dot(y_mem, w_br[DN_WIDTH + SC_WIDTH:, :])
    h = x + _dot(merged.astype(BF16), w_o[...])
    h_ref[...] = h
    hn = h * lax.rsqrt(jnp.mean(h * h, axis=-1, keepdims=True) + EPS) * n2_ref[...]
    for c in range(ROW_TILES):
        hn_ref[pl.ds(c, rows, stride=ROW_TILES), :] = hn[:, c * LANES:(c + 1) * LANES]
    lg_ref[...] = _dot(hn, wr_ref[...]) + br_ref[...]


def _mixer(x2d, dnc_in, dns_in, scc_in, mk, mv, weights, hn_all, *, n_seq, seq_len, nb, tt, row0):
    chunk = CHUNK if seq_len % CHUNK == 0 else seq_len
    rows = nb * tt
    n_t = seq_len // tt
    total_rows = n_seq * seq_len
    n_chunks = rows // chunk
    hl = DN_HEADS * chunk
    grid = (n_seq // nb, n_t)
    const = lambda b, t: (0, 0)
    seq3 = lambda b, t: (b, 0, 0)
    assert row0 % rows == 0

    def tok(b, t):
        return (b * n_t + t, 0)

    def tok_all(b, t):
        return (row0 // rows + b * n_t + t, 0)

    (w_qkv, w_rest, w_ab, w_br, w_o, n1, n2, dcw, scw, alog, dtb, dnw, wr, br) = weights
    in_specs = [
        pl.BlockSpec((rows, D_MODEL), tok),
        pl.BlockSpec((nb, DN_CONV - 1, QKV_WIDTH), seq3),
        pl.BlockSpec((nb, DN_HEADS, DN_HEAD_DIM, DN_HEAD_DIM), lambda b, t: (b, 0, 0, 0)),
        pl.BlockSpec((nb, SC_CONV - 1, SC_WIDTH), seq3),
        pl.BlockSpec((nb, MEM_LEN, MEM_WIDTH), seq3),
        pl.BlockSpec((nb, MEM_LEN, MEM_WIDTH), seq3),
        pl.BlockSpec(w_qkv.shape, const, pipeline_mode=pl.Buffered(1)),
        pl.BlockSpec(w_rest.shape, const, pipeline_mode=pl.Buffered(1)),
        pl.BlockSpec(w_ab.shape, const, pipeline_mode=pl.Buffered(1)),
        pl.BlockSpec(w_br.shape, const, pipeline_mode=pl.Buffered(1)),
        pl.BlockSpec(w_o.shape, const, pipeline_mode=pl.Buffered(1)),
        pl.BlockSpec(n1.shape, const),
        pl.BlockSpec(n2.shape, const),
        pl.BlockSpec(dcw.shape, const),
        pl.BlockSpec(scw.shape, const),
        pl.BlockSpec(alog.shape, const),
        pl.BlockSpec(dtb.shape, const),
        pl.BlockSpec(dnw.shape, const),
        pl.BlockSpec(wr.shape, const),
        pl.BlockSpec(br.shape, const),
        pl.BlockSpec(memory_space=pl.ANY),
    ]
    out_shape = [
        jax.ShapeDtypeStruct((total_rows, D_MODEL), F32),
        jax.ShapeDtypeStruct(hn_all.shape, F32),
        jax.ShapeDtypeStruct((total_rows, LANES), F32),
        jax.ShapeDtypeStruct((n_seq, DN_CONV - 1, QKV_WIDTH), F32),
        jax.ShapeDtypeStruct((n_seq, DN_HEADS, DN_HEAD_DIM, DN_HEAD_DIM), F32),
        jax.ShapeDtypeStruct((n_seq, SC_CONV - 1, SC_WIDTH), F32),
    ]
    out_specs = [
        pl.BlockSpec((rows, D_MODEL), tok),
        pl.BlockSpec((rows * ROW_TILES, LANES), tok_all),
        pl.BlockSpec((rows, LANES), tok),
        pl.BlockSpec((nb, DN_CONV - 1, QKV_WIDTH), seq3),
        pl.BlockSpec((nb, DN_HEADS, DN_HEAD_DIM, DN_HEAD_DIM), lambda b, t: (b, 0, 0, 0)),
        pl.BlockSpec((nb, SC_CONV - 1, SC_WIDTH), seq3),
    ]
    args = [x2d, dnc_in, dns_in, scc_in, mk, mv, w_qkv, w_rest, w_ab, w_br, w_o, n1, n2, dcw, scw, alog, dtb, dnw, wr,
            br, hn_all]
    scratch = [
        pltpu.VMEM((nb, CONV_PAD + tt, QKV_WIDTH), F32),
        pltpu.VMEM((nb, CONV_PAD + tt, SC_WIDTH), F32),
        pltpu.VMEM((rows, DN_WIDTH), F32),
        pltpu.VMEM((rows, DN_WIDTH), F32),
        pltpu.VMEM((rows, DN_WIDTH), F32),
        pltpu.VMEM((rows, LANES), F32),
        pltpu.VMEM((rows, DN_WIDTH), F32),
        pltpu.VMEM((rows, MEM_WIDTH), F32),
        pltpu.VMEM((rows, SC_WIDTH), F32),
        pltpu.VMEM((rows, MEM_WIDTH), F32),
        pltpu.VMEM((n_chunks, hl, DN_HEAD_DIM), F32),
        pltpu.VMEM((n_chunks, hl, DN_HEAD_DIM), F32),
        pltpu.VMEM((n_chunks, hl, hl), F32),
        pltpu.VMEM((n_chunks, hl, DN_HEAD_DIM), F32),
        pltpu.VMEM((n_chunks, hl, DN_HEAD_DIM), F32),
        pltpu.VMEM((n_chunks, hl, DN_HEAD_DIM), F32),
    ]
    return pl.pallas_call(
        functools.partial(_mixer_kernel, nb=nb, tt=tt, chunk=chunk),
        grid=grid,
        in_specs=in_specs,
        out_specs=out_specs,
        out_shape=out_shape,
        scratch_shapes=scratch,
        input_output_aliases={len(args) - 1: 1},
        compiler_params=pltpu.CompilerParams(
            dimension_semantics=("arbitrary", "arbitrary"), vmem_limit_bytes=VMEM_LIMIT),
        name="mixer",
    )(*args)


def _router_kernel(lg_ref, idx_ref, gate_ref, rank_ref, cnt_ref, carry):
    i = pl.program_id(0)

    @pl.when(i == 0)
    def _():
        carry[...] = jnp.zeros_like(carry)

    work = lg_ref[...]
    tm = work.shape[0]
    lane = lax.broadcasted_iota(jnp.int32, (tm, LANES), 1).astype(F32)
    idxs, vals = [], []
    for _ in range(TOP_K):
        m = jnp.max(work, axis=-1, keepdims=True)
        ik = jnp.min(jnp.where(work == m, lane, float(LANES)), axis=-1, keepdims=True)
        idxs.append(ik)
        vals.append(m)
        work = jnp.where(lane == ik, -jnp.inf, work)
    exps = [jnp.exp(v - vals[0]) for v in vals]
    denom = exps[0] + exps[1] + exps[2] + exps[3]
    hot = jnp.zeros((tm, LANES), F32)
    for ik in idxs:
        hot = hot + (lane == ik).astype(F32)
    ri = lax.broadcasted_iota(jnp.int32, (tm, tm), 0)
    ci = lax.broadcasted_iota(jnp.int32, (tm, tm), 1)
    before = (ri > ci).astype(BF16)
    prefix = _dot(before, hot.astype(BF16)) + carry[...]
    idx_out = jnp.zeros((tm, LANES), F32)
    gate_out = jnp.zeros((tm, LANES), F32)
    rank_out = jnp.zeros((tm, LANES), F32)
    for k in range(TOP_K):
        rk = jnp.sum(jnp.where(lane == idxs[k], prefix, 0.0), axis=-1, keepdims=True)
        idx_out = jnp.where(lane == k, idxs[k], idx_out)
        gate_out = jnp.where(lane == k, exps[k] / denom, gate_out)
        rank_out = jnp.where(lane == k, rk, rank_out)
    idx_ref[...] = idx_out.astype(jnp.int32)
    gate_ref[...] = gate_out
    rank_ref[...] = rank_out.astype(jnp.int32)
    carry[...] = carry[...] + jnp.sum(hot, axis=0, keepdims=True)
    cnt_ref[...] = carry[...].astype(jnp.int32)


def _router(logits):
    n_tok = logits.shape[0]
    tile = lambda i: (i, 0)
    return pl.pallas_call(
        _router_kernel,
        grid=(n_tok // TOK_TILE,),
        in_specs=[pl.BlockSpec((TOK_TILE, LANES), tile)],
        out_specs=[pl.BlockSpec((TOK_TILE, LANES), tile)] * 3 + [pl.BlockSpec((1, LANES), lambda i: (0, 0))],
        out_shape=[
            jax.ShapeDtypeStruct((n_tok, LANES), jnp.int32),
            jax.ShapeDtypeStruct((n_tok, LANES), F32),
            jax.ShapeDtypeStruct((n_tok, LANES), jnp.int32),
            jax.ShapeDtypeStruct((1, LANES), jnp.int32),
        ],
        scratch_shapes=[pltpu.VMEM((1, LANES), F32)],
        compiler_params=pltpu.CompilerParams(dimension_semantics=("arbitrary",)),
        name="router",
    )(logits)


def _invert_kernel(dest_ref, default_ref, inv_ref, sem):
    i = pl.program_id(0)

    @pl.when(i == 0)
    def _():
        cp = pltpu.make_async_copy(default_ref, inv_ref, sem)
        cp.start()
        cp.wait()

    tok_per_trip = INVERT_UNROLL // TOP_K

    def body(j, c):
        tok0 = i * TOK_TILE + j * tok_per_trip
        for d in range(INVERT_UNROLL):
            code = tok0 + (((d % TOP_K) << PAIR_TOK_BITS) + d // TOP_K)
            inv_ref[dest_ref[j * INVERT_UNROLL + d]] = code
        return c

    lax.fori_loop(0, TOK_TILE * TOP_K // INVERT_UNROLL, body, 0)


def _invert(dest_flat, default_inv):
    n_pairs = TOK_TILE * TOP_K
    assert INVERT_UNROLL % TOP_K == 0 and n_pairs % INVERT_UNROLL == 0
    return pl.pallas_call(
        _invert_kernel,
        grid=(dest_flat.shape[0] // n_pairs,),
        in_specs=[
            pl.BlockSpec((n_pairs,), lambda i: (i,), memory_space=pltpu.SMEM),
            pl.BlockSpec(memory_space=pl.ANY),
        ],
        out_specs=pl.BlockSpec(memory_space=pltpu.SMEM),
        out_shape=jax.ShapeDtypeStruct(default_inv.shape, jnp.int32),
        scratch_shapes=[pltpu.SemaphoreType.DMA(())],
        compiler_params=pltpu.CompilerParams(dimension_semantics=("arbitrary",)),
        name="invert",
    )(dest_flat, default_inv)


def _experts_kernel(blk_e, n_valid, inv, hn_ref, wg_ref, bg_ref, wu_ref, bu_ref, wd_ref, bd_ref, yp_ref,
                    wg_bf, wu_bf, wd_bf, xbuf0, xbuf1, xbuf2, ybuf0, ybuf1, ybuf2, zeros, gsem, ssem, zsem,
                    *, n_tok):
    i = pl.program_id(0)
    nv = n_valid[0]
    phase = lax.rem(i, RING)
    xbuf = (xbuf0, xbuf1, xbuf2)
    ybuf = (ybuf0, ybuf1, ybuf2)
    plane_rows = n_tok + TRASH_TOK

    def tile_rows(row, n=1):
        return pl.ds(pl.multiple_of(row * ROW_TILES, ROW_TILES), n * ROW_TILES)

    def gather(block, s):
        base = block * ROW_BLOCK
        for r in range(ROW_BLOCK):
            tok = inv[base + r] & PAIR_TOK_MASK
            pltpu.make_async_copy(
                hn_ref.at[tile_rows(tok)], xbuf[s].at[tile_rows(r)], gsem.at[s]).start(priority=0)

    def scatter(block, s):
        base = block * ROW_BLOCK
        for r in range(ROW_BLOCK):
            pair = inv[base + r]
            row = lax.shift_right_logical(pair, PAIR_TOK_BITS) * plane_rows + (pair & PAIR_TOK_MASK)
            pltpu.make_async_copy(
                ybuf[s].at[tile_rows(r)], yp_ref.at[tile_rows(row)], ssem.at[s]).start(priority=1)

    def wait_gather(s):
        pltpu.make_async_copy(hn_ref.at[tile_rows(0, ROW_BLOCK)], xbuf[s], gsem.at[s]).wait()

    def wait_scatter(s):
        pltpu.make_async_copy(ybuf[s], yp_ref.at[tile_rows(0, ROW_BLOCK)], ssem.at[s]).wait()

    @pl.when(i == 0)
    def _():
        zeros[...] = jnp.zeros_like(zeros)
        for k in range(TOP_K):
            for part in range(TRASH_TOK // ROW_BLOCK):
                first = k * plane_rows + n_tok + part * ROW_BLOCK
                cp = pltpu.make_async_copy(zeros, yp_ref.at[tile_rows(first, ROW_BLOCK)], zsem)
                cp.start()
                cp.wait()
        gather(0, 0)
        gather(jnp.minimum(1, nv - 1), 1)

    e = blk_e[i]
    prev = blk_e[jnp.maximum(i - 1, 0)]

    @pl.when(((i == 0) | (e != prev)) & (i < nv))
    def _():
        wg_bf[...] = wg_ref[0].astype(BF16)
        wu_bf[...] = wu_ref[0].astype(BF16)
        wd_bf[...] = wd_ref[0].astype(BF16)

    def step(slot, scatter_previous):
        ahead, behind = (slot + 2) % RING, (slot - 1) % RING
        wait_gather(slot)
        gather(jnp.minimum(i + 2, nv - 1), ahead)
        if scatter_previous:
            scatter(i - 1, behind)
        x = jnp.concatenate(
            [xbuf[slot][pl.ds(c, ROW_BLOCK, stride=ROW_TILES), :] for c in range(ROW_TILES)], axis=1).astype(BF16)
        gl = jnp.minimum(_dot(x, wg_bf[...]) + bg_ref[0], SWIGLU_LIMIT)
        ul = jnp.clip(_dot(x, wu_bf[...]) + bu_ref[0], -SWIGLU_LIMIT, SWIGLU_LIMIT)
        act = (ul + 1.0) * (gl * _sigmoid(SWIGLU_ALPHA * gl))
        y = _dot(act.astype(BF16), wd_bf[...]) + bd_ref[0]
        for c in range(ROW_TILES):
            ybuf[slot][pl.ds(c, ROW_BLOCK, stride=ROW_TILES), :] = y[:, c * LANES:(c + 1) * LANES]

    def finish(slot):
        ahead, behind = (slot + 2) % RING, (slot - 1) % RING
        scatter(i, slot)
        wait_gather((slot + 1) % RING)
        wait_gather(ahead)

        @pl.when(i >= 2)
        def _():
            wait_scatter((slot - 2) % RING)

        @pl.when(i >= 1)
        def _():
            wait_scatter(behind)

        wait_scatter(slot)

    @pl.when((i == 0) & (i < nv))
    def _():
        step(0, False)

    for slot in range(RING):
        mine = (phase == slot) & (i < nv)

        @pl.when(mine & (i >= RING))
        def _():
            wait_scatter(slot)

        @pl.when(mine & (i > 0))
        def _():
            step(slot, True)

        @pl.when(mine & (i == nv - 1))
        def _():
            finish(slot)


def _experts(blk_e, n_valid, inv, hn_all, w_gate, b_gate, w_up, b_up, w_down, b_down, *, n_tok):
    n_blocks = blk_e.shape[0]
    wsel = lambda i, be, nv, iv: (be[i], 0, 0)
    d_ff = w_gate.shape[-1]
    grid_spec = pltpu.PrefetchScalarGridSpec(
        num_scalar_prefetch=3,
        grid=(n_blocks,),
        in_specs=[
            pl.BlockSpec(memory_space=pl.ANY),
            pl.BlockSpec((1, D_MODEL, d_ff), wsel),
            pl.BlockSpec((1, 1, d_ff), wsel),
            pl.BlockSpec((1, D_MODEL, d_ff), wsel),
            pl.BlockSpec((1, 1, d_ff), wsel),
            pl.BlockSpec((1, d_ff, D_MODEL), wsel),
            pl.BlockSpec((1, 1, D_MODEL), wsel),
        ],
        out_specs=pl.BlockSpec(memory_space=pl.ANY),
        scratch_shapes=[
            pltpu.VMEM((D_MODEL, d_ff), BF16),
            pltpu.VMEM((D_MODEL, d_ff), BF16),
            pltpu.VMEM((d_ff, D_MODEL), BF16),
        ] + [pltpu.VMEM((ROW_BLOCK * ROW_TILES, LANES), F32)] * (2 * RING) + [
            pltpu.VMEM((ROW_BLOCK * ROW_TILES, LANES), F32),
            pltpu.SemaphoreType.DMA((RING,)),
            pltpu.SemaphoreType.DMA((RING,)),
            pltpu.SemaphoreType.DMA(()),
        ],
    )
    return pl.pallas_call(
        functools.partial(_experts_kernel, n_tok=n_tok),
        grid_spec=grid_spec,
        out_shape=jax.ShapeDtypeStruct((TOP_K * (n_tok + TRASH_TOK) * ROW_TILES, LANES), F32),
        compiler_params=pltpu.CompilerParams(
            dimension_semantics=("arbitrary",), vmem_limit_bytes=VMEM_LIMIT),
        name="experts",
    )(blk_e, n_valid, inv, hn_all, w_gate, b_gate, w_up, b_up, w_down, b_down)


def _combine_kernel(gate_ref, h_ref, fw_ref, *refs):
    y_ref = refs[TOP_K]
    gate = gate_ref[...]
    acc = h_ref[...]
    for k in range(TOP_K):
        y_k = jnp.concatenate(
            [refs[k][pl.ds(c, TOK_TILE, stride=ROW_TILES), :] for c in range(ROW_TILES)], axis=1)
        acc = acc + gate[:, k:k + 1] * y_k
    y_ref[...] = acc * lax.rsqrt(jnp.mean(acc * acc, axis=-1, keepdims=True) + EPS) * fw_ref[...]


def _combine(gates, h_group, final_w, y_pairs, *, tile0, plane_rows):
    n_tiles = h_group.shape[0] // TOK_TILE
    plane_tiles = plane_rows // TOK_TILE

    def plane(k):
        return pl.BlockSpec((TOK_TILE * ROW_TILES, LANES), lambda i: (k * plane_tiles + tile0 + i, 0))

    return pl.pallas_call(
        _combine_kernel,
        grid=(n_tiles,),
        in_specs=[
            pl.BlockSpec((TOK_TILE, LANES), lambda i: (tile0 + i, 0)),
            pl.BlockSpec((TOK_TILE, D_MODEL), lambda i: (i, 0)),
            pl.BlockSpec((1, D_MODEL), lambda i: (0, 0)),
        ] + [plane(k) for k in range(TOP_K)],
        out_specs=pl.BlockSpec((TOK_TILE, D_MODEL), lambda i: (i, 0)),
        out_shape=jax.ShapeDtypeStruct((n_tiles * TOK_TILE, D_MODEL), F32),
        compiler_params=pltpu.CompilerParams(dimension_semantics=("arbitrary",)),
        name="combine",
    )(gates, h_group, final_w, *([y_pairs] * TOP_K))


def _split_w_in(w):
    o_a = QKV_WIDTH
    o_g = o_a + 2 * DN_HEADS
    pad = jnp.zeros((w.shape[0], LANES - 2 * DN_HEADS), BF16)
    w_ab = jnp.concatenate([w[:, o_a:o_g].astype(BF16), pad], axis=1)
    return w[:, :o_a].astype(BF16), w[:, o_g:].astype(BF16), w_ab


def _lane_row(v, fill=0.0):
    return jnp.concatenate([v.astype(F32), jnp.full((LANES - v.shape[0],), fill, F32)]).reshape(1, LANES)


def kernel(x_prompt, x_sample, mem_prompt, state_dn, state_dn_conv, state_sc_conv, cache_mem_k, cache_mem_v, w_in, dn_conv_w, dn_A_log, dn_dt_bias, dn_norm_w, sc_conv_w, mem_norm_w, w_mem_kv, w_br, w_o, norm1_w, norm2_w, w_router, b_router, w_gate, b_gate, w_up, b_up, w_down, b_down, final_norm_w):
    assert w_in.shape[0] == 1, "one layer"
    bp, tp, _ = x_prompt.shape
    bs, ts, _ = x_sample.shape
    n_p, n_s = bp * tp, bs * ts
    n_tok = n_p + n_s
    assert n_p % TOK_TILE == 0 and n_s % TOK_TILE == 0 and tp % MIX_TILE == 0

    weights = (
        *_split_w_in(w_in[0]),
        w_br[0].astype(BF16),
        w_o[0].astype(BF16),
        norm1_w[0].reshape(1, D_MODEL),
        norm2_w[0].reshape(1, D_MODEL),
        dn_conv_w[0],
        sc_conv_w[0],
        _lane_row(dn_A_log[0]),
        _lane_row(dn_dt_bias[0]),
        dn_norm_w[0].reshape(1, DN_HEAD_DIM),
        jnp.concatenate([w_router[0], jnp.zeros((D_MODEL, LANES - N_EXPERTS), F32)], axis=1),
        _lane_row(b_router[0], NEG_BIG),
    )

    mk2d, mv2d = _memkv(mem_prompt.reshape(bp * MEM_LEN, D_MODEL), mem_norm_w[0].reshape(1, D_MODEL),
                        w_mem_kv[0].astype(BF16))

    assert n_tok + TRASH_TOK <= PAIR_TOK_MASK + 1 and (n_tok + TRASH_TOK) % TOK_TILE == 0
    hn_all = jnp.zeros(((n_tok + TRASH_TOK) * ROW_TILES, LANES), F32)
    h_s, hn_all, lg_s, s_dnc, s_dns, s_scc = _mixer(
        x_sample.reshape(n_s, D_MODEL), state_dn_conv[0], state_dn[0], state_sc_conv[0],
        cache_mem_k[0].reshape(bs, MEM_LEN, MEM_WIDTH), cache_mem_v[0].reshape(bs, MEM_LEN, MEM_WIDTH),
        weights, hn_all, n_seq=bs, seq_len=ts, nb=bs, tt=ts, row0=n_p)
    h_p, hn_all, lg_p, p_dnc, p_dns, p_scc = _mixer(
        x_prompt.reshape(n_p, D_MODEL),
        jnp.zeros((bp, DN_CONV - 1, QKV_WIDTH), F32),
        jnp.zeros((bp, DN_HEADS, DN_HEAD_DIM, DN_HEAD_DIM), F32),
        jnp.zeros((bp, SC_CONV - 1, SC_WIDTH), F32),
        mk2d.reshape(bp, MEM_LEN, MEM_WIDTH), mv2d.reshape(bp, MEM_LEN, MEM_WIDTH),
        weights, hn_all, n_seq=bp, seq_len=tp, nb=1, tt=MIX_TILE, row0=0)

    idx, gates, rank, counts = _router(jnp.concatenate([lg_p, lg_s], axis=0))
    counts = counts[0, :N_EXPERTS]
    n_blk_e = (counts + ROW_BLOCK - 1) // ROW_BLOCK
    blk_end = jnp.cumsum(n_blk_e)
    blk_start = blk_end - n_blk_e
    row_start = blk_start * ROW_BLOCK
    experts = jnp.arange(N_EXPERTS, dtype=jnp.int32)
    start_of = jnp.sum(jnp.where(idx[:, :TOP_K, None] == experts, row_start, 0), axis=-1)
    dest = (start_of + rank[:, :TOP_K]).reshape(n_tok * TOP_K)

    n_blocks = (n_tok * TOP_K) // ROW_BLOCK + N_EXPERTS
    bi = jnp.arange(n_blocks, dtype=jnp.int32)
    n_valid = blk_end[-1].astype(jnp.int32)
    bclip = jnp.minimum(bi, n_valid - 1)
    blk_e = jnp.sum((bclip[:, None] >= blk_end[None, :]).astype(jnp.int32), axis=1)
    n_valid = n_valid.reshape(1)

    slots = jnp.arange(n_blocks * ROW_BLOCK, dtype=jnp.int32)
    default_inv = n_tok + (slots & (TRASH_TOK - 1))
    inv = _invert(dest.astype(jnp.int32), default_inv)

    y_pairs = _experts(blk_e, n_valid, inv, hn_all,
                       w_gate[0], b_gate[0].reshape(N_EXPERTS, 1, -1), w_up[0], b_up[0].reshape(N_EXPERTS, 1, -1),
                       w_down[0], b_down[0].reshape(N_EXPERTS, 1, -1), n_tok=n_tok)
    fw = final_norm_w.reshape(1, D_MODEL)
    y_p = _combine(gates, h_p, fw, y_pairs, tile0=0, plane_rows=n_tok + TRASH_TOK)
    y_s = _combine(gates, h_s, fw, y_pairs, tile0=n_p // TOK_TILE, plane_rows=n_tok + TRASH_TOK)

    return (y_p.reshape(bp, tp, D_MODEL), y_s.reshape(bs, ts, D_MODEL),
            p_dns[None], p_dnc[None], p_scc[None],
            mk2d.reshape(1, bp, MEM_LEN, MEM_HEADS, MEM_HEAD_DIM), mv2d.reshape(1, bp, MEM_LEN, MEM_HEADS, MEM_HEAD_DIM),
            s_dns[None], s_dnc[None], s_scc[None])
```

```python
import functools

import jax
import jax.numpy as jnp
from jax import lax
from jax.experimental import pallas as pl
from jax.experimental.pallas import tpu as pltpu

F32 = jnp.float32
BF16 = jnp.bfloat16

D_MODEL = 1024
CHUNK = 64
EPS = 1e-6
DN_HEADS = 4
DN_HEAD_DIM = 128
DN_WIDTH = DN_HEADS * DN_HEAD_DIM
QKV_WIDTH = 3 * DN_WIDTH
DN_CONV = 4
SC_WIDTH = 256
SC_CONV = 3
MEM_LEN = 256
MEM_HEADS = 4
MEM_HEAD_DIM = 64
MEM_WIDTH = MEM_HEADS * MEM_HEAD_DIM
N_EXPERTS = 32
TOP_K = 4
SWIGLU_ALPHA = 1.702
SWIGLU_LIMIT = 7.0

LANES = 128
SUBLANES = 8
ROW_TILES = D_MODEL // LANES
assert ROW_TILES == SUBLANES
CONV_PAD = 8

OFF_QKV = 0
OFF_DNG = OFF_QKV + QKV_WIDTH
OFF_SC = OFF_DNG + DN_WIDTH
OFF_MQ = OFF_SC + 3 * SC_WIDTH
OFF_GATE = OFF_MQ + MEM_WIDTH
OFF_AB = OFF_GATE + 3 * D_MODEL
IN_PERM_WIDTH = OFF_AB + LANES

MIX_TILE = 512
TOK_TILE = 256
ROW_BLOCK = 256
INVERT_UNROLL = 32
TRASH_TOK = 2 * ROW_BLOCK
RING = 3
PAIR_TOK_BITS = 16
PAIR_TOK_MASK = (1 << PAIR_TOK_BITS) - 1
VMEM_LIMIT = 56 * 1024 * 1024
NEG_BIG = -1e30


def _dot(a, b):
    return jnp.dot(a, b, preferred_element_type=F32)


def _dot_nt(a, b):
    return lax.dot_general(a, b, (((1,), (1,)), ((), ())), preferred_element_type=F32)


def _dot_tn(a, b):
    return lax.dot_general(a, b, (((0,), (0,)), ((), ())), preferred_element_type=F32)


def _sigmoid(x):
    return 1.0 / (1.0 + jnp.exp(-x))


def _softplus(x):
    return jnp.maximum(x, 0.0) + jnp.log1p(jnp.exp(-jnp.abs(x)))


def _for_each(n, body):
    if n == 1:
        body(0)
    else:
        def step(i, carry):
            body(i)
            return carry
        lax.fori_loop(0, n, step, 0)


def _memkv_kernel(mem_ref, nw_ref, w_ref, k_ref, v_ref):
    x = mem_ref[...]
    xn = x * lax.rsqrt(jnp.mean(x * x, axis=-1, keepdims=True) + EPS) * nw_ref[...]
    kv = _dot(xn.astype(BF16), w_ref[...])
    k_ref[...] = kv[:, :MEM_WIDTH]
    v_ref[...] = kv[:, MEM_WIDTH:]


def _memkv(mem2d, norm_w, w_kv_bf16):
    rows = mem2d.shape[0]
    grid = rows // MEM_LEN
    return pl.pallas_call(
        _memkv_kernel,
        grid=(grid,),
        in_specs=[
            pl.BlockSpec((MEM_LEN, D_MODEL), lambda i: (i, 0)),
            pl.BlockSpec((1, D_MODEL), lambda i: (0, 0)),
            pl.BlockSpec((D_MODEL, 2 * MEM_WIDTH), lambda i: (0, 0)),
        ],
        out_specs=[
            pl.BlockSpec((MEM_LEN, MEM_WIDTH), lambda i: (i, 0)),
            pl.BlockSpec((MEM_LEN, MEM_WIDTH), lambda i: (i, 0)),
        ],
        out_shape=[jax.ShapeDtypeStruct((rows, MEM_WIDTH), F32)] * 2,
        name="memkv",
    )(mem2d, norm_w, w_kv_bf16)


def _unit_lower_inverse(a, eye, size):
    inv = eye - a
    power = a
    span = 2
    while span < size:
        power = _dot(power, power)
        inv = _dot(inv, eye + power)
        span *= 2
    return inv


def _mixer_kernel(x_ref, dnc_in, dns_in, scc_in, mk_ref, mv_ref, w_qkv, w_rest, w_ab, w_br, w_o, n1_ref, n2_ref,
                  dcw_ref, scw_ref,
                  alog_ref, dtb_ref, dnw_ref, wr_ref, br_ref, hn_all_ref,
                  h_ref, hn_ref, lg_ref, dnc_out, dns_out, scc_out,
                  xp, scp, q_s, k_s, v_s, gb_s, o_s, mq_s, ysc_s, ymem_s, wv_s, wk_s, qk_s, qd_s, kd_s, cd_s,
                  *, nb, tt, chunk):
    t_idx = pl.program_id(1)
    rows = nb * tt
    n_chunk = tt // chunk

    @pl.when(t_idx == 0)
    def _():
        xp[:, CONV_PAD - (DN_CONV - 1):CONV_PAD, :] = dnc_in[...]
        scp[:, CONV_PAD - (SC_CONV - 1):CONV_PAD, :] = scc_in[...]
        dns_out[...] = dns_in[...]

    x = x_ref[...]
    xn = (x * lax.rsqrt(jnp.mean(x * x, axis=-1, keepdims=True) + EPS) * n1_ref[...]).astype(BF16)

    def proj(off, width):
        if off >= OFF_AB:
            w = w_ab[:, off - OFF_AB:off - OFF_AB + width]
        elif off >= OFF_DNG:
            w = w_rest[:, off - OFF_DNG:off - OFF_DNG + width]
        else:
            w = w_qkv[:, off:off + width]
        return _dot(xn, w)

    qkv_pre = proj(OFF_QKV, QKV_WIDTH)
    for s in range(nb):
        xp[s, CONV_PAD:CONV_PAD + tt, :] = qkv_pre[s * tt:(s + 1) * tt, :]
    sc = proj(OFF_SC, 3 * SC_WIDTH)
    sc_b = sc[:, :SC_WIDTH]
    sc_ch = sc[:, SC_WIDTH:2 * SC_WIDTH] * sc[:, 2 * SC_WIDTH:]
    for s in range(nb):
        scp[s, CONV_PAD:CONV_PAD + tt, :] = sc_ch[s * tt:(s + 1) * tt, :]
    mq_s[...] = proj(OFF_MQ, MEM_WIDTH)

    ab = proj(OFF_AB, LANES)
    lane = lax.broadcasted_iota(jnp.int32, (rows, LANES), 1)
    g_log = -jnp.exp(alog_ref[...]) * _softplus(ab + dtb_ref[...])
    gb_s[...] = jnp.where(lane < DN_HEADS, g_log, _sigmoid(ab))

    def conv_seq(s):
        base = CONV_PAD - (DN_CONV - 1)
        acc = dcw_ref[0:1, :] * xp[s, pl.ds(base, tt), :]
        for j in range(1, DN_CONV):
            acc = acc + dcw_ref[j:j + 1, :] * xp[s, pl.ds(base + j, tt), :]
        act = acc * _sigmoid(acc)
        r0 = pl.multiple_of(s * tt, tt)
        for hd in range(DN_HEADS):
            lo = hd * DN_HEAD_DIM
            qh = act[:, lo:lo + DN_HEAD_DIM]
            kh = act[:, DN_WIDTH + lo:DN_WIDTH + lo + DN_HEAD_DIM]
            q_s[pl.ds(r0, tt), lo:lo + DN_HEAD_DIM] = (
                qh * lax.rsqrt(jnp.sum(qh * qh, axis=-1, keepdims=True) + EPS) * (DN_HEAD_DIM ** -0.5))
            k_s[pl.ds(r0, tt), lo:lo + DN_HEAD_DIM] = (
                kh * lax.rsqrt(jnp.sum(kh * kh, axis=-1, keepdims=True) + EPS))
        v_s[pl.ds(r0, tt), :] = act[:, 2 * DN_WIDTH:]
        tail = xp[s, pl.ds(tt + base, DN_CONV - 1), :]
        dnc_out[s] = tail
        xp[s, pl.ds(base, DN_CONV - 1), :] = tail

        base2 = CONV_PAD - (SC_CONV - 1)
        acc2 = scw_ref[0:1, :] * scp[s, pl.ds(base2, tt), :]
        for j in range(1, SC_CONV):
            acc2 = acc2 + scw_ref[j:j + 1, :] * scp[s, pl.ds(base2 + j, tt), :]
        ysc_s[pl.ds(r0, tt), :] = acc2
        tail2 = scp[s, pl.ds(tt + base2, SC_CONV - 1), :]
        scc_out[s] = tail2
        scp[s, pl.ds(base2, SC_CONV - 1), :] = tail2

    _for_each(nb, conv_seq)

    hl = DN_HEADS * chunk
    shift = chunk.bit_length() - 1
    ri = lax.broadcasted_iota(jnp.int32, (hl, hl), 0)
    ci = lax.broadcasted_iota(jnp.int32, (hl, hl), 1)
    same_head = lax.shift_right_logical(ri, shift) == lax.shift_right_logical(ci, shift)
    causal = same_head & (ri >= ci)
    strict = same_head & (ri > ci)
    eye = (ri == ci).astype(F32)
    ri1 = lax.broadcasted_iota(jnp.int32, (chunk, chunk), 0)
    ci1 = lax.broadcasted_iota(jnp.int32, (chunk, chunk), 1)
    tril = (ri1 >= ci1).astype(F32)
    triu = (ri1 <= ci1).astype(F32)

    def stack_heads(ref, r0):
        return jnp.concatenate(
            [ref[r0:r0 + chunk, hd * DN_HEAD_DIM:(hd + 1) * DN_HEAD_DIM] for hd in range(DN_HEADS)], axis=0)

    for c in range(nb * n_chunk):
        r0 = c * chunk
        gb = gb_s[r0:r0 + chunk, :]
        cum = _dot(tril, gb)
        cum_t = _dot_tn(gb, triu)
        cc = jnp.concatenate([cum[:, hd:hd + 1] for hd in range(DN_HEADS)], axis=0)
        cr = jnp.concatenate([cum_t[hd:hd + 1, :] for hd in range(DN_HEADS)], axis=1)
        beta = jnp.concatenate([gb[:, DN_HEADS + hd:DN_HEADS + hd + 1] for hd in range(DN_HEADS)], axis=0)
        c_last = jnp.concatenate(
            [jnp.broadcast_to(cum[chunk - 1:chunk, hd:hd + 1], (chunk, 1)) for hd in range(DN_HEADS)], axis=0)
        qst, kst, vst = stack_heads(q_s, r0), stack_heads(k_s, r0), stack_heads(v_s, r0)
        decay = jnp.where(causal, jnp.exp(jnp.where(causal, cc - cr, 0.0)), 0.0)
        a_mat = jnp.where(strict, decay * _dot_nt(kst, kst), 0.0) * beta
        t_inv = _unit_lower_inverse(a_mat, eye, chunk)
        e_cum = jnp.exp(cc)
        w = _dot(t_inv, jnp.concatenate([beta * vst, (beta * e_cum) * kst], axis=1))
        wv_s[c] = w[:, :DN_HEAD_DIM]
        wk_s[c] = w[:, DN_HEAD_DIM:]
        qk_s[c] = _dot_nt(qst, kst) * decay
        qd_s[c] = e_cum * qst
        kd_s[c] = jnp.exp(c_last - cc) * kst
        cd_s[c] = jnp.broadcast_to(jnp.exp(c_last), (hl, DN_HEAD_DIM))

    for c in range(nb * n_chunk):
        s = c // n_chunk
        r0 = c * chunk
        states, us = [], []
        for hd in range(DN_HEADS):
            hr = slice(hd * chunk, (hd + 1) * chunk)
            state = dns_out[s, hd]
            states.append(state)
            us.append(wv_s[c, hr, :] - _dot(wk_s[c, hr, :], state))
        o_intra = _dot(qk_s[c], jnp.concatenate(us, axis=0))
        for hd in range(DN_HEADS):
            hr = slice(hd * chunk, (hd + 1) * chunk)
            lo = hd * DN_HEAD_DIM
            o_s[r0:r0 + chunk, lo:lo + DN_HEAD_DIM] = _dot(qd_s[c, hr, :], states[hd]) + o_intra[hr, :]
            dns_out[s, hd] = (cd_s[c, hd * chunk:hd * chunk + 1, :] * states[hd]
                              + _dot_tn(kd_s[c, hr, :], us[hd]))

    def attn_seq(s):
        r0 = pl.multiple_of(s * tt, tt)
        mq = mq_s[pl.ds(r0, tt), :]
        for hd in range(MEM_HEADS):
            lo = hd * MEM_HEAD_DIM
            qh = mq[:, lo:lo + MEM_HEAD_DIM].astype(BF16)
            kh = mk_ref[s, :, lo:lo + MEM_HEAD_DIM].astype(BF16)
            vh = mv_ref[s, :, lo:lo + MEM_HEAD_DIM].astype(BF16)
            sc_h = _dot_nt(qh, kh) * (MEM_HEAD_DIM ** -0.5)
            p = jnp.exp(sc_h - jnp.max(sc_h, axis=-1, keepdims=True))
            denom = jnp.sum(p, axis=-1, keepdims=True)
            ymem_s[pl.ds(r0, tt), lo:lo + MEM_HEAD_DIM] = _dot(p.astype(BF16), vh) / denom

    _for_each(nb, attn_seq)

    dn_gate = proj(OFF_DNG, DN_WIDTH)
    o_all = o_s[...]
    y_heads = []
    for hd in range(DN_HEADS):
        lo = hd * DN_HEAD_DIM
        oh = o_all[:, lo:lo + DN_HEAD_DIM]
        oh = oh * lax.rsqrt(jnp.mean(oh * oh, axis=-1, keepdims=True) + EPS) * dnw_ref[...]
        gh = dn_gate[:, lo:lo + DN_HEAD_DIM]
        y_heads.append(oh * (gh * _sigmoid(gh)))
    y_dn = jnp.concatenate(y_heads, axis=-1).astype(BF16)
    y_sc = (sc_b * ysc_s[...]).astype(BF16)
    y_mem = ymem_s[...].astype(BF16)

    half = D_MODEL // 2
    merged = []
    for lo in (0, half):
        m = _sigmoid(proj(OFF_GATE + lo, half)) * _dot(y_dn, w_br[0:DN_WIDTH, lo:lo + half])
        m = m + (_sigmoid(proj(OFF_GATE + D_MODEL + lo, half))
                 * _dot(y_sc, w_br[DN_WIDTH:DN_WIDTH + SC_WIDTH, lo:lo + half]))
        m = m + (_sigmoid(proj(OFF_GATE + 2 * D_MODEL + lo, half))
                 * _dot(y_mem, w_br[DN_WIDTH + SC_WIDTH:, lo:lo + half]))
        merged.append(m.astype(BF16))
    h = x + _dot(jnp.concatenate(merged, axis=1), w_o[...])
    h_ref[...] = h
    hn = h * lax.rsqrt(jnp.mean(h * h, axis=-1, keepdims=True) + EPS) * n2_ref[...]
    for c in range(ROW_TILES):
        hn_ref[pl.ds(c, rows, stride=ROW_TILES), :] = hn[:, c * LANES:(c + 1) * LANES]
    lg_ref[...] = _dot(hn, wr_ref[...]) + br_ref[...]


def _mixer(x2d, dnc_in, dns_in, scc_in, mk, mv, weights, hn_all, *, n_seq, seq_len, nb, tt, row0):
    chunk = CHUNK if seq_len % CHUNK == 0 else seq_len
    rows = nb * tt
    n_t = seq_len // tt
    total_rows = n_seq * seq_len
    n_chunks = rows // chunk
    hl = DN_HEADS * chunk
    grid = (n_seq // nb, n_t)
    const = lambda b, t: (0, 0)
    seq3 = lambda b, t: (b, 0, 0)
    assert row0 % rows == 0

    def tok(b, t):
        return (b * n_t + t, 0)

    def tok_all(b, t):
        return (row0 // rows + b * n_t + t, 0)

    (w_qkv, w_rest, w_ab, w_br, w_o, n1, n2, dcw, scw, alog, dtb, dnw, wr, br) = weights
    in_specs = [
        pl.BlockSpec((rows, D_MODEL), tok),
        pl.BlockSpec((nb, DN_CONV - 1, QKV_WIDTH), seq3),
        pl.BlockSpec((nb, DN_HEADS, DN_HEAD_DIM, DN_HEAD_DIM), lambda b, t: (b, 0, 0, 0)),
        pl.BlockSpec((nb, SC_CONV - 1, SC_WIDTH), seq3),
        pl.BlockSpec((nb, MEM_LEN, MEM_WIDTH), seq3),
        pl.BlockSpec((nb, MEM_LEN, MEM_WIDTH), seq3),
        pl.BlockSpec(w_qkv.shape, const, pipeline_mode=pl.Buffered(1)),
        pl.BlockSpec(w_rest.shape, const, pipeline_mode=pl.Buffered(1)),
        pl.BlockSpec(w_ab.shape, const, pipeline_mode=pl.Buffered(1)),
        pl.BlockSpec(w_br.shape, const, pipeline_mode=pl.Buffered(1)),
        pl.BlockSpec(w_o.shape, const, pipeline_mode=pl.Buffered(1)),
        pl.BlockSpec(n1.shape, const),
        pl.BlockSpec(n2.shape, const),
        pl.BlockSpec(dcw.shape, const),
        pl.BlockSpec(scw.shape, const),
        pl.BlockSpec(alog.shape, const),
        pl.BlockSpec(dtb.shape, const),
        pl.BlockSpec(dnw.shape, const),
        pl.BlockSpec(wr.shape, const),
        pl.BlockSpec(br.shape, const),
        pl.BlockSpec(memory_space=pl.ANY),
    ]
    out_shape = [
        jax.ShapeDtypeStruct((total_rows, D_MODEL), F32),
        jax.ShapeDtypeStruct(hn_all.shape, F32),
        jax.ShapeDtypeStruct((total_rows, LANES), F32),
        jax.ShapeDtypeStruct((n_seq, DN_CONV - 1, QKV_WIDTH), F32),
        jax.ShapeDtypeStruct((n_seq, DN_HEADS, DN_HEAD_DIM, DN_HEAD_DIM), F32),
        jax.ShapeDtypeStruct((n_seq, SC_CONV - 1, SC_WIDTH), F32),
    ]
    out_specs = [
        pl.BlockSpec((rows, D_MODEL), tok),
        pl.BlockSpec((rows * ROW_TILES, LANES), tok_all),
        pl.BlockSpec((rows, LANES), tok),
        pl.BlockSpec((nb, DN_CONV - 1, QKV_WIDTH), seq3),
        pl.BlockSpec((nb, DN_HEADS, DN_HEAD_DIM, DN_HEAD_DIM), lambda b, t: (b, 0, 0, 0)),
        pl.BlockSpec((nb, SC_CONV - 1, SC_WIDTH), seq3),
    ]
    args = [x2d, dnc_in, dns_in, scc_in, mk, mv, w_qkv, w_rest, w_ab, w_br, w_o, n1, n2, dcw, scw, alog, dtb, dnw, wr,
            br, hn_all]
    scratch = [
        pltpu.VMEM((nb, CONV_PAD + tt, QKV_WIDTH), F32),
        pltpu.VMEM((nb, CONV_PAD + tt, SC_WIDTH), F32),
        pltpu.VMEM((rows, DN_WIDTH), F32),
        pltpu.VMEM((rows, DN_WIDTH), F32),
        pltpu.VMEM((rows, DN_WIDTH), F32),
        pltpu.VMEM((rows, LANES), F32),
        pltpu.VMEM((rows, DN_WIDTH), F32),
        pltpu.VMEM((rows, MEM_WIDTH), F32),
        pltpu.VMEM((rows, SC_WIDTH), F32),
        pltpu.VMEM((rows, MEM_WIDTH), F32),
        pltpu.VMEM((n_chunks, hl, DN_HEAD_DIM), F32),
        pltpu.VMEM((n_chunks, hl, DN_HEAD_DIM), F32),
        pltpu.VMEM((n_chunks, hl, hl), F32),
        pltpu.VMEM((n_chunks, hl, DN_HEAD_DIM), F32),
        pltpu.VMEM((n_chunks, hl, DN_HEAD_DIM), F32),
        pltpu.VMEM((n_chunks, hl, DN_HEAD_DIM), F32),
    ]
    return pl.pallas_call(
        functools.partial(_mixer_kernel, nb=nb, tt=tt, chunk=chunk),
        grid=grid,
        in_specs=in_specs,
        out_specs=out_specs,
        out_shape=out_shape,
        scratch_shapes=scratch,
        input_output_aliases={len(args) - 1: 1},
        compiler_params=pltpu.CompilerParams(
            dimension_semantics=("arbitrary", "arbitrary"), vmem_limit_bytes=VMEM_LIMIT),
        name="mixer",
    )(*args)


def _router_kernel(lg_ref, idx_ref, gate_ref, rank_ref, cnt_ref, carry):
    i = pl.program_id(0)

    @pl.when(i == 0)
    def _():
        carry[...] = jnp.zeros_like(carry)

    work = lg_ref[...]
    tm = work.shape[0]
    lane = lax.broadcasted_iota(jnp.int32, (tm, LANES), 1).astype(F32)
    idxs, vals = [], []
    for _ in range(TOP_K):
        m = jnp.max(work, axis=-1, keepdims=True)
        ik = jnp.min(jnp.where(work == m, lane, float(LANES)), axis=-1, keepdims=True)
        idxs.append(ik)
        vals.append(m)
        work = jnp.where(lane == ik, -jnp.inf, work)
    exps = [jnp.exp(v - vals[0]) for v in vals]
    denom = exps[0] + exps[1] + exps[2] + exps[3]
    hot = jnp.zeros((tm, LANES), F32)
    for ik in idxs:
        hot = hot + (lane == ik).astype(F32)
    ri = lax.broadcasted_iota(jnp.int32, (tm, tm), 0)
    ci = lax.broadcasted_iota(jnp.int32, (tm, tm), 1)
    before = (ri > ci).astype(BF16)
    prefix = _dot(before, hot.astype(BF16)) + carry[...]
    idx_out = jnp.zeros((tm, LANES), F32)
    gate_out = jnp.zeros((tm, LANES), F32)
    rank_out = jnp.zeros((tm, LANES), F32)
    for k in range(TOP_K):
        rk = jnp.sum(jnp.where(lane == idxs[k], prefix, 0.0), axis=-1, keepdims=True)
        idx_out = jnp.where(lane == k, idxs[k], idx_out)
        gate_out = jnp.where(lane == k, exps[k] / denom, gate_out)
        rank_out = jnp.where(lane == k, rk, rank_out)
    idx_ref[...] = idx_out.astype(jnp.int32)
    gate_ref[...] = gate_out
    rank_ref[...] = rank_out.astype(jnp.int32)
    carry[...] = carry[...] + jnp.sum(hot, axis=0, keepdims=True)
    cnt_ref[...] = carry[...].astype(jnp.int32)


def _router(logits):
    n_tok = logits.shape[0]
    tile = lambda i: (i, 0)
    return pl.pallas_call(
        _router_kernel,
        grid=(n_tok // TOK_TILE,),
        in_specs=[pl.BlockSpec((TOK_TILE, LANES), tile)],
        out_specs=[pl.BlockSpec((TOK_TILE, LANES), tile)] * 3 + [pl.BlockSpec((1, LANES), lambda i: (0, 0))],
        out_shape=[
            jax.ShapeDtypeStruct((n_tok, LANES), jnp.int32),
            jax.ShapeDtypeStruct((n_tok, LANES), F32),
            jax.ShapeDtypeStruct((n_tok, LANES), jnp.int32),
            jax.ShapeDtypeStruct((1, LANES), jnp.int32),
        ],
        scratch_shapes=[pltpu.VMEM((1, LANES), F32)],
        compiler_params=pltpu.CompilerParams(dimension_semantics=("arbitrary",)),
        name="router",
    )(logits)


def _invert_kernel(dest_ref, default_ref, inv_ref, sem):
    i = pl.program_id(0)

    @pl.when(i == 0)
    def _():
        cp = pltpu.make_async_copy(default_ref, inv_ref, sem)
        cp.start()
        cp.wait()

    tok_per_trip = INVERT_UNROLL // TOP_K

    def body(j, c):
        tok0 = i * TOK_TILE + j * tok_per_trip
        for d in range(INVERT_UNROLL):
            code = tok0 + (((d % TOP_K) << PAIR_TOK_BITS) + d // TOP_K)
            inv_ref[dest_ref[j * INVERT_UNROLL + d]] = code
        return c

    lax.fori_loop(0, TOK_TILE * TOP_K // INVERT_UNROLL, body, 0)


def _invert(dest_flat, default_inv):
    n_pairs = TOK_TILE * TOP_K
    assert INVERT_UNROLL % TOP_K == 0 and n_pairs % INVERT_UNROLL == 0
    return pl.pallas_call(
        _invert_kernel,
        grid=(dest_flat.shape[0] // n_pairs,),
        in_specs=[
            pl.BlockSpec((n_pairs,), lambda i: (i,), memory_space=pltpu.SMEM),
            pl.BlockSpec(memory_space=pl.ANY),
        ],
        out_specs=pl.BlockSpec(memory_space=pltpu.SMEM),
        out_shape=jax.ShapeDtypeStruct(default_inv.shape, jnp.int32),
        scratch_shapes=[pltpu.SemaphoreType.DMA(())],
        compiler_params=pltpu.CompilerParams(dimension_semantics=("arbitrary",)),
        name="invert",
    )(dest_flat, default_inv)


def _experts_kernel(blk_e, n_valid, inv, hn_ref, wg_ref, bg_ref, wu_ref, bu_ref, wd_ref, bd_ref, yp_ref,
                    wg_bf, wu_bf, wd_bf, xbuf0, xbuf1, xbuf2, ybuf0, ybuf1, ybuf2, zeros, gsem, ssem, zsem,
                    *, n_tok):
    i = pl.program_id(0)
    nv = n_valid[0]
    phase = lax.rem(i, RING)
    xbuf = (xbuf0, xbuf1, xbuf2)
    ybuf = (ybuf0, ybuf1, ybuf2)
    plane_rows = n_tok + TRASH_TOK

    def tile_rows(row, n=1):
        return pl.ds(pl.multiple_of(row * ROW_TILES, ROW_TILES), n * ROW_TILES)

    def gather(block, s):
        base = block * ROW_BLOCK
        for r in range(ROW_BLOCK):
            tok = inv[base + r] & PAIR_TOK_MASK
            pltpu.make_async_copy(
                hn_ref.at[tile_rows(tok)], xbuf[s].at[tile_rows(r)], gsem.at[s]).start(priority=0)

    def scatter(block, s):
        base = block * ROW_BLOCK
        for r in range(ROW_BLOCK):
            pair = inv[base + r]
            row = lax.shift_right_logical(pair, PAIR_TOK_BITS) * plane_rows + (pair & PAIR_TOK_MASK)
            pltpu.make_async_copy(
                ybuf[s].at[tile_rows(r)], yp_ref.at[tile_rows(row)], ssem.at[s]).start(priority=1)

    def wait_gather(s):
        pltpu.make_async_copy(hn_ref.at[tile_rows(0, ROW_BLOCK)], xbuf[s], gsem.at[s]).wait()

    def wait_scatter(s):
        pltpu.make_async_copy(ybuf[s], yp_ref.at[tile_rows(0, ROW_BLOCK)], ssem.at[s]).wait()

    @pl.when(i == 0)
    def _():
        zeros[...] = jnp.zeros_like(zeros)
        for k in range(TOP_K):
            for part in range(TRASH_TOK // ROW_BLOCK):
                first = k * plane_rows + n_tok + part * ROW_BLOCK
                cp = pltpu.make_async_copy(zeros, yp_ref.at[tile_rows(first, ROW_BLOCK)], zsem)
                cp.start()
                cp.wait()
        gather(0, 0)
        gather(jnp.minimum(1, nv - 1), 1)

    e = blk_e[i]
    prev = blk_e[jnp.maximum(i - 1, 0)]

    @pl.when(((i == 0) | (e != prev)) & (i < nv))
    def _():
        wg_bf[...] = wg_ref[0].astype(BF16)
        wu_bf[...] = wu_ref[0].astype(BF16)
        wd_bf[...] = wd_ref[0].astype(BF16)

    def step(slot, scatter_previous):
        ahead, behind = (slot + 2) % RING, (slot - 1) % RING
        wait_gather(slot)
        gather(jnp.minimum(i + 2, nv - 1), ahead)
        if scatter_previous:
            scatter(i - 1, behind)
        x = jnp.concatenate(
            [xbuf[slot][pl.ds(c, ROW_BLOCK, stride=ROW_TILES), :] for c in range(ROW_TILES)], axis=1).astype(BF16)
        acts = []
        half = wg_bf.shape[1] // 2
        for lo in (0, half):
            gl = jnp.minimum(_dot(x, wg_bf[:, lo:lo + half]) + bg_ref[0][:, lo:lo + half], SWIGLU_LIMIT)
            ul = jnp.clip(_dot(x, wu_bf[:, lo:lo + half]) + bu_ref[0][:, lo:lo + half], -SWIGLU_LIMIT, SWIGLU_LIMIT)
            acts.append(((ul + 1.0) * (gl * _sigmoid(SWIGLU_ALPHA * gl))).astype(BF16))
        y = _dot(jnp.concatenate(acts, axis=1), wd_bf[...]) + bd_ref[0]
        for c in range(ROW_TILES):
            ybuf[slot][pl.ds(c, ROW_BLOCK, stride=ROW_TILES), :] = y[:, c * LANES:(c + 1) * LANES]

    def finish(slot):
        ahead, behind = (slot + 2) % RING, (slot - 1) % RING
        scatter(i, slot)
        wait_gather((slot + 1) % RING)
        wait_gather(ahead)

        @pl.when(i >= 2)
        def _():
            wait_scatter((slot - 2) % RING)

        @pl.when(i >= 1)
        def _():
            wait_scatter(behind)

        wait_scatter(slot)

    @pl.when((i == 0) & (i < nv))
    def _():
        step(0, False)

    for slot in range(RING):
        mine = (phase == slot) & (i < nv)

        @pl.when(mine & (i >= RING))
        def _():
            wait_scatter(slot)

        @pl.when(mine & (i > 0))
        def _():
            step(slot, True)

        @pl.when(mine & (i == nv - 1))
        def _():
            finish(slot)


def _experts(blk_e, n_valid, inv, hn_all, w_gate, b_gate, w_up, b_up, w_down, b_down, *, n_tok):
    n_blocks = blk_e.shape[0]
    wsel = lambda i, be, nv, iv: (be[i], 0, 0)
    d_ff = w_gate.shape[-1]
    grid_spec = pltpu.PrefetchScalarGridSpec(
        num_scalar_prefetch=3,
        grid=(n_blocks,),
        in_specs=[
            pl.BlockSpec(memory_space=pl.ANY),
            pl.BlockSpec((1, D_MODEL, d_ff), wsel),
            pl.BlockSpec((1, 1, d_ff), wsel),
            pl.BlockSpec((1, D_MODEL, d_ff), wsel),
            pl.BlockSpec((1, 1, d_ff), wsel),
            pl.BlockSpec((1, d_ff, D_MODEL), wsel),
            pl.BlockSpec((1, 1, D_MODEL), wsel),
        ],
        out_specs=pl.BlockSpec(memory_space=pl.ANY),
        scratch_shapes=[
            pltpu.VMEM((D_MODEL, d_ff), BF16),
            pltpu.VMEM((D_MODEL, d_ff), BF16),
            pltpu.VMEM((d_ff, D_MODEL), BF16),
        ] + [pltpu.VMEM((ROW_BLOCK * ROW_TILES, LANES), F32)] * (2 * RING) + [
            pltpu.VMEM((ROW_BLOCK * ROW_TILES, LANES), F32),
            pltpu.SemaphoreType.DMA((RING,)),
            pltpu.SemaphoreType.DMA((RING,)),
            pltpu.SemaphoreType.DMA(()),
        ],
    )
    return pl.pallas_call(
        functools.partial(_experts_kernel, n_tok=n_tok),
        grid_spec=grid_spec,
        out_shape=jax.ShapeDtypeStruct((TOP_K * (n_tok + TRASH_TOK) * ROW_TILES, LANES), F32),
        compiler_params=pltpu.CompilerParams(
            dimension_semantics=("arbitrary",), vmem_limit_bytes=VMEM_LIMIT),
        name="experts",
    )(blk_e, n_valid, inv, hn_all, w_gate, b_gate, w_up, b_up, w_down, b_down)


def _combine_kernel(gate_ref, h_ref, fw_ref, *refs):
    y_ref = refs[TOP_K]
    gate = gate_ref[...]
    acc = h_ref[...]
    for k in range(TOP_K):
        y_k = jnp.concatenate(
            [refs[k][pl.ds(c, TOK_TILE, stride=ROW_TILES), :] for c in range(ROW_TILES)], axis=1)
        acc = acc + gate[:, k:k + 1] * y_k
    y_ref[...] = acc * lax.rsqrt(jnp.mean(acc * acc, axis=-1, keepdims=True) + EPS) * fw_ref[...]


def _combine(gates, h_group, final_w, y_pairs, *, tile0, plane_rows):
    n_tiles = h_group.shape[0] // TOK_TILE
    plane_tiles = plane_rows // TOK_TILE

    def plane(k):
        return pl.BlockSpec((TOK_TILE * ROW_TILES, LANES), lambda i: (k * plane_tiles + tile0 + i, 0))

    return pl.pallas_call(
        _combine_kernel,
        grid=(n_tiles,),
        in_specs=[
            pl.BlockSpec((TOK_TILE, LANES), lambda i: (tile0 + i, 0)),
            pl.BlockSpec((TOK_TILE, D_MODEL), lambda i: (i, 0)),
            pl.BlockSpec((1, D_MODEL), lambda i: (0, 0)),
        ] + [plane(k) for k in range(TOP_K)],
        out_specs=pl.BlockSpec((TOK_TILE, D_MODEL), lambda i: (i, 0)),
        out_shape=jax.ShapeDtypeStruct((n_tiles * TOK_TILE, D_MODEL), F32),
        compiler_params=pltpu.CompilerParams(dimension_semantics=("arbitrary",)),
        name="combine",
    )(gates, h_group, final_w, *([y_pairs] * TOP_K))


def _split_w_in(w):
    o_a = QKV_WIDTH
    o_g = o_a + 2 * DN_HEADS
    pad = jnp.zeros((w.shape[0], LANES - 2 * DN_HEADS), BF16)
    w_ab = jnp.concatenate([w[:, o_a:o_g].astype(BF16), pad], axis=1)
    return w[:, :o_a].astype(BF16), w[:, o_g:].astype(BF16), w_ab


def _lane_row(v, fill=0.0):
    return jnp.concatenate([v.astype(F32), jnp.full((LANES - v.shape[0],), fill, F32)]).reshape(1, LANES)


def kernel(x_prompt, x_sample, mem_prompt, state_dn, state_dn_conv, state_sc_conv, cache_mem_k, cache_mem_v, w_in, dn_conv_w, dn_A_log, dn_dt_bias, dn_norm_w, sc_conv_w, mem_norm_w, w_mem_kv, w_br, w_o, norm1_w, norm2_w, w_router, b_router, w_gate, b_gate, w_up, b_up, w_down, b_down, final_norm_w):
    assert w_in.shape[0] == 1, "one layer"
    bp, tp, _ = x_prompt.shape
    bs, ts, _ = x_sample.shape
    n_p, n_s = bp * tp, bs * ts
    n_tok = n_p + n_s
    assert n_p % TOK_TILE == 0 and n_s % TOK_TILE == 0 and tp % MIX_TILE == 0

    weights = (
        *_split_w_in(w_in[0]),
        w_br[0].astype(BF16),
        w_o[0].astype(BF16),
        norm1_w[0].reshape(1, D_MODEL),
        norm2_w[0].reshape(1, D_MODEL),
        dn_conv_w[0],
        sc_conv_w[0],
        _lane_row(dn_A_log[0]),
        _lane_row(dn_dt_bias[0]),
        dn_norm_w[0].reshape(1, DN_HEAD_DIM),
        jnp.concatenate([w_router[0], jnp.zeros((D_MODEL, LANES - N_EXPERTS), F32)], axis=1),
        _lane_row(b_router[0], NEG_BIG),
    )

    mk2d, mv2d = _memkv(mem_prompt.reshape(bp * MEM_LEN, D_MODEL), mem_norm_w[0].reshape(1, D_MODEL),
                        w_mem_kv[0].astype(BF16))

    assert n_tok + TRASH_TOK <= PAIR_TOK_MASK + 1 and (n_tok + TRASH_TOK) % TOK_TILE == 0
    hn_all = jnp.zeros(((n_tok + TRASH_TOK) * ROW_TILES, LANES), F32)
    h_s, hn_all, lg_s, s_dnc, s_dns, s_scc = _mixer(
        x_sample.reshape(n_s, D_MODEL), state_dn_conv[0], state_dn[0], state_sc_conv[0],
        cache_mem_k[0].reshape(bs, MEM_LEN, MEM_WIDTH), cache_mem_v[0].reshape(bs, MEM_LEN, MEM_WIDTH),
        weights, hn_all, n_seq=bs, seq_len=ts, nb=bs, tt=ts, row0=n_p)
    h_p, hn_all, lg_p, p_dnc, p_dns, p_scc = _mixer(
        x_prompt.reshape(n_p, D_MODEL),
        jnp.zeros((bp, DN_CONV - 1, QKV_WIDTH), F32),
        jnp.zeros((bp, DN_HEADS, DN_HEAD_DIM, DN_HEAD_DIM), F32),
        jnp.zeros((bp, SC_CONV - 1, SC_WIDTH), F32),
        mk2d.reshape(bp, MEM_LEN, MEM_WIDTH), mv2d.reshape(bp, MEM_LEN, MEM_WIDTH),
        weights, hn_all, n_seq=bp, seq_len=tp, nb=1, tt=MIX_TILE, row0=0)

    idx, gates, rank, counts = _router(jnp.concatenate([lg_p, lg_s], axis=0))
    counts = counts[0, :N_EXPERTS]
    n_blk_e = (counts + ROW_BLOCK - 1) // ROW_BLOCK
    blk_end = jnp.cumsum(n_blk_e)
    blk_start = blk_end - n_blk_e
    row_start = blk_start * ROW_BLOCK
    experts = jnp.arange(N_EXPERTS, dtype=jnp.int32)
    start_of = jnp.sum(jnp.where(idx[:, :TOP_K, None] == experts, row_start, 0), axis=-1)
    dest = (start_of + rank[:, :TOP_K]).reshape(n_tok * TOP_K)

    n_blocks = (n_tok * TOP_K) // ROW_BLOCK + N_EXPERTS
    bi = jnp.arange(n_blocks, dtype=jnp.int32)
    n_valid = blk_end[-1].astype(jnp.int32)
    bclip = jnp.minimum(bi, n_valid - 1)
    blk_e = jnp.sum((bclip[:, None] >= blk_end[None, :]).astype(jnp.int32), axis=1)
    n_valid = n_valid.reshape(1)

    slots = jnp.arange(n_blocks * ROW_BLOCK, dtype=jnp.int32)
    default_inv = n_tok + (slots & (TRASH_TOK - 1))
    inv = _invert(dest.astype(jnp.int32), default_inv)

    y_pairs = _experts(blk_e, n_valid, inv, hn_all,
                       w_gate[0], b_gate[0].reshape(N_EXPERTS, 1, -1), w_up[0], b_up[0].reshape(N_EXPERTS, 1, -1),
                       w_down[0], b_down[0].reshape(N_EXPERTS, 1, -1), n_tok=n_tok)
    fw = final_norm_w.reshape(1, D_MODEL)
    y_p = _combine(gates, h_p, fw, y_pairs, tile0=0, plane_rows=n_tok + TRASH_TOK)
    y_s = _combine(gates, h_s, fw, y_pairs, tile0=n_p // TOK_TILE, plane_rows=n_tok + TRASH_TOK)

    return (y_p.reshape(bp, tp, D_MODEL), y_s.reshape(bs, ts, D_MODEL),
            p_dns[None], p_dnc[None], p_scc[None],
            mk2d.reshape(1, bp, MEM_LEN, MEM_HEADS, MEM_HEAD_DIM), mv2d.reshape(1, bp, MEM_LEN, MEM_HEADS, MEM_HEAD_DIM),
            s_dns[None], s_dnc[None], s_scc[None])
```

```python
import functools

import jax
import jax.numpy as jnp
from jax import lax
from jax.experimental import pallas as pl
from jax.experimental.pallas import tpu as pltpu

F32 = jnp.float32
BF16 = jnp.bfloat16

D_MODEL = 1024
CHUNK = 64
EPS = 1e-6
DN_HEADS = 4
DN_HEAD_DIM = 128
DN_WIDTH = DN_HEADS * DN_HEAD_DIM
QKV_WIDTH = 3 * DN_WIDTH
DN_CONV = 4
SC_WIDTH = 256
SC_CONV = 3
MEM_LEN = 256
MEM_HEADS = 4
MEM_HEAD_DIM = 64
MEM_WIDTH = MEM_HEADS * MEM_HEAD_DIM
N_EXPERTS = 32
TOP_K = 4
SWIGLU_ALPHA = 1.702
SWIGLU_LIMIT = 7.0

LANES = 128
SUBLANES = 8
ROW_TILES = D_MODEL // LANES
assert ROW_TILES == SUBLANES
CONV_PAD = 8

OFF_QKV = 0
OFF_DNG = OFF_QKV + QKV_WIDTH
OFF_SC = OFF_DNG + DN_WIDTH
OFF_MQ = OFF_SC + 3 * SC_WIDTH
OFF_GATE = OFF_MQ + MEM_WIDTH
OFF_AB = OFF_GATE + 3 * D_MODEL
IN_PERM_WIDTH = OFF_AB + LANES

MIX_TILE = 512
TOK_TILE = 256
ROW_BLOCK = 256
INVERT_UNROLL = 32
TRASH_TOK = 2 * ROW_BLOCK
RING = 3
PAIR_TOK_BITS = 16
PAIR_TOK_MASK = (1 << PAIR_TOK_BITS) - 1
VMEM_LIMIT = 56 * 1024 * 1024
NEG_BIG = -1e30


def _dot(a, b):
    return jnp.dot(a, b, preferred_element_type=F32)


def _dot_nt(a, b):
    return lax.dot_general(a, b, (((1,), (1,)), ((), ())), preferred_element_type=F32)


def _dot_tn(a, b):
    return lax.dot_general(a, b, (((0,), (0,)), ((), ())), preferred_element_type=F32)


def _sigmoid(x):
    return 1.0 / (1.0 + jnp.exp(-x))


def _softplus(x):
    return jnp.maximum(x, 0.0) + jnp.log1p(jnp.exp(-jnp.abs(x)))


def _for_each(n, body):
    if n == 1:
        body(0)
    else:
        def step(i, carry):
            body(i)
            return carry
        lax.fori_loop(0, n, step, 0)


def _memkv_kernel(mem_ref, nw_ref, w_ref, k_ref, v_ref):
    x = mem_ref[...]
    xn = x * lax.rsqrt(jnp.mean(x * x, axis=-1, keepdims=True) + EPS) * nw_ref[...]
    kv = _dot(xn.astype(BF16), w_ref[...])
    k_ref[...] = kv[:, :MEM_WIDTH]
    v_ref[...] = kv[:, MEM_WIDTH:]


def _memkv(mem2d, norm_w, w_kv_bf16):
    rows = mem2d.shape[0]
    grid = rows // MEM_LEN
    return pl.pallas_call(
        _memkv_kernel,
        grid=(grid,),
        in_specs=[
            pl.BlockSpec((MEM_LEN, D_MODEL), lambda i: (i, 0)),
            pl.BlockSpec((1, D_MODEL), lambda i: (0, 0)),
            pl.BlockSpec((D_MODEL, 2 * MEM_WIDTH), lambda i: (0, 0)),
        ],
        out_specs=[
            pl.BlockSpec((MEM_LEN, MEM_WIDTH), lambda i: (i, 0)),
            pl.BlockSpec((MEM_LEN, MEM_WIDTH), lambda i: (i, 0)),
        ],
        out_shape=[jax.ShapeDtypeStruct((rows, MEM_WIDTH), F32)] * 2,
        name="memkv",
    )(mem2d, norm_w, w_kv_bf16)


def _unit_lower_inverse(a, eye, size):
    inv = eye - a
    power = a
    span = 2
    while span < size:
        power = _dot(power, power)
        inv = _dot(inv, eye + power)
        span *= 2
    return inv


def _mixer_kernel(x_ref, dnc_in, dns_in, scc_in, mk_ref, mv_ref, w_qkv, w_rest, w_ab, w_br, w_o, n1_ref, n2_ref,
                  dcw_ref, scw_ref,
                  alog_ref, dtb_ref, dnw_ref, wr_ref, br_ref, hn_all_ref,
                  h_ref, hn_ref, lg_ref, dnc_out, dns_out, scc_out,
                  xp, scp, q_s, k_s, v_s, gb_s, o_s, mq_s, ysc_s, ymem_s, wv_s, wk_s, qk_s, qd_s, kd_s, cd_s,
                  *, nb, tt, chunk):
    t_idx = pl.program_id(1)
    rows = nb * tt
    n_chunk = tt // chunk

    @pl.when(t_idx == 0)
    def _():
        xp[:, CONV_PAD - (DN_CONV - 1):CONV_PAD, :] = dnc_in[...]
        scp[:, CONV_PAD - (SC_CONV - 1):CONV_PAD, :] = scc_in[...]
        dns_out[...] = dns_in[...]

    x = x_ref[...]
    xn = (x * lax.rsqrt(jnp.mean(x * x, axis=-1, keepdims=True) + EPS) * n1_ref[...]).astype(BF16)

    def proj(off, width):
        if off >= OFF_AB:
            w = w_ab[:, off - OFF_AB:off - OFF_AB + width]
        elif off >= OFF_DNG:
            w = w_rest[:, off - OFF_DNG:off - OFF_DNG + width]
        else:
            w = w_qkv[:, off:off + width]
        return _dot(xn, w)

    qkv_pre = proj(OFF_QKV, QKV_WIDTH)
    for s in range(nb):
        xp[s, CONV_PAD:CONV_PAD + tt, :] = qkv_pre[s * tt:(s + 1) * tt, :]
    sc = proj(OFF_SC, 3 * SC_WIDTH)
    sc_b = sc[:, :SC_WIDTH]
    sc_ch = sc[:, SC_WIDTH:2 * SC_WIDTH] * sc[:, 2 * SC_WIDTH:]
    for s in range(nb):
        scp[s, CONV_PAD:CONV_PAD + tt, :] = sc_ch[s * tt:(s + 1) * tt, :]
    mq_s[...] = proj(OFF_MQ, MEM_WIDTH)

    ab = proj(OFF_AB, LANES)
    lane = lax.broadcasted_iota(jnp.int32, (rows, LANES), 1)
    g_log = -jnp.exp(alog_ref[...]) * _softplus(ab + dtb_ref[...])
    gb_s[...] = jnp.where(lane < DN_HEADS, g_log, _sigmoid(ab))

    def conv_seq(s):
        base = CONV_PAD - (DN_CONV - 1)
        acc = dcw_ref[0:1, :] * xp[s, pl.ds(base, tt), :]
        for j in range(1, DN_CONV):
            acc = acc + dcw_ref[j:j + 1, :] * xp[s, pl.ds(base + j, tt), :]
        act = acc * _sigmoid(acc)
        r0 = pl.multiple_of(s * tt, tt)
        for hd in range(DN_HEADS):
            lo = hd * DN_HEAD_DIM
            qh = act[:, lo:lo + DN_HEAD_DIM]
            kh = act[:, DN_WIDTH + lo:DN_WIDTH + lo + DN_HEAD_DIM]
            q_s[pl.ds(r0, tt), lo:lo + DN_HEAD_DIM] = (
                qh * lax.rsqrt(jnp.sum(qh * qh, axis=-1, keepdims=True) + EPS) * (DN_HEAD_DIM ** -0.5))
            k_s[pl.ds(r0, tt), lo:lo + DN_HEAD_DIM] = (
                kh * lax.rsqrt(jnp.sum(kh * kh, axis=-1, keepdims=True) + EPS))
        v_s[pl.ds(r0, tt), :] = act[:, 2 * DN_WIDTH:]
        tail = xp[s, pl.ds(tt + base, DN_CONV - 1), :]
        dnc_out[s] = tail
        xp[s, pl.ds(base, DN_CONV - 1), :] = tail

        base2 = CONV_PAD - (SC_CONV - 1)
        acc2 = scw_ref[0:1, :] * scp[s, pl.ds(base2, tt), :]
        for j in range(1, SC_CONV):
            acc2 = acc2 + scw_ref[j:j + 1, :] * scp[s, pl.ds(base2 + j, tt), :]
        ysc_s[pl.ds(r0, tt), :] = acc2
        tail2 = scp[s, pl.ds(tt + base2, SC_CONV - 1), :]
        scc_out[s] = tail2
        scp[s, pl.ds(base2, SC_CONV - 1), :] = tail2

    _for_each(nb, conv_seq)

    hl = DN_HEADS * chunk
    shift = chunk.bit_length() - 1
    ri = lax.broadcasted_iota(jnp.int32, (hl, hl), 0)
    ci = lax.broadcasted_iota(jnp.int32, (hl, hl), 1)
    same_head = lax.shift_right_logical(ri, shift) == lax.shift_right_logical(ci, shift)
    causal = same_head & (ri >= ci)
    strict = same_head & (ri > ci)
    eye = (ri == ci).astype(F32)
    ri1 = lax.broadcasted_iota(jnp.int32, (chunk, chunk), 0)
    ci1 = lax.broadcasted_iota(jnp.int32, (chunk, chunk), 1)
    tril = (ri1 >= ci1).astype(F32)
    triu = (ri1 <= ci1).astype(F32)

    def stack_heads(ref, r0):
        return jnp.concatenate(
            [ref[r0:r0 + chunk, hd * DN_HEAD_DIM:(hd + 1) * DN_HEAD_DIM] for hd in range(DN_HEADS)], axis=0)

    for c in range(nb * n_chunk):
        r0 = c * chunk
        gb = gb_s[r0:r0 + chunk, :]
        cum = _dot(tril, gb)
        cum_t = _dot_tn(gb, triu)
        cc = jnp.concatenate([cum[:, hd:hd + 1] for hd in range(DN_HEADS)], axis=0)
        cr = jnp.concatenate([cum_t[hd:hd + 1, :] for hd in range(DN_HEADS)], axis=1)
        beta = jnp.concatenate([gb[:, DN_HEADS + hd:DN_HEADS + hd + 1] for hd in range(DN_HEADS)], axis=0)
        c_last = jnp.concatenate(
            [jnp.broadcast_to(cum[chunk - 1:chunk, hd:hd + 1], (chunk, 1)) for hd in range(DN_HEADS)], axis=0)
        qst, kst, vst = stack_heads(q_s, r0), stack_heads(k_s, r0), stack_heads(v_s, r0)
        decay = jnp.where(causal, jnp.exp(jnp.where(causal, cc - cr, 0.0)), 0.0)
        a_mat = jnp.where(strict, decay * _dot_nt(kst, kst), 0.0) * beta
        t_inv = _unit_lower_inverse(a_mat, eye, chunk)
        e_cum = jnp.exp(cc)
        w = _dot(t_inv, jnp.concatenate([beta * vst, (beta * e_cum) * kst], axis=1))
        wv_s[c] = w[:, :DN_HEAD_DIM]
        wk_s[c] = w[:, DN_HEAD_DIM:]
        qk_s[c] = _dot_nt(qst, kst) * decay
        qd_s[c] = e_cum * qst
        kd_s[c] = jnp.exp(c_last - cc) * kst
        cd_s[c] = jnp.broadcast_to(jnp.exp(c_last), (hl, DN_HEAD_DIM))

    for c in range(nb * n_chunk):
        s = c // n_chunk
        r0 = c * chunk
        states, us = [], []
        for hd in range(DN_HEADS):
            hr = slice(hd * chunk, (hd + 1) * chunk)
            state = dns_out[s, hd]
            states.append(state)
            us.append(wv_s[c, hr, :] - _dot(wk_s[c, hr, :], state))
        o_intra = _dot(qk_s[c], jnp.concatenate(us, axis=0))
        for hd in range(DN_HEADS):
            hr = slice(hd * chunk, (hd + 1) * chunk)
            lo = hd * DN_HEAD_DIM
            o_s[r0:r0 + chunk, lo:lo + DN_HEAD_DIM] = _dot(qd_s[c, hr, :], states[hd]) + o_intra[hr, :]
            dns_out[s, hd] = (cd_s[c, hd * chunk:hd * chunk + 1, :] * states[hd]
                              + _dot_tn(kd_s[c, hr, :], us[hd]))

    def attn_seq(s):
        r0 = pl.multiple_of(s * tt, tt)
        mq = mq_s[pl.ds(r0, tt), :]
        for hd in range(MEM_HEADS):
            lo = hd * MEM_HEAD_DIM
            qh = mq[:, lo:lo + MEM_HEAD_DIM].astype(BF16)
            kh = mk_ref[s, :, lo:lo + MEM_HEAD_DIM].astype(BF16)
            vh = mv_ref[s, :, lo:lo + MEM_HEAD_DIM].astype(BF16)
            sc_h = _dot_nt(qh, kh) * (MEM_HEAD_DIM ** -0.5)
            p = jnp.exp(sc_h - jnp.max(sc_h, axis=-1, keepdims=True))
            denom = jnp.sum(p, axis=-1, keepdims=True)
            ymem_s[pl.ds(r0, tt), lo:lo + MEM_HEAD_DIM] = _dot(p.astype(BF16), vh) / denom

    _for_each(nb, attn_seq)

    dn_gate = proj(OFF_DNG, DN_WIDTH)
    y_heads = []
    for hd in range(DN_HEADS):
        lo = hd * DN_HEAD_DIM
        oh = o_s[:, lo:lo + DN_HEAD_DIM]
        oh = oh * lax.rsqrt(jnp.mean(oh * oh, axis=-1, keepdims=True) + EPS) * dnw_ref[...]
        gh = dn_gate[:, lo:lo + DN_HEAD_DIM]
        y_heads.append(oh * (gh * _sigmoid(gh)))
    y_dn = jnp.concatenate(y_heads, axis=-1).astype(BF16)
    y_sc = (sc_b * ysc_s[...]).astype(BF16)
    y_mem = ymem_s[...].astype(BF16)

    merged = _sigmoid(proj(OFF_GATE, D_MODEL)) * _dot(y_dn, w_br[0:DN_WIDTH, :])
    merged = merged + _sigmoid(proj(OFF_GATE + D_MODEL, D_MODEL)) * _dot(y_sc, w_br[DN_WIDTH:DN_WIDTH + SC_WIDTH, :])
    merged = merged + _sigmoid(proj(OFF_GATE + 2 * D_MODEL, D_MODEL)) * _dot(y_mem, w_br[DN_WIDTH + SC_WIDTH:, :])
    h = x_ref[...] + _dot(merged.astype(BF16), w_o[...])
    h_ref[...] = h
    hn = h * lax.rsqrt(jnp.mean(h * h, axis=-1, keepdims=True) + EPS) * n2_ref[...]
    for c in range(ROW_TILES):
        hn_ref[pl.ds(c, rows, stride=ROW_TILES), :] = hn[:, c * LANES:(c + 1) * LANES]
    lg_ref[...] = _dot(hn, wr_ref[...]) + br_ref[...]


def _mixer(x2d, dnc_in, dns_in, scc_in, mk, mv, weights, hn_all, *, n_seq, seq_len, nb, tt, row0):
    chunk = CHUNK if seq_len % CHUNK == 0 else seq_len
    rows = nb * tt
    n_t = seq_len // tt
    total_rows = n_seq * seq_len
    n_chunks = rows // chunk
    hl = DN_HEADS * chunk
    grid = (n_seq // nb, n_t)
    const = lambda b, t: (0, 0)
    seq3 = lambda b, t: (b, 0, 0)
    assert row0 % rows == 0

    def tok(b, t):
        return (b * n_t + t, 0)

    def tok_all(b, t):
        return (row0 // rows + b * n_t + t, 0)

    (w_qkv, w_rest, w_ab, w_br, w_o, n1, n2, dcw, scw, alog, dtb, dnw, wr, br) = weights
    in_specs = [
        pl.BlockSpec((rows, D_MODEL), tok),
        pl.BlockSpec((nb, DN_CONV - 1, QKV_WIDTH), seq3),
        pl.BlockSpec((nb, DN_HEADS, DN_HEAD_DIM, DN_HEAD_DIM), lambda b, t: (b, 0, 0, 0)),
        pl.BlockSpec((nb, SC_CONV - 1, SC_WIDTH), seq3),
        pl.BlockSpec((nb, MEM_LEN, MEM_WIDTH), seq3),
        pl.BlockSpec((nb, MEM_LEN, MEM_WIDTH), seq3),
        pl.BlockSpec(w_qkv.shape, const, pipeline_mode=pl.Buffered(1)),
        pl.BlockSpec(w_rest.shape, const, pipeline_mode=pl.Buffered(1)),
        pl.BlockSpec(w_ab.shape, const, pipeline_mode=pl.Buffered(1)),
        pl.BlockSpec(w_br.shape, const, pipeline_mode=pl.Buffered(1)),
        pl.BlockSpec(w_o.shape, const, pipeline_mode=pl.Buffered(1)),
        pl.BlockSpec(n1.shape, const),
        pl.BlockSpec(n2.shape, const),
        pl.BlockSpec(dcw.shape, const),
        pl.BlockSpec(scw.shape, const),
        pl.BlockSpec(alog.shape, const),
        pl.BlockSpec(dtb.shape, const),
        pl.BlockSpec(dnw.shape, const),
        pl.BlockSpec(wr.shape, const),
        pl.BlockSpec(br.shape, const),
        pl.BlockSpec(memory_space=pl.ANY),
    ]
    out_shape = [
        jax.ShapeDtypeStruct((total_rows, D_MODEL), F32),
        jax.ShapeDtypeStruct(hn_all.shape, F32),
        jax.ShapeDtypeStruct((total_rows, LANES), F32),
        jax.ShapeDtypeStruct((n_seq, DN_CONV - 1, QKV_WIDTH), F32),
        jax.ShapeDtypeStruct((n_seq, DN_HEADS, DN_HEAD_DIM, DN_HEAD_DIM), F32),
        jax.ShapeDtypeStruct((n_seq, SC_CONV - 1, SC_WIDTH), F32),
    ]
    out_specs = [
        pl.BlockSpec((rows, D_MODEL), tok),
        pl.BlockSpec((rows * ROW_TILES, LANES), tok_all),
        pl.BlockSpec((rows, LANES), tok),
        pl.BlockSpec((nb, DN_CONV - 1, QKV_WIDTH), seq3),
        pl.BlockSpec((nb, DN_HEADS, DN_HEAD_DIM, DN_HEAD_DIM), lambda b, t: (b, 0, 0, 0)),
        pl.BlockSpec((nb, SC_CONV - 1, SC_WIDTH), seq3),
    ]
    args = [x2d, dnc_in, dns_in, scc_in, mk, mv, w_qkv, w_rest, w_ab, w_br, w_o, n1, n2, dcw, scw, alog, dtb, dnw, wr,
            br, hn_all]
    scratch = [
        pltpu.VMEM((nb, CONV_PAD + tt, QKV_WIDTH), F32),
        pltpu.VMEM((nb, CONV_PAD + tt, SC_WIDTH), F32),
        pltpu.VMEM((rows, DN_WIDTH), F32),
        pltpu.VMEM((rows, DN_WIDTH), F32),
        pltpu.VMEM((rows, DN_WIDTH), F32),
        pltpu.VMEM((rows, LANES), F32),
        pltpu.VMEM((rows, DN_WIDTH), F32),
        pltpu.VMEM((rows, MEM_WIDTH), F32),
        pltpu.VMEM((rows, SC_WIDTH), F32),
        pltpu.VMEM((rows, MEM_WIDTH), F32),
        pltpu.VMEM((n_chunks, hl, DN_HEAD_DIM), F32),
        pltpu.VMEM((n_chunks, hl, DN_HEAD_DIM), F32),
        pltpu.VMEM((n_chunks, hl, hl), F32),
        pltpu.VMEM((n_chunks, hl, DN_HEAD_DIM), F32),
        pltpu.VMEM((n_chunks, hl, DN_HEAD_DIM), F32),
        pltpu.VMEM((n_chunks, hl, DN_HEAD_DIM), F32),
    ]
    return pl.pallas_call(
        functools.partial(_mixer_kernel, nb=nb, tt=tt, chunk=chunk),
        grid=grid,
        in_specs=in_specs,
        out_specs=out_specs,
        out_shape=out_shape,
        scratch_shapes=scratch,
        input_output_aliases={len(args) - 1: 1},
        compiler_params=pltpu.CompilerParams(
            dimension_semantics=("arbitrary", "arbitrary"), vmem_limit_bytes=VMEM_LIMIT),
        name="mixer",
    )(*args)


def _router_kernel(lg_ref, idx_ref, gate_ref, rank_ref, cnt_ref, carry):
    i = pl.program_id(0)

    @pl.when(i == 0)
    def _():
        carry[...] = jnp.zeros_like(carry)

    work = lg_ref[...]
    tm = work.shape[0]
    lane = lax.broadcasted_iota(jnp.int32, (tm, LANES), 1).astype(F32)
    idxs, vals = [], []
    for _ in range(TOP_K):
        m = jnp.max(work, axis=-1, keepdims=True)
        ik = jnp.min(jnp.where(work == m, lane, float(LANES)), axis=-1, keepdims=True)
        idxs.append(ik)
        vals.append(m)
        work = jnp.where(lane == ik, -jnp.inf, work)
    exps = [jnp.exp(v - vals[0]) for v in vals]
    denom = exps[0] + exps[1] + exps[2] + exps[3]
    hot = jnp.zeros((tm, LANES), F32)
    for ik in idxs:
        hot = hot + (lane == ik).astype(F32)
    ri = lax.broadcasted_iota(jnp.int32, (tm, tm), 0)
    ci = lax.broadcasted_iota(jnp.int32, (tm, tm), 1)
    before = (ri > ci).astype(BF16)
    prefix = _dot(before, hot.astype(BF16)) + carry[...]
    idx_out = jnp.zeros((tm, LANES), F32)
    gate_out = jnp.zeros((tm, LANES), F32)
    rank_out = jnp.zeros((tm, LANES), F32)
    for k in range(TOP_K):
        rk = jnp.sum(jnp.where(lane == idxs[k], prefix, 0.0), axis=-1, keepdims=True)
        idx_out = jnp.where(lane == k, idxs[k], idx_out)
        gate_out = jnp.where(lane == k, exps[k] / denom, gate_out)
        rank_out = jnp.where(lane == k, rk, rank_out)
    idx_ref[...] = idx_out.astype(jnp.int32)
    gate_ref[...] = gate_out
    rank_ref[...] = rank_out.astype(jnp.int32)
    carry[...] = carry[...] + jnp.sum(hot, axis=0, keepdims=True)
    cnt_ref[...] = carry[...].astype(jnp.int32)


def _router(logits):
    n_tok = logits.shape[0]
    tile = lambda i: (i, 0)
    return pl.pallas_call(
        _router_kernel,
        grid=(n_tok // TOK_TILE,),
        in_specs=[pl.BlockSpec((TOK_TILE, LANES), tile)],
        out_specs=[pl.BlockSpec((TOK_TILE, LANES), tile)] * 3 + [pl.BlockSpec((1, LANES), lambda i: (0, 0))],
        out_shape=[
            jax.ShapeDtypeStruct((n_tok, LANES), jnp.int32),
            jax.ShapeDtypeStruct((n_tok, LANES), F32),
            jax.ShapeDtypeStruct((n_tok, LANES), jnp.int32),
            jax.ShapeDtypeStruct((1, LANES), jnp.int32),
        ],
        scratch_shapes=[pltpu.VMEM((1, LANES), F32)],
        compiler_params=pltpu.CompilerParams(dimension_semantics=("arbitrary",)),
        name="router",
    )(logits)


def _invert_kernel(dest_ref, default_ref, inv_ref, sem):
    i = pl.program_id(0)

    @pl.when(i == 0)
    def _():
        cp = pltpu.make_async_copy(default_ref, inv_ref, sem)
        cp.start()
        cp.wait()

    tok_per_trip = INVERT_UNROLL // TOP_K

    def body(j, c):
        tok0 = i * TOK_TILE + j * tok_per_trip
        for d in range(INVERT_UNROLL):
            code = tok0 + (((d % TOP_K) << PAIR_TOK_BITS) + d // TOP_K)
            inv_ref[dest_ref[j * INVERT_UNROLL + d]] = code
        return c

    lax.fori_loop(0, TOK_TILE * TOP_K // INVERT_UNROLL, body, 0)


def _invert(dest_flat, default_inv):
    n_pairs = TOK_TILE * TOP_K
    assert INVERT_UNROLL % TOP_K == 0 and n_pairs % INVERT_UNROLL == 0
    return pl.pallas_call(
        _invert_kernel,
        grid=(dest_flat.shape[0] // n_pairs,),
        in_specs=[
            pl.BlockSpec((n_pairs,), lambda i: (i,), memory_space=pltpu.SMEM),
            pl.BlockSpec(memory_space=pl.ANY),
        ],
        out_specs=pl.BlockSpec(memory_space=pltpu.SMEM),
        out_shape=jax.ShapeDtypeStruct(default_inv.shape, jnp.int32),
        scratch_shapes=[pltpu.SemaphoreType.DMA(())],
        compiler_params=pltpu.CompilerParams(dimension_semantics=("arbitrary",)),
        name="invert",
    )(dest_flat, default_inv)


def _experts_kernel(blk_e, n_valid, inv, hn_ref, wg_ref, bg_ref, wu_ref, bu_ref, wd_ref, bd_ref, yp_ref,
                    wg_bf, wu_bf, wd_bf, xbuf0, xbuf1, xbuf2, ybuf0, ybuf1, ybuf2, zeros, gsem, ssem, zsem,
                    *, n_tok):
    i = pl.program_id(0)
    nv = n_valid[0]
    phase = lax.rem(i, RING)
    xbuf = (xbuf0, xbuf1, xbuf2)
    ybuf = (ybuf0, ybuf1, ybuf2)
    plane_rows = n_tok + TRASH_TOK

    def tile_rows(row, n=1):
        return pl.ds(pl.multiple_of(row * ROW_TILES, ROW_TILES), n * ROW_TILES)

    def gather(block, s):
        base = block * ROW_BLOCK
        for r in range(ROW_BLOCK):
            tok = inv[base + r] & PAIR_TOK_MASK
            pltpu.make_async_copy(
                hn_ref.at[tile_rows(tok)], xbuf[s].at[tile_rows(r)], gsem.at[s]).start(priority=0)

    def scatter(block, s):
        base = block * ROW_BLOCK
        for r in range(ROW_BLOCK):
            pair = inv[base + r]
            row = lax.shift_right_logical(pair, PAIR_TOK_BITS) * plane_rows + (pair & PAIR_TOK_MASK)
            pltpu.make_async_copy(
                ybuf[s].at[tile_rows(r)], yp_ref.at[tile_rows(row)], ssem.at[s]).start(priority=1)

    def wait_gather(s):
        pltpu.make_async_copy(hn_ref.at[tile_rows(0, ROW_BLOCK)], xbuf[s], gsem.at[s]).wait()

    def wait_scatter(s):
        pltpu.make_async_copy(ybuf[s], yp_ref.at[tile_rows(0, ROW_BLOCK)], ssem.at[s]).wait()

    @pl.when(i == 0)
    def _():
        zeros[...] = jnp.zeros_like(zeros)
        for k in range(TOP_K):
            for part in range(TRASH_TOK // ROW_BLOCK):
                first = k * plane_rows + n_tok + part * ROW_BLOCK
                cp = pltpu.make_async_copy(zeros, yp_ref.at[tile_rows(first, ROW_BLOCK)], zsem)
                cp.start()
                cp.wait()
        gather(0, 0)
        gather(jnp.minimum(1, nv - 1), 1)

    e = blk_e[i]
    prev = blk_e[jnp.maximum(i - 1, 0)]

    @pl.when(((i == 0) | (e != prev)) & (i < nv))
    def _():
        wg_bf[...] = wg_ref[0].astype(BF16)
        wu_bf[...] = wu_ref[0].astype(BF16)
        wd_bf[...] = wd_ref[0].astype(BF16)

    def step(slot, scatter_previous):
        ahead, behind = (slot + 2) % RING, (slot - 1) % RING
        wait_gather(slot)
        gather(jnp.minimum(i + 2, nv - 1), ahead)
        if scatter_previous:
            scatter(i - 1, behind)
        x = jnp.concatenate(
            [xbuf[slot][pl.ds(c, ROW_BLOCK, stride=ROW_TILES), :] for c in range(ROW_TILES)], axis=1).astype(BF16)
        gl = jnp.minimum(_dot(x, wg_bf[...]) + bg_ref[0], SWIGLU_LIMIT)
        ul = jnp.clip(_dot(x, wu_bf[...]) + bu_ref[0], -SWIGLU_LIMIT, SWIGLU_LIMIT)
        act = (ul + 1.0) * (gl * _sigmoid(SWIGLU_ALPHA * gl))
        y = _dot(act.astype(BF16), wd_bf[...]) + bd_ref[0]
        for c in range(ROW_TILES):
            ybuf[slot][pl.ds(c, ROW_BLOCK, stride=ROW_TILES), :] = y[:, c * LANES:(c + 1) * LANES]

    def finish(slot):
        ahead, behind = (slot + 2) % RING, (slot - 1) % RING
        scatter(i, slot)
        wait_gather((slot + 1) % RING)
        wait_gather(ahead)

        @pl.when(i >= 2)
        def _():
            wait_scatter((slot - 2) % RING)

        @pl.when(i >= 1)
        def _():
            wait_scatter(behind)

        wait_scatter(slot)

    @pl.when((i == 0) & (i < nv))
    def _():
        step(0, False)

    for slot in range(RING):
        mine = (phase == slot) & (i < nv)

        @pl.when(mine & (i >= RING))
        def _():
            wait_scatter(slot)

        @pl.when(mine & (i > 0))
        def _():
            step(slot, True)

        @pl.when(mine & (i == nv - 1))
        def _():
            finish(slot)


def _experts(blk_e, n_valid, inv, hn_all, w_gate, b_gate, w_up, b_up, w_down, b_down, *, n_tok):
    n_blocks = blk_e.shape[0]
    wsel = lambda i, be, nv, iv: (be[i], 0, 0)
    d_ff = w_gate.shape[-1]
    grid_spec = pltpu.PrefetchScalarGridSpec(
        num_scalar_prefetch=3,
        grid=(n_blocks,),
        in_specs=[
            pl.BlockSpec(memory_space=pl.ANY),
            pl.BlockSpec((1, D_MODEL, d_ff), wsel),
            pl.BlockSpec((1, 1, d_ff), wsel),
            pl.BlockSpec((1, D_MODEL, d_ff), wsel),
            pl.BlockSpec((1, 1, d_ff), wsel),
            pl.BlockSpec((1, d_ff, D_MODEL), wsel),
            pl.BlockSpec((1, 1, D_MODEL), wsel),
        ],
        out_specs=pl.BlockSpec(memory_space=pl.ANY),
        scratch_shapes=[
            pltpu.VMEM((D_MODEL, d_ff), BF16),
            pltpu.VMEM((D_MODEL, d_ff), BF16),
            pltpu.VMEM((d_ff, D_MODEL), BF16),
        ] + [pltpu.VMEM((ROW_BLOCK * ROW_TILES, LANES), F32)] * (2 * RING) + [
            pltpu.VMEM((ROW_BLOCK * ROW_TILES, LANES), F32),
            pltpu.SemaphoreType.DMA((RING,)),
            pltpu.SemaphoreType.DMA((RING,)),
            pltpu.SemaphoreType.DMA(()),
        ],
    )
    return pl.pallas_call(
        functools.partial(_experts_kernel, n_tok=n_tok),
        grid_spec=grid_spec,
        out_shape=jax.ShapeDtypeStruct((TOP_K * (n_tok + TRASH_TOK) * ROW_TILES, LANES), F32),
        compiler_params=pltpu.CompilerParams(
            dimension_semantics=("arbitrary",), vmem_limit_bytes=VMEM_LIMIT),
        name="experts",
    )(blk_e, n_valid, inv, hn_all, w_gate, b_gate, w_up, b_up, w_down, b_down)


def _combine_kernel(gate_ref, h_ref, fw_ref, *refs):
    y_ref = refs[TOP_K]
    gate = gate_ref[...]
    acc = h_ref[...]
    for k in range(TOP_K):
        y_k = jnp.concatenate(
            [refs[k][pl.ds(c, TOK_TILE, stride=ROW_TILES), :] for c in range(ROW_TILES)], axis=1)
        acc = acc + gate[:, k:k + 1] * y_k
    y_ref[...] = acc * lax.rsqrt(jnp.mean(acc * acc, axis=-1, keepdims=True) + EPS) * fw_ref[...]


def _combine(gates, h_group, final_w, y_pairs, *, tile0, plane_rows):
    n_tiles = h_group.shape[0] // TOK_TILE
    plane_tiles = plane_rows // TOK_TILE

    def plane(k):
        return pl.BlockSpec((TOK_TILE * ROW_TILES, LANES), lambda i: (k * plane_tiles + tile0 + i, 0))

    return pl.pallas_call(
        _combine_kernel,
        grid=(n_tiles,),
        in_specs=[
            pl.BlockSpec((TOK_TILE, LANES), lambda i: (tile0 + i, 0)),
            pl.BlockSpec((TOK_TILE, D_MODEL), lambda i: (i, 0)),
            pl.BlockSpec((1, D_MODEL), lambda i: (0, 0)),
        ] + [plane(k) for k in range(TOP_K)],
        out_specs=pl.BlockSpec((TOK_TILE, D_MODEL), lambda i: (i, 0)),
        out_shape=jax.ShapeDtypeStruct((n_tiles * TOK_TILE, D_MODEL), F32),
        compiler_params=pltpu.CompilerParams(dimension_semantics=("arbitrary",)),
        name="combine",
    )(gates, h_group, final_w, *([y_pairs] * TOP_K))


def _split_w_in(w):
    o_a = QKV_WIDTH
    o_g = o_a + 2 * DN_HEADS
    pad = jnp.zeros((w.shape[0], LANES - 2 * DN_HEADS), BF16)
    w_ab = jnp.concatenate([w[:, o_a:o_g].astype(BF16), pad], axis=1)
    return w[:, :o_a].astype(BF16), w[:, o_g:].astype(BF16), w_ab


def _lane_row(v, fill=0.0):
    return jnp.concatenate([v.astype(F32), jnp.full((LANES - v.shape[0],), fill, F32)]).reshape(1, LANES)


def kernel(x_prompt, x_sample, mem_prompt, state_dn, state_dn_conv, state_sc_conv, cache_mem_k, cache_mem_v, w_in, dn_conv_w, dn_A_log, dn_dt_bias, dn_norm_w, sc_conv_w, mem_norm_w, w_mem_kv, w_br, w_o, norm1_w, norm2_w, w_router, b_router, w_gate, b_gate, w_up, b_up, w_down, b_down, final_norm_w):
    assert w_in.shape[0] == 1, "one layer"
    bp, tp, _ = x_prompt.shape
    bs, ts, _ = x_sample.shape
    n_p, n_s = bp * tp, bs * ts
    n_tok = n_p + n_s
    assert n_p % TOK_TILE == 0 and n_s % TOK_TILE == 0 and tp % MIX_TILE == 0

    weights = (
        *_split_w_in(w_in[0]),
        w_br[0].astype(BF16),
        w_o[0].astype(BF16),
        norm1_w[0].reshape(1, D_MODEL),
        norm2_w[0].reshape(1, D_MODEL),
        dn_conv_w[0],
        sc_conv_w[0],
        _lane_row(dn_A_log[0]),
        _lane_row(dn_dt_bias[0]),
        dn_norm_w[0].reshape(1, DN_HEAD_DIM),
        jnp.concatenate([w_router[0], jnp.zeros((D_MODEL, LANES - N_EXPERTS), F32)], axis=1),
        _lane_row(b_router[0], NEG_BIG),
    )

    mk2d, mv2d = _memkv(mem_prompt.reshape(bp * MEM_LEN, D_MODEL), mem_norm_w[0].reshape(1, D_MODEL),
                        w_mem_kv[0].astype(BF16))

    assert n_tok + TRASH_TOK <= PAIR_TOK_MASK + 1 and (n_tok + TRASH_TOK) % TOK_TILE == 0
    hn_all = jnp.zeros(((n_tok + TRASH_TOK) * ROW_TILES, LANES), F32)
    h_s, hn_all, lg_s, s_dnc, s_dns, s_scc = _mixer(
        x_sample.reshape(n_s, D_MODEL), state_dn_conv[0], state_dn[0], state_sc_conv[0],
        cache_mem_k[0].reshape(bs, MEM_LEN, MEM_WIDTH), cache_mem_v[0].reshape(bs, MEM_LEN, MEM_WIDTH),
        weights, hn_all, n_seq=bs, seq_len=ts, nb=bs, tt=ts, row0=n_p)
    h_p, hn_all, lg_p, p_dnc, p_dns, p_scc = _mixer(
        x_prompt.reshape(n_p, D_MODEL),
        jnp.zeros((bp, DN_CONV - 1, QKV_WIDTH), F32),
        jnp.zeros((bp, DN_HEADS, DN_HEAD_DIM, DN_HEAD_DIM), F32),
        jnp.zeros((bp, SC_CONV - 1, SC_WIDTH), F32),
        mk2d.reshape(bp, MEM_LEN, MEM_WIDTH), mv2d.reshape(bp, MEM_LEN, MEM_WIDTH),
        weights, hn_all, n_seq=bp, seq_len=tp, nb=1, tt=MIX_TILE, row0=0)

    idx, gates, rank, counts = _router(jnp.concatenate([lg_p, lg_s], axis=0))
    counts = counts[0, :N_EXPERTS]
    n_blk_e = (counts + ROW_BLOCK - 1) // ROW_BLOCK
    blk_end = jnp.cumsum(n_blk_e)
    blk_start = blk_end - n_blk_e
    row_start = blk_start * ROW_BLOCK
    experts = jnp.arange(N_EXPERTS, dtype=jnp.int32)
    start_of = jnp.sum(jnp.where(idx[:, :TOP_K, None] == experts, row_start, 0), axis=-1)
    dest = (start_of + rank[:, :TOP_K]).reshape(n_tok * TOP_K)

    n_blocks = (n_tok * TOP_K) // ROW_BLOCK + N_EXPERTS
    bi = jnp.arange(n_blocks, dtype=jnp.int32)
    n_valid = blk_end[-1].astype(jnp.int32)
    bclip = jnp.minimum(bi, n_valid - 1)
    blk_e = jnp.sum((bclip[:, None] >= blk_end[None, :]).astype(jnp.int32), axis=1)
    n_valid = n_valid.reshape(1)

    slots = jnp.arange(n_blocks * ROW_BLOCK, dtype=jnp.int32)
    default_inv = n_tok + (slots & (TRASH_TOK - 1))
    inv = _invert(dest.astype(jnp.int32), default_inv)

    y_pairs = _experts(blk_e, n_valid, inv, hn_all,
                       w_gate[0], b_gate[0].reshape(N_EXPERTS, 1, -1), w_up[0], b_up[0].reshape(N_EXPERTS, 1, -1),
                       w_down[0], b_down[0].reshape(N_EXPERTS, 1, -1), n_tok=n_tok)
    fw = final_norm_w.reshape(1, D_MODEL)
    y_p = _combine(gates, h_p, fw, y_pairs, tile0=0, plane_rows=n_tok + TRASH_TOK)
    y_s = _combine(gates, h_s, fw, y_pairs, tile0=n_p // TOK_TILE, plane_rows=n_tok + TRASH_TOK)

    return (y_p.reshape(bp, tp, D_MODEL), y_s.reshape(bs, ts, D_MODEL),
            p_dns[None], p_dnc[None], p_scc[None],
            mk2d.reshape(1, bp, MEM_LEN, MEM_HEADS, MEM_HEAD_DIM), mv2d.reshape(1, bp, MEM_LEN, MEM_HEADS, MEM_HEAD_DIM),
            s_dns[None], s_dnc[None], s_scc[None])
```
